```python
import jax
import jax.numpy as jnp
from jax import lax
import numpy as np

D_MODEL = 1024
BATCH = 2
SEQ = 16384
DEPTH = 2

POOL_WIDTH = 512
POOL_GROUPS = 4
POOL_WINDOWS = (2, 4, 8, 16)
SG_WIDTH = 512
SG_HEADS = 4
SG_CHUNK = 128
NSA_Q_HEADS = 8
NSA_KV_HEADS = 2
GROUP = NSA_Q_HEADS // NSA_KV_HEADS
HEAD_DIM = 64
Q_WIDTH = NSA_Q_HEADS * HEAD_DIM
KV_WIDTH = NSA_KV_HEADS * HEAD_DIM
ROT_DIM = HEAD_DIM // 4
ROPE_THETA = 500000.0
CMP_LEN = 32
CMP_STRIDE = 16
CMP_HIDDEN = 256
SLC_LEN = 64
SLC_TOP = 16
WINDOW = 512
Q_BLOCK = 128
NEG_INF = -1e30
FORCE_SCORE = 1e4
D_FF = 2816
N_EXPERTS = 8
TOP_K = 2
EXPERT_BLOCK = 512
PLE_DIM = 256
N_DENSE = (DEPTH + 1) // 2
N_MOE = DEPTH // 2
ALPHA = (2 * DEPTH) ** 0.25
BETA = (8 * DEPTH) ** -0.25

WIDTHS = (POOL_WIDTH, SG_WIDTH, SG_WIDTH, Q_WIDTH,
          KV_WIDTH, KV_WIDTH, KV_WIDTH, KV_WIDTH, KV_WIDTH, KV_WIDTH,
          NSA_Q_HEADS * 3, 3 * D_MODEL)
IN_WIDTH = sum(WIDTHS)
SPLIT_POINTS = tuple(int(v) for v in np.cumsum(WIDTHS)[:-1])

kernel_name = 'hybrid_pool_gmlp_nsa_moe_deepnorm'


def layer_norm(x, g, b, eps=1e-5):
    xf = x.astype(jnp.float32)
    mu = jnp.mean(xf, axis=-1, keepdims=True)
    var = jnp.mean(jnp.square(xf - mu), axis=-1, keepdims=True)
    return ((xf - mu) * lax.rsqrt(var + eps)).astype(x.dtype) * g + b


def partial_rope(x, positions):
    half = ROT_DIM // 2
    inv = ROPE_THETA ** (-jnp.arange(half, dtype=jnp.float32) / half)
    ang = positions.astype(jnp.float32)[:, :, None, None] * inv
    cos = jnp.cos(ang).astype(x.dtype)
    sin = jnp.sin(ang).astype(x.dtype)
    x1 = x[..., :half]
    x2 = x[..., half:ROT_DIM]
    return jnp.concatenate([x1 * cos - x2 * sin, x2 * cos + x1 * sin, x[..., ROT_DIM:]], axis=-1)


def swiglu(x, wg, wu, wd):
    return (jax.nn.silu(x @ wg) * (x @ wu)) @ wd


def pool_mixer(a, pool_w, pool_scale):
    B_, S_, _ = a.shape
    af = a.astype(jnp.float32)
    wmax = max(POOL_WINDOWS)
    cs = jnp.concatenate([jnp.zeros((B_, wmax, POOL_WIDTH), jnp.float32), jnp.cumsum(af, axis=1)], axis=1)
    gw = POOL_WIDTH // POOL_GROUPS
    outs = []
    for g, w in enumerate(POOL_WINDOWS):
        c = cs[:, :, g * gw:(g + 1) * gw]
        win_sum = c[:, wmax:] - c[:, wmax - w:wmax - w + S_]
        count = jnp.minimum(jnp.arange(1, S_ + 1), w).astype(jnp.float32)[None, :, None]
        outs.append(win_sum / count)
    d = (jnp.concatenate(outs, axis=-1) - af).astype(a.dtype).reshape(B_, S_, POOL_GROUPS, gw)
    y = jnp.einsum('bsgc,gcd->bsgd', d, pool_w).reshape(B_, S_, POOL_WIDTH)
    return y * pool_scale


def spatial_gating_mixer(u, v, ln_g, ln_b, sg_w, sg_b):
    B_, S_, _ = u.shape
    u = jax.nn.gelu(u)
    v = layer_norm(jax.nn.gelu(v), ln_g, ln_b)
    nc = S_ // SG_CHUNK
    hc = SG_WIDTH // SG_HEADS
    v = v.reshape(B_, nc, SG_CHUNK, SG_HEADS, hc)
    causal = jnp.tril(jnp.ones((SG_CHUNK, SG_CHUNK), dtype=bool))
    ws = jnp.where(causal[None], sg_w, 0.0)
    mixed = jnp.einsum('gts,bnsgc->bntgc', ws, v) + sg_b.T[None, None, :, :, None]
    return u * mixed.reshape(B_, S_, SG_WIDTH)


def compress(kv, pos_enc, w1, w2):
    B_, H_, S_, d = kv.shape
    ratio = CMP_LEN // CMP_STRIDE
    chunks = kv.reshape(B_, H_, S_ // CMP_STRIDE, CMP_STRIDE, d)
    n_cmp = S_ // CMP_STRIDE - ratio + 1
    blocks = jnp.concatenate([chunks[:, :, r:r + n_cmp] for r in range(ratio)], axis=3)
    blocks = (blocks + pos_enc).reshape(B_, H_, n_cmp, CMP_LEN * d)
    return jax.nn.gelu(blocks @ w1) @ w2


def nsa_attention(q, k_cmp, v_cmp, k_slc, v_slc, k_win, v_win, gates):
    B_, H_, G_, S_, d = q.shape
    n_cmp = k_cmp.shape[2]
    n_slc = S_ // SLC_LEN
    n_top = min(SLC_TOP, n_slc)
    ratio = CMP_LEN // CMP_STRIDE
    n_chunks = S_ // CMP_STRIDE
    k_blk = k_slc.reshape(B_, H_, n_slc, SLC_LEN, d)
    v_blk = v_slc.reshape(B_, H_, n_slc, SLC_LEN, d)
    pad_w = ((0, 0), (0, 0), (WINDOW, 0), (0, 0))
    k_pad = jnp.pad(k_win, pad_w)
    v_pad = jnp.pad(v_win, pad_w)
    cmp_end = jnp.arange(n_cmp) * CMP_STRIDE + CMP_LEN - 1
    blk_id = jnp.arange(n_slc)
    gather = jax.vmap(jax.vmap(lambda tbl, ix: tbl[ix]))

    def block(n):
        qs = n * Q_BLOCK
        t = qs + jnp.arange(Q_BLOCK)
        qb = lax.dynamic_slice_in_dim(q, qs, Q_BLOCK, axis=3)
        gb = lax.dynamic_slice_in_dim(gates, qs, Q_BLOCK, axis=3)
        s = jnp.einsum('bhgqd,bhkd->bhgqk', qb, k_cmp).astype(jnp.float32)
        valid = cmp_end[None, :] <= t[:, None]
        p_cmp = jax.nn.softmax(jnp.where(valid, s, NEG_INF), axis=-1) * valid
        o_cmp = jnp.einsum('bhgqk,bhkd->bhgqd', p_cmp.astype(q.dtype), v_cmp)
        imp = jnp.pad(p_cmp.sum(axis=2), ((0, 0), (0, 0), (0, 0), (ratio - 1, ratio - 1)))
        imp = sum(imp[..., r:r + n_chunks] for r in range(ratio))
        imp = imp.reshape(B_, H_, Q_BLOCK, n_slc, SLC_LEN // CMP_STRIDE).sum(-1)
        cur = t[:, None] // SLC_LEN
        forced = (blk_id == 0) | (blk_id == cur) | (blk_id == cur - 1)
        score = jnp.where(forced, FORCE_SCORE, imp)
        score = jnp.where(blk_id <= cur, score, -1.0)
        _, idx = lax.top_k(score, n_top)
        k_sel = gather(k_blk, idx)
        v_sel = gather(v_blk, idx)
        s = jnp.einsum('bhgqd,bhqnld->bhgqnl', qb, k_sel).astype(jnp.float32)
        pos = idx[..., None] * SLC_LEN + jnp.arange(SLC_LEN)
        ok = (pos <= t[:, None, None])[:, :, None]
        s = jnp.where(ok, s, NEG_INF).reshape(B_, H_, G_, Q_BLOCK, n_top * SLC_LEN)
        p = jax.nn.softmax(s, axis=-1).reshape(B_, H_, G_, Q_BLOCK, n_top, SLC_LEN)
        o_slc = jnp.einsum('bhgqnl,bhqnld->bhgqd', p.astype(q.dtype), v_sel)
        kw = lax.dynamic_slice_in_dim(k_pad, qs, WINDOW + Q_BLOCK, axis=2)
        vw = lax.dynamic_slice_in_dim(v_pad, qs, WINDOW + Q_BLOCK, axis=2)
        s_pos = qs - WINDOW + jnp.arange(WINDOW + Q_BLOCK)
        diff = t[:, None] - s_pos[None, :]
        ok_w = (diff >= 0) & (diff < WINDOW) & (s_pos[None, :] >= 0)
        s = jnp.einsum('bhgqd,bhkd->bhgqk', qb, kw).astype(jnp.float32)
        p = jax.nn.softmax(jnp.where(ok_w, s, NEG_INF), axis=-1)
        o_win = jnp.einsum('bhgqk,bhkd->bhgqd', p.astype(q.dtype), vw)
        return gb[..., 0:1] * o_cmp + gb[..., 1:2] * o_slc + gb[..., 2:3] * o_win

    out = lax.map(block, jnp.arange(S_ // Q_BLOCK))
    return jnp.moveaxis(out, 0, 3).reshape(B_, H_, G_, S_, d)


def token_mixer(x, positions, w_in, pool_w, pool_scale, sg_ln_g, sg_ln_b, sg_w, sg_b,
                cmp_k_pos, cmp_k_w1, cmp_k_w2, cmp_v_pos, cmp_v_w1, cmp_v_w2,
                w_pool_out, w_sg_out, w_nsa_out, w_out):
    B_, S_, _ = x.shape
    proj = x @ w_in
    a, u, v, q, kc, vc, ks, vs, kw, vw, nsa_g, br_g = jnp.split(proj, SPLIT_POINTS, axis=-1)
    y_pool = pool_mixer(a, pool_w, pool_scale) @ w_pool_out
    y_sg = spatial_gating_mixer(u, v, sg_ln_g, sg_ln_b, sg_w, sg_b) @ w_sg_out

    q = partial_rope(q.reshape(B_, S_, NSA_Q_HEADS, HEAD_DIM), positions) * HEAD_DIM ** -0.5
    q = q.reshape(B_, S_, NSA_KV_HEADS, GROUP, HEAD_DIM).transpose(0, 2, 3, 1, 4)

    def kv_heads(t, rope):
        t = t.reshape(B_, S_, NSA_KV_HEADS, HEAD_DIM)
        if rope:
            t = partial_rope(t, positions)
        return t.transpose(0, 2, 1, 3)

    k_cmp = compress(kv_heads(kc, True), cmp_k_pos, cmp_k_w1, cmp_k_w2)
    v_cmp = compress(kv_heads(vc, False), cmp_v_pos, cmp_v_w1, cmp_v_w2)
    gates = jax.nn.sigmoid(nsa_g.reshape(B_, S_, NSA_KV_HEADS, GROUP, 3).transpose(0, 2, 3, 1, 4))
    o = nsa_attention(q, k_cmp, v_cmp, kv_heads(ks, True), kv_heads(vs, False),
                      kv_heads(kw, True), kv_heads(vw, False), gates)
    y_nsa = o.transpose(0, 3, 1, 2, 4).reshape(B_, S_, Q_WIDTH) @ w_nsa_out

    g_pool, g_sg, g_nsa = jnp.split(jax.nn.sigmoid(br_g), 3, axis=-1)
    return (g_pool * y_pool + g_sg * y_sg + g_nsa * y_nsa) @ w_out


def moe_swiglu(x, w_router, b_router, w_gate, w_up, w_down):
    B_, S_, D_ = x.shape
    T = B_ * S_
    TK = T * TOP_K
    xt = x.reshape(T, D_)
    logits = (xt @ w_router).astype(jnp.float32) + b_router.astype(jnp.float32)
    top_logit, top_e = lax.top_k(logits, TOP_K)
    weights = jax.nn.softmax(top_logit, axis=-1)
    flat_e = top_e.reshape(-1)
    flat_tok = jnp.arange(TK, dtype=jnp.int32) // TOP_K
    order = jnp.argsort(flat_e)
    e_sorted = flat_e[order]
    counts = jnp.bincount(flat_e, length=N_EXPERTS)
    padded = (counts + EXPERT_BLOCK - 1) // EXPERT_BLOCK * EXPERT_BLOCK
    start = jnp.cumsum(counts) - counts
    pstart = jnp.cumsum(padded) - padded
    dest = pstart[e_sorted] + jnp.arange(TK) - start[e_sorted]
    n_rows = (-(-TK // EXPERT_BLOCK) + N_EXPERTS) * EXPERT_BLOCK
    n_blocks = n_rows // EXPERT_BLOCK
    row_tok = jnp.full((n_rows,), T, jnp.int32).at[dest].set(flat_tok[order])
    row_w = jnp.zeros((n_rows,), jnp.float32).at[dest].set(weights.reshape(-1)[order])
    block_e = jnp.searchsorted(jnp.cumsum(padded), jnp.arange(n_blocks) * EXPERT_BLOCK, side='right')
    block_e = jnp.minimum(block_e, N_EXPERTS - 1)
    x_rows = jnp.concatenate([xt, jnp.zeros((1, D_), x.dtype)], axis=0)[row_tok]
    x_rows = x_rows.reshape(n_blocks, EXPERT_BLOCK, D_)

    def expert_block(args):
        xb, e = args
        return swiglu(xb, w_gate[e], w_up[e], w_down[e])

    y = lax.map(expert_block, (x_rows, block_e)).reshape(n_rows, D_)
    y = y * row_w[:, None].astype(y.dtype)
    out = jnp.zeros((T + 1, D_), y.dtype).at[row_tok].add(y)[:T]
    return out.reshape(B_, S_, D_)


def setup_inputs(seed: int = 0) -> dict:
    key = jax.random.key(seed)
    keys = iter(jax.random.split(key, 40))

    def nrm(shape, scale):
        return jax.random.normal(next(keys), shape, jnp.float32) * scale

    gw = POOL_WIDTH // POOL_GROUPS
    hc = SG_WIDTH // SG_HEADS
    return {
        'x': nrm((BATCH, SEQ, D_MODEL), 1.0),
        'p': nrm((DEPTH, BATCH, SEQ, PLE_DIM), 1.0),
        'positions': jax.random.randint(next(keys), (BATCH, 1), 0, 4096, jnp.int32) + jnp.arange(SEQ, dtype=jnp.int32)[None, :],
        'w_in': nrm((DEPTH, D_MODEL, IN_WIDTH), D_MODEL ** -0.5),
        'pool_w': nrm((DEPTH, POOL_GROUPS, gw, gw), gw ** -0.5),
        'pool_scale': 1.0 + nrm((DEPTH, POOL_WIDTH), 0.1),
        'sg_ln_g': 1.0 + nrm((DEPTH, SG_WIDTH), 0.05),
        'sg_ln_b': nrm((DEPTH, SG_WIDTH), 0.02),
        'sg_w': nrm((DEPTH, SG_HEADS, SG_CHUNK, SG_CHUNK), 0.5 * SG_CHUNK ** -0.5),
        'sg_b': 1.0 + nrm((DEPTH, SG_HEADS, SG_CHUNK), 0.1),
        'cmp_k_pos': nrm((DEPTH, CMP_LEN, HEAD_DIM), 0.02),
        'cmp_k_w1': nrm((DEPTH, CMP_LEN * HEAD_DIM, CMP_HIDDEN), (CMP_LEN * HEAD_DIM) ** -0.5),
        'cmp_k_w2': nrm((DEPTH, CMP_HIDDEN, HEAD_DIM), CMP_HIDDEN ** -0.5),
        'cmp_v_pos': nrm((DEPTH, CMP_LEN, HEAD_DIM), 0.02),
        'cmp_v_w1': nrm((DEPTH, CMP_LEN * HEAD_DIM, CMP_HIDDEN), (CMP_LEN * HEAD_DIM) ** -0.5),
        'cmp_v_w2': nrm((DEPTH, CMP_HIDDEN, HEAD_DIM), CMP_HIDDEN ** -0.5),
        'w_pool_out': nrm((DEPTH, POOL_WIDTH, D_MODEL), POOL_WIDTH ** -0.5),
        'w_sg_out': nrm((DEPTH, SG_WIDTH, D_MODEL), SG_WIDTH ** -0.5),
        'w_nsa_out': nrm((DEPTH, Q_WIDTH, D_MODEL), Q_WIDTH ** -0.5),
        'w_out': nrm((DEPTH, D_MODEL, D_MODEL), BETA * D_MODEL ** -0.5),
        'ln1_g': 1.0 + nrm((DEPTH, D_MODEL), 0.05),
        'ln1_b': nrm((DEPTH, D_MODEL), 0.02),
        'ffn_w_gate': nrm((N_DENSE, D_MODEL, D_FF), D_MODEL ** -0.5),
        'ffn_w_up': nrm((N_DENSE, D_MODEL, D_FF), D_MODEL ** -0.5),
        'ffn_w_down': nrm((N_DENSE, D_FF, D_MODEL), BETA * D_FF ** -0.5),
        'moe_router': nrm((N_MOE, D_MODEL, N_EXPERTS), D_MODEL ** -0.5),
        'moe_router_b': nrm((N_MOE, N_EXPERTS), 0.01),
        'moe_w_gate': nrm((N_MOE, N_EXPERTS, D_MODEL, D_FF), D_MODEL ** -0.5),
        'moe_w_up': nrm((N_MOE, N_EXPERTS, D_MODEL, D_FF), D_MODEL ** -0.5),
        'moe_w_down': nrm((N_MOE, N_EXPERTS, D_FF, D_MODEL), BETA * D_FF ** -0.5),
        'ple_gate_w': nrm((DEPTH, D_MODEL, D_MODEL), D_MODEL ** -0.5),
        'ple_gate_b': nrm((DEPTH, D_MODEL), 0.02),
        'ple_proj': nrm((DEPTH, PLE_DIM, D_MODEL), BETA * PLE_DIM ** -0.5),
        'ln2_g': 1.0 + nrm((DEPTH, D_MODEL), 0.05),
        'ln2_b': nrm((DEPTH, D_MODEL), 0.02),
    }


def reference(x, p, positions, w_in, pool_w, pool_scale, sg_ln_g, sg_ln_b, sg_w, sg_b,
              cmp_k_pos, cmp_k_w1, cmp_k_w2, cmp_v_pos, cmp_v_w1, cmp_v_w2,
              w_pool_out, w_sg_out, w_nsa_out, w_out, ln1_g, ln1_b,
              ffn_w_gate, ffn_w_up, ffn_w_down,
              moe_router, moe_router_b, moe_w_gate, moe_w_up, moe_w_down,
              ple_gate_w, ple_gate_b, ple_proj, ln2_g, ln2_b):
    for i in range(DEPTH):
        h = token_mixer(x, positions, w_in[i], pool_w[i], pool_scale[i], sg_ln_g[i], sg_ln_b[i],
                        sg_w[i], sg_b[i], cmp_k_pos[i], cmp_k_w1[i], cmp_k_w2[i],
                        cmp_v_pos[i], cmp_v_w1[i], cmp_v_w2[i],
                        w_pool_out[i], w_sg_out[i], w_nsa_out[i], w_out[i])
        x = layer_norm(ALPHA * x + h, ln1_g[i], ln1_b[i])
        j = i // 2
        if i % 2 == 0:
            f = swiglu(x, ffn_w_gate[j], ffn_w_up[j], ffn_w_down[j])
        else:
            f = moe_swiglu(x, moe_router[j], moe_router_b[j], moe_w_gate[j], moe_w_up[j], moe_w_down[j])
        ple = jax.nn.sigmoid(x @ ple_gate_w[i] + ple_gate_b[i]) * (p[i] @ ple_proj[i])
        x = layer_norm(ALPHA * x + f + ple, ln2_g[i], ln2_b[i])
    return x
```

```python
import functools

import jax
import jax.numpy as jnp
import numpy as np
from jax import lax
from jax.experimental import pallas as pl
from jax.experimental.pallas import tpu as pltpu

F32 = jnp.float32
BF16 = jnp.bfloat16

D_MODEL = 1024
POOL_WIDTH = 512
POOL_WINDOWS = (2, 4, 8, 16)
POOL_HALO = 16
SG_WIDTH = 512
SG_HEADS = 4
SG_CHUNK = 128
NSA_Q_HEADS = 8
NSA_KV_HEADS = 2
GROUP = NSA_Q_HEADS // NSA_KV_HEADS
HEAD_DIM = 64
Q_WIDTH = NSA_Q_HEADS * HEAD_DIM
KV_WIDTH = NSA_KV_HEADS * HEAD_DIM
ROT_DIM = HEAD_DIM // 4
ROPE_THETA = 500000.0
CMP_LEN = 32
CMP_STRIDE = 16
CMP_HIDDEN = 256
SLC_LEN = 64
SLC_TOP = 16
WINDOW = 512
Q_BLOCK = 128
NEG_INF = -1e30
FORCE_SCORE = 1e4
D_FF = 2816
N_EXPERTS = 8
TOP_K = 2
EXPERT_BLOCK = 512
PLE_DIM = 256
LN_EPS = 1e-5

LANE = 128
PROJ_BLOCK = 256
GATE_COL = 2816
BRANCH_COL = 3072
PROJ_COLS = BRANCH_COL + 3 * D_MODEL
SLC_TILE = 512
FF_TILE = 1408
VMEM_LIMIT = 56 * 1024 * 1024


def _params(semantics, vmem=VMEM_LIMIT):
    return pltpu.CompilerParams(dimension_semantics=semantics, vmem_limit_bytes=vmem)


def _dot(a, b):
    return jnp.dot(a, b, preferred_element_type=F32)


def _layer_norm_rows(v, g, b):
    mu = jnp.mean(v, axis=-1, keepdims=True)
    c = v - mu
    var = jnp.mean(c * c, axis=-1, keepdims=True)
    return c * lax.rsqrt(var + LN_EPS) * g + b


def _rope_tables(positions):
    half = ROT_DIM // 2
    inv = ROPE_THETA ** (-jnp.arange(half, dtype=F32) / half)
    ang = positions.astype(F32)[:, :, None] * inv
    cos, sin = jnp.cos(ang), jnp.sin(ang)
    rest = HEAD_DIM - ROT_DIM
    one = jnp.ones(ang.shape[:2] + (rest,), F32)
    zero_r = jnp.zeros(ang.shape[:2] + (rest,), F32)
    zero_h = jnp.zeros_like(sin)
    c = jnp.concatenate([cos, cos, one], -1)
    s1 = jnp.concatenate([zero_h, sin, zero_r], -1)
    s2 = jnp.concatenate([-sin, zero_h, zero_r], -1)
    reps = LANE // HEAD_DIM
    return tuple(jnp.tile(t, (1, 1, reps)).reshape(-1, LANE) for t in (c, s1, s2))


def _proj_kernel(x_ref, w_ref, c_ref, s1_ref, s2_ref, o_ref, xb_ref):
    j = pl.program_id(1)
    half = ROT_DIM // 2

    @pl.when(j == 0)
    def _():
        xb_ref[...] = x_ref[...].astype(BF16)

    acc = _dot(xb_ref[...], w_ref[...])

    def rope(v):
        return (v * c_ref[...] + pltpu.roll(v, half, 1) * s1_ref[...]
                + pltpu.roll(v, LANE - half, 1) * s2_ref[...])

    q_first = (POOL_WIDTH + 2 * SG_WIDTH) // PROJ_BLOCK
    kv_first = q_first + Q_WIDTH // PROJ_BLOCK
    is_q = (j >= q_first) & (j < kv_first)
    is_kv = (j >= kv_first) & (j < kv_first + 3)

    @pl.when(is_q)
    def _():
        scale = HEAD_DIM ** -0.5
        o_ref[:, :LANE] = (rope(acc[:, :LANE]) * scale).astype(o_ref.dtype)
        o_ref[:, LANE:] = (rope(acc[:, LANE:]) * scale).astype(o_ref.dtype)

    @pl.when(is_kv)
    def _():
        o_ref[:, :LANE] = rope(acc[:, :LANE]).astype(o_ref.dtype)
        o_ref[:, LANE:] = acc[:, LANE:].astype(o_ref.dtype)

    @pl.when(jnp.logical_not(is_q | is_kv))
    def _():
        o_ref[...] = acc.astype(o_ref.dtype)


def _proj(x2, w, tabs, tm):
    t = x2.shape[0]
    nj = PROJ_COLS // PROJ_BLOCK
    row = lambda i, j: (i, 0)
    return pl.pallas_call(
        _proj_kernel,
        grid=(t // tm, nj),
        in_specs=[pl.BlockSpec((tm, D_MODEL), row),
                  pl.BlockSpec((D_MODEL, PROJ_BLOCK), lambda i, j: (0, j)),
                  pl.BlockSpec((tm, LANE), row), pl.BlockSpec((tm, LANE), row), pl.BlockSpec((tm, LANE), row)],
        out_specs=pl.BlockSpec((tm, PROJ_BLOCK), lambda i, j: (i, j)),
        out_shape=jax.ShapeDtypeStruct((t, PROJ_COLS), BF16),
        scratch_shapes=[pltpu.VMEM((tm, D_MODEL), BF16)],
        compiler_params=_params(("parallel", "arbitrary")),
        name="proj",
    )(x2, w, *tabs)


def _prep_w_in(w):
    n_gate = NSA_Q_HEADS * 3
    main, gate, branch = w[:, :GATE_COL], w[:, GATE_COL:GATE_COL + n_gate], w[:, GATE_COL + n_gate:]
    pad = jnp.zeros((w.shape[0], BRANCH_COL - GATE_COL - n_gate), w.dtype)
    return jnp.concatenate([main, gate, pad, branch], axis=1).astype(BF16)


def _mixers_kernel(a_ref, ap_ref, u_ref, v_ref, pw_ref, ps_ref, lg_ref, lb_ref, sw_ref, sb_ref,
                   pm_ref, sg_ref, *, tiles_per_seq):
    i = pl.program_id(0)
    tm = a_ref.shape[0]
    it = i % tiles_per_seq
    gw = POOL_WIDTH // len(POOL_WINDOWS)

    a = a_ref[...].astype(F32)
    prev = jnp.where(it == 0, 0.0, ap_ref[...].astype(F32))
    ext = jnp.concatenate([prev, a], axis=0)
    tpos = (it * tm + lax.broadcasted_iota(jnp.int32, (tm, 1), 0) + 1).astype(F32)
    for g, w in enumerate(POOL_WINDOWS):
        s = ext[:, g * gw:(g + 1) * gw]
        k = 1
        while k < w:
            s = s + pltpu.roll(s, k, 0)
            k *= 2
        d = s[POOL_HALO:] / jnp.minimum(tpos, float(w)) - a[:, g * gw:(g + 1) * gw]
        y = _dot(d.astype(BF16), pw_ref[g]) * ps_ref[:, g * gw:(g + 1) * gw]
        pm_ref[:, g * gw:(g + 1) * gw] = y.astype(pm_ref.dtype)

    u = jax.nn.gelu(u_ref[...].astype(F32))
    v = _layer_norm_rows(jax.nn.gelu(v_ref[...].astype(F32)), lg_ref[...], lb_ref[...]).astype(BF16)
    hc = SG_WIDTH // SG_HEADS
    tri = (lax.broadcasted_iota(jnp.int32, (SG_CHUNK, SG_CHUNK), 0)
           >= lax.broadcasted_iota(jnp.int32, (SG_CHUNK, SG_CHUNK), 1))
    for g in range(SG_HEADS):
        ws = jnp.where(tri, sw_ref[g], 0.0).astype(BF16)
        bias = sb_ref[:, g:g + 1]
        for c in range(tm // SG_CHUNK):
            rows = slice(c * SG_CHUNK, (c + 1) * SG_CHUNK)
            cols = slice(g * hc, (g + 1) * hc)
            mixed = _dot(ws, v[rows, cols]) + bias
            sg_ref[rows, cols] = (u[rows, cols] * mixed).astype(sg_ref.dtype)


def _mixers(proj, pool_w, pool_scale, ln_g, ln_b, sg_w, sg_b, seq, tm):
    t = proj.shape[0]
    halo_blocks = tm // POOL_HALO
    full = lambda shape: pl.BlockSpec(shape, lambda i: (0,) * len(shape))
    return pl.pallas_call(
        functools.partial(_mixers_kernel, tiles_per_seq=seq // tm),
        grid=(t // tm,),
        in_specs=[pl.BlockSpec((tm, POOL_WIDTH), lambda i: (i, 0)),
                  pl.BlockSpec((POOL_HALO, POOL_WIDTH), lambda i: (jnp.maximum(i * halo_blocks - 1, 0), 0)),
                  pl.BlockSpec((tm, SG_WIDTH), lambda i: (i, 1)),
                  pl.BlockSpec((tm, SG_WIDTH), lambda i: (i, 2)),
                  full(pool_w.shape), full((1, POOL_WIDTH)), full((1, SG_WIDTH)), full((1, SG_WIDTH)),
                  full(sg_w.shape), full((SG_CHUNK, SG_HEADS))],
        out_specs=[pl.BlockSpec((tm, POOL_WIDTH), lambda i: (i, 0)),
                   pl.BlockSpec((tm, SG_WIDTH), lambda i: (i, 0))],
        out_shape=[jax.ShapeDtypeStruct((t, POOL_WIDTH), BF16), jax.ShapeDtypeStruct((t, SG_WIDTH), BF16)],
        compiler_params=_params(("parallel",)),
        name="mixers",
    )(proj, proj, proj, proj, pool_w.astype(BF16), pool_scale[None], ln_g[None], ln_b[None], sg_w, sg_b.T)


def _compress_kernel(ch_ref, pos_ref, w1_ref, w2_ref, o_ref):
    nc, half = ch_ref.shape
    ch = ch_ref[...].astype(F32)
    pos = pos_ref[...]
    first = _dot((ch + pos[:, :half]).astype(BF16), w1_ref[:half, :])
    second = _dot((ch + pos[:, half:]).astype(BF16), w1_ref[half:, :])
    pre = first + pltpu.roll(second, nc - 1, 0)
    o_ref[...] = _dot(jax.nn.gelu(pre).astype(BF16), w2_ref[...]).astype(o_ref.dtype)


def _compress(chunks, pos, w1, w2):
    _, b, h, nc, width = chunks.shape
    sel = lambda s, bi, hi: (s, 0, 0)
    return pl.pallas_call(
        _compress_kernel,
        grid=(2, b, h),
        in_specs=[pl.BlockSpec((None, None, None, nc, width), lambda s, bi, hi: (s, bi, hi, 0, 0)),
                  pl.BlockSpec((None, 1, 2 * width), sel),
                  pl.BlockSpec((None, 2 * width, CMP_HIDDEN), sel),
                  pl.BlockSpec((None, CMP_HIDDEN, HEAD_DIM), sel)],
        out_specs=pl.BlockSpec((None, None, None, nc, HEAD_DIM), lambda s, bi, hi: (s, bi, hi, 0, 0)),
        out_shape=jax.ShapeDtypeStruct((2, b, h, nc, HEAD_DIM), BF16),
        compiler_params=_params(("parallel", "parallel", "parallel")),
        name="compress",
    )(chunks, pos, w1, w2)


def _nsa_kernel(qT_ref, kc_ref, vcT_ref, mt_ref, ks_ref, vsT_ref, kw_ref, vwT_ref, g_ref, o_ref, selneg_ref,
                *, n_top):
    n = pl.program_id(2)
    qs = n * Q_BLOCK
    width = GROUP * Q_BLOCK
    qT = qT_ref[...]
    t_row = qs + lax.broadcasted_iota(jnp.int32, (1, width), 1) % Q_BLOCK
    nc = kc_ref.shape[0]
    ns = mt_ref.shape[0]

    s = _dot(kc_ref[...], qT)
    c_end = lax.broadcasted_iota(jnp.int32, (nc, 1), 0) * CMP_STRIDE + (CMP_LEN - 1)
    valid = c_end <= t_row
    s = jnp.where(valid, s, NEG_INF)
    e = jnp.where(valid, jnp.exp(s - jnp.max(s, axis=0, keepdims=True)), 0.0)
    l = jnp.sum(e, axis=0, keepdims=True)
    p = e * jnp.where(l > 0.0, 1.0 / l, 0.0)
    o_cmp = _dot(vcT_ref[...], p.astype(BF16))

    psum = p[:, :Q_BLOCK]
    for g in range(1, GROUP):
        psum = psum + p[:, g * Q_BLOCK:(g + 1) * Q_BLOCK]
    mt = mt_ref[...]
    imp = jnp.zeros((ns, Q_BLOCK), F32)
    rem = psum
    for _ in range(3):
        piece = rem.astype(BF16)
        imp = imp + _dot(mt, piece)
        rem = rem - piece.astype(F32)

    blk = lax.broadcasted_iota(jnp.int32, (ns, Q_BLOCK), 0)
    cur = (qs + lax.broadcasted_iota(jnp.int32, (ns, Q_BLOCK), 1)) // SLC_LEN
    score = jnp.where(blk == 0, FORCE_SCORE, jnp.where(blk == cur, FORCE_SCORE,
                                                      jnp.where(blk == cur - 1, FORCE_SCORE, imp)))
    score = jnp.where(blk <= cur, score, -1.0)
    blk_f = blk.astype(F32)
    selneg = jnp.full((ns, Q_BLOCK), NEG_INF, F32)
    for _ in range(n_top):
        top = jnp.max(score, axis=0, keepdims=True)
        first = jnp.min(jnp.where(score == top, blk_f, float(ns)), axis=0, keepdims=True)
        hit = blk_f == first
        selneg = jnp.where(hit, 0.0, selneg)
        score = jnp.where(hit, -2.0, score)
    selneg_ref[...] = selneg

    blocks_per_tile = SLC_TILE // SLC_LEN

    def slc_tile(kt, carry, causal):
        m, l, acc = carry
        k0 = pl.multiple_of(kt * SLC_TILE, SLC_TILE)
        s = _dot(ks_ref[pl.ds(k0, SLC_TILE), :], qT)
        b0 = pl.multiple_of(kt * blocks_per_tile, blocks_per_tile)
        bias = selneg_ref[pl.ds(b0, blocks_per_tile), :]
        bias = jnp.concatenate([bias] * GROUP, axis=1)
        s = jnp.concatenate([s[r * SLC_LEN:(r + 1) * SLC_LEN] + bias[r:r + 1]
                             for r in range(blocks_per_tile)], axis=0)
        if causal:
            kpos = k0 + lax.broadcasted_iota(jnp.int32, (SLC_TILE, 1), 0)
            s = jnp.where(kpos <= t_row, s, NEG_INF)
        m_new = jnp.maximum(m, jnp.max(s, axis=0, keepdims=True))
        alpha = jnp.exp(m - m_new)
        pr = jnp.exp(s - m_new)
        l = alpha * l + jnp.sum(pr, axis=0, keepdims=True)
        acc = alpha * acc + _dot(vsT_ref[:, pl.ds(k0, SLC_TILE)], pr.astype(BF16))
        return m_new, l, acc

    n_full = n // (SLC_TILE // Q_BLOCK)
    init = (jnp.full((1, width), NEG_INF, F32), jnp.zeros((1, width), F32), jnp.zeros((HEAD_DIM, width), F32))
    carry = lax.fori_loop(0, n_full, lambda kt, c: slc_tile(kt, c, False), init)
    _, l, acc = slc_tile(n_full, carry, True)
    o_slc = acc / l

    span = WINDOW + Q_BLOCK
    w0 = pl.multiple_of(jnp.maximum(qs - WINDOW, 0), Q_BLOCK)
    s = _dot(kw_ref[pl.ds(w0, span), :], qT)
    diff = t_row - (w0 + lax.broadcasted_iota(jnp.int32, (span, 1), 0))
    s = jnp.where(diff >= 0, jnp.where(diff < WINDOW, s, NEG_INF), NEG_INF)
    pr = jnp.exp(s - jnp.max(s, axis=0, keepdims=True))
    l = jnp.sum(pr, axis=0, keepdims=True)
    o_win = _dot(vwT_ref[:, pl.ds(w0, span)], pr.astype(BF16)) / l

    gates = jax.nn.sigmoid(g_ref[...].astype(F32))

    def gate(c):
        return jnp.concatenate([gates[g * 3 + c:g * 3 + c + 1, :] for g in range(GROUP)], axis=1)

    o_ref[...] = (gate(0) * o_cmp + gate(1) * o_slc + gate(2) * o_win).astype(o_ref.dtype)


def _importance_matrix(nc, ns):
    per = SLC_LEN // CMP_STRIDE
    c = np.arange(nc)[None, :]
    j = np.arange(ns)[:, None]
    m = ((c >= per * j - 1) & (c <= per * j + per - 1)).astype(np.float32)
    m += ((c >= per * j) & (c <= per * j + per - 2)).astype(np.float32)
    return jnp.asarray(m, BF16)


def _nsa(qT, kc, vcT, ks, vsT, kw, vwT, gates, seq):
    b, h = qT.shape[:2]
    nq = seq // Q_BLOCK
    nc = kc.shape[2]
    ns = seq // SLC_LEN
    width = GROUP * Q_BLOCK
    per_head = lambda shape: pl.BlockSpec((None, None) + shape, lambda bi, hi, n: (bi, hi, 0, 0))
    return pl.pallas_call(
        functools.partial(_nsa_kernel, n_top=min(SLC_TOP, ns)),
        grid=(b, h, nq),
        in_specs=[pl.BlockSpec((None, None, HEAD_DIM, width), lambda bi, hi, n: (bi, hi, 0, n)),
                  per_head((nc, HEAD_DIM)), per_head((HEAD_DIM, nc)),
                  pl.BlockSpec((ns, nc), lambda bi, hi, n: (0, 0)),
                  per_head((seq, HEAD_DIM)), per_head((HEAD_DIM, seq)),
                  per_head((seq, HEAD_DIM)), per_head((HEAD_DIM, seq)),
                  pl.BlockSpec((None, None, 16, Q_BLOCK), lambda bi, hi, n: (bi, hi, 0, n))],
        out_specs=pl.BlockSpec((None, None, HEAD_DIM, width), lambda bi, hi, n: (bi, hi, 0, n)),
        out_shape=jax.ShapeDtypeStruct((b, h, HEAD_DIM, nq * width), BF16),
        scratch_shapes=[pltpu.VMEM((ns, Q_BLOCK), F32)],
        compiler_params=_params(("parallel", "parallel", "arbitrary")),
        name="nsa",
    )(qT, kc, vcT, _importance_matrix(nc, ns), ks, vsT, kw, vwT, gates)


def _nsa_layouts(proj, b, seq):
    h = NSA_KV_HEADS
    nq = seq // Q_BLOCK
    col = POOL_WIDTH + 2 * SG_WIDTH
    q = proj[:, col:col + Q_WIDTH].reshape(b, nq, Q_BLOCK, h, GROUP, HEAD_DIM)
    qT = q.transpose(0, 3, 5, 1, 4, 2).reshape(b, h, HEAD_DIM, nq * GROUP * Q_BLOCK)
    col += Q_WIDTH
    heads = [proj[:, col + k * KV_WIDTH:col + (k + 1) * KV_WIDTH].reshape(b, seq, h, HEAD_DIM).transpose(0, 2, 1, 3)
             for k in range(6)]
    kc, vc, ks, vs, kw, vw = heads
    chunk = lambda t: t.reshape(b, h, seq // CMP_STRIDE, CMP_STRIDE * HEAD_DIM)
    chunks = jnp.stack([chunk(kc), chunk(vc)])
    gates = proj[:, GATE_COL:GATE_COL + NSA_Q_HEADS * 3].reshape(b, seq, h, GROUP * 3).transpose(0, 2, 3, 1)
    gates = jnp.pad(gates, ((0, 0), (0, 0), (0, 16 - GROUP * 3), (0, 0)))
    return qT, chunks, ks, vs.transpose(0, 1, 3, 2), kw, vw.transpose(0, 1, 3, 2), gates


def _merge_kernel(*refs, with_router, alpha):
    (pm_ref, sg_ref, on_ref, gp_ref, gs_ref, gn_ref, x_ref, wp_ref, ws_ref, wn_ref, wo_ref, lg_ref, lb_ref) = refs[:13]
    if with_router:
        wr_ref, br_ref, x1_ref, route_ref = refs[13:]
    else:
        (x1_ref,) = refs[13:]
    sig = lambda r: jax.nn.sigmoid(r[...].astype(F32))
    y = (sig(gp_ref) * _dot(pm_ref[...], wp_ref[...]) + sig(gs_ref) * _dot(sg_ref[...], ws_ref[...])
         + sig(gn_ref) * _dot(on_ref[...], wn_ref[...]))
    hmix = _dot(y.astype(BF16), wo_ref[...])
    x1 = _layer_norm_rows(alpha * x_ref[...] + hmix, lg_ref[...], lb_ref[...])
    x1_ref[...] = x1
    if with_router:
        wr = wr_ref[...]
        w_hi = wr.astype(BF16)
        w_lo = (wr - w_hi.astype(F32)).astype(BF16)
        x_hi = x1.astype(BF16)
        x_lo = (x1 - x_hi.astype(F32)).astype(BF16)
        logits = _dot(x_hi, w_hi) + (_dot(x_lo, w_hi) + _dot(x_hi, w_lo)) + br_ref[...]
        lane = lax.broadcasted_iota(jnp.int32, logits.shape, 1)
        lane_f = lane.astype(F32)
        logits = jnp.where(lane < N_EXPERTS, logits, -jnp.inf)
        m1 = jnp.max(logits, axis=-1, keepdims=True)
        i1 = jnp.min(jnp.where(logits == m1, lane_f, float(LANE)), axis=-1, keepdims=True)
        rest = jnp.where(lane_f == i1, -jnp.inf, logits)
        m2 = jnp.max(rest, axis=-1, keepdims=True)
        i2 = jnp.min(jnp.where(rest == m2, lane_f, float(LANE)), axis=-1, keepdims=True)
        e2 = jnp.exp(m2 - m1)
        w1 = 1.0 / (1.0 + e2)
        route_ref[...] = jnp.where(lane == 0, i1, jnp.where(lane == 1, i2, jnp.where(lane == 2, w1,
                                   jnp.where(lane == 3, e2 * w1, 0.0))))


def _merge(pm, sgm, on, proj, x2, wp, ws, wn, wo, lg, lb, alpha, tm, router=None):
    t = x2.shape[0]
    row = lambda width, cb=0: pl.BlockSpec((tm, width), lambda i, cb=cb: (i, cb))
    full = lambda shape: pl.BlockSpec(shape, lambda i: (0,) * len(shape))
    gate_block = BRANCH_COL // D_MODEL
    in_specs = [row(POOL_WIDTH), row(SG_WIDTH), row(Q_WIDTH),
                row(D_MODEL, gate_block), row(D_MODEL, gate_block + 1), row(D_MODEL, gate_block + 2),
                row(D_MODEL), full(wp.shape), full(ws.shape), full(wn.shape), full(wo.shape),
                full((1, D_MODEL)), full((1, D_MODEL))]
    args = [pm, sgm, on, proj, proj, proj, x2, wp.astype(BF16), ws.astype(BF16), wn.astype(BF16), wo.astype(BF16),
            lg[None], lb[None]]
    out_specs = [row(D_MODEL)]
    out_shape = [jax.ShapeDtypeStruct((t, D_MODEL), F32)]
    if router is not None:
        w_router, b_router = router
        pad = LANE - N_EXPERTS
        in_specs += [full((D_MODEL, LANE)), full((1, LANE))]
        args += [jnp.pad(w_router, ((0, 0), (0, pad))), jnp.pad(b_router, (0, pad))[None]]
        out_specs.append(row(LANE))
        out_shape.append(jax.ShapeDtypeStruct((t, LANE), F32))
    out = pl.pallas_call(
        functools.partial(_merge_kernel, with_router=router is not None, alpha=alpha),
        grid=(t // tm,), in_specs=in_specs, out_specs=out_specs, out_shape=out_shape,
        compiler_params=_params(("parallel",)),
        name="merge",
    )(*args)
    return out if router is not None else out[0]


def _gather_kernel(idx_ref, src_ref, out_ref, sem):
    rows = idx_ref.shape[1]
    base = pl.program_id(0) * rows

    def row_copy(r):
        return pltpu.make_async_copy(src_ref.at[pl.ds(idx_ref[0, r], 1)], out_ref.at[pl.ds(base + r, 1)], sem)

    def start(r, c):
        row_copy(r).start()
        return c

    def wait(r, c):
        row_copy(r).wait()
        return c

    lax.fori_loop(0, rows, start, 0)
    lax.fori_loop(0, rows, wait, 0)


def _gather_rows(src, idx, rows_per_step):
    n = idx.shape[0]
    steps = n // rows_per_step
    return pl.pallas_call(
        _gather_kernel,
        grid=(steps,),
        in_specs=[pl.BlockSpec((None, 1, rows_per_step), lambda i: (i, 0, 0), memory_space=pltpu.SMEM),
                  pl.BlockSpec(memory_space=pl.ANY)],
        out_specs=pl.BlockSpec(memory_space=pl.ANY),
        out_shape=jax.ShapeDtypeStruct((n, src.shape[1]), src.dtype),
        scratch_shapes=[pltpu.SemaphoreType.DMA(())],
        compiler_params=_params(("arbitrary",)),
        name="gather_rows",
    )(idx.reshape(steps, 1, rows_per_step), src)


def _swiglu_kernel(be_ref, x_ref, wg_ref, wu_ref, wd_ref, o_ref, xb_ref):
    k = pl.program_id(1)

    @pl.when(k == 0)
    def _():
        xb_ref[...] = x_ref[...].astype(BF16)

    xb = xb_ref[...]
    hidden = (jax.nn.silu(_dot(xb, wg_ref[...])) * _dot(xb, wu_ref[...])).astype(BF16)
    part = _dot(hidden, wd_ref[...])

    @pl.when(k == 0)
    def _():
        o_ref[...] = part

    @pl.when(k != 0)
    def _():
        o_ref[...] += part


def _swiglu(rows, block_e, wg, wu, wd, tm):
    r = rows.shape[0]
    kf = D_FF // FF_TILE
    return pl.pallas_call(
        _swiglu_kernel,
        grid_spec=pltpu.PrefetchScalarGridSpec(
            num_scalar_prefetch=1,
            grid=(r // tm, kf),
            in_specs=[pl.BlockSpec((tm, D_MODEL), lambda i, k, be: (i, 0)),
                      pl.BlockSpec((None, D_MODEL, FF_TILE), lambda i, k, be: (be[i], 0, k)),
                      pl.BlockSpec((None, D_MODEL, FF_TILE), lambda i, k, be: (be[i], 0, k)),
                      pl.BlockSpec((None, FF_TILE, D_MODEL), lambda i, k, be: (be[i], k, 0))],
            out_specs=pl.BlockSpec((tm, D_MODEL), lambda i, k, be: (i, 0)),
            scratch_shapes=[pltpu.VMEM((tm, D_MODEL), BF16)]),
        out_shape=jax.ShapeDtypeStruct((r, D_MODEL), F32),
        compiler_params=_params(("parallel", "arbitrary")),
        name="swiglu",
    )(block_e, rows, wg, wu, wd)


def _ple_ln2_kernel(*refs, n_parts, alpha):
    x_ref, p_ref = refs[0], refs[1]
    parts = refs[2:2 + n_parts]
    rest = refs[2 + n_parts:]
    if n_parts > 1:
        route_ref, rest = rest[0], rest[1:]
    wg_ref, bg_ref, wp_ref, lg_ref, lb_ref, o_ref = rest
    x1 = x_ref[...]
    if n_parts > 1:
        route = route_ref[...]
        f = parts[0][...] * route[:, TOP_K:TOP_K + 1]
        for k in range(1, n_parts):
            f = f + parts[k][...] * route[:, TOP_K + k:TOP_K + k + 1]
    else:
        f = parts[0][...]
    gate = jax.nn.sigmoid(_dot(x1.astype(BF16), wg_ref[...]) + bg_ref[...])
    ple = gate * _dot(p_ref[...].astype(BF16), wp_ref[...])
    o_ref[...] = _layer_norm_rows(alpha * x1 + f + ple, lg_ref[...], lb_ref[...])


def _ple_ln2(x1, p2, y, route, wg, bg, wp, lg, lb, alpha, tm):
    t = x1.shape[0]
    n_parts = y.shape[0] // t
    steps = t // tm
    row = lambda width: pl.BlockSpec((tm, width), lambda i: (i, 0))
    full = lambda shape: pl.BlockSpec(shape, lambda i: (0,) * len(shape))
    in_specs = [row(D_MODEL), row(PLE_DIM)]
    in_specs += [pl.BlockSpec((tm, D_MODEL), lambda i, k=k: (k * steps + i, 0)) for k in range(n_parts)]
    args = [x1, p2] + [y] * n_parts
    if n_parts > 1:
        in_specs.append(row(LANE))
        args.append(route)
    in_specs += [full((D_MODEL, D_MODEL)), full((1, D_MODEL)), full((PLE_DIM, D_MODEL)),
                 full((1, D_MODEL)), full((1, D_MODEL))]
    args += [wg.astype(BF16), bg[None], wp.astype(BF16), lg[None], lb[None]]
    return pl.pallas_call(
        functools.partial(_ple_ln2_kernel, n_parts=n_parts, alpha=alpha),
        grid=(steps,), in_specs=in_specs, out_specs=row(D_MODEL),
        out_shape=jax.ShapeDtypeStruct((t, D_MODEL), F32),
        compiler_params=_params(("parallel",)),
        name="ple_ln2",
    )(*args)


def _route_tables(route, t):
    flat_e = route[:, :TOP_K].astype(jnp.int32).reshape(-1)
    tk = flat_e.shape[0]
    onehot = (flat_e[:, None] == jnp.arange(N_EXPERTS, dtype=jnp.int32)[None, :]).astype(jnp.int32)
    before = jnp.cumsum(onehot, axis=0) - onehot
    rank = jnp.sum(before * onehot, axis=1)
    counts = jnp.sum(onehot, axis=0)
    padded = (counts + EXPERT_BLOCK - 1) // EXPERT_BLOCK * EXPERT_BLOCK
    pend = jnp.cumsum(padded)
    dest = (pend - padded)[flat_e] + rank
    n_blocks = -(-tk // EXPERT_BLOCK) + N_EXPERTS
    n_rows = n_blocks * EXPERT_BLOCK
    row_tok = jnp.zeros((n_rows,), jnp.int32).at[dest].set(jnp.arange(tk, dtype=jnp.int32) // TOP_K)
    block_e = jnp.searchsorted(pend, jnp.arange(n_blocks, dtype=jnp.int32) * EXPERT_BLOCK, side='right')
    block_e = jnp.minimum(block_e, N_EXPERTS - 1).astype(jnp.int32)
    pair_rows = dest.reshape(t, TOP_K).T.reshape(-1)
    return row_tok, block_e, pair_rows


def kernel(x, p, positions, w_in, pool_w, pool_scale, sg_ln_g, sg_ln_b, sg_w, sg_b, cmp_k_pos, cmp_k_w1, cmp_k_w2, cmp_v_pos, cmp_v_w1, cmp_v_w2, w_pool_out, w_sg_out, w_nsa_out, w_out, ln1_g, ln1_b, ffn_w_gate, ffn_w_up, ffn_w_down, moe_router, moe_router_b, moe_w_gate, moe_w_up, moe_w_down, ple_gate_w, ple_gate_b, ple_proj, ln2_g, ln2_b):
    b, seq, d = x.shape
    depth = w_in.shape[0]
    t = b * seq
    alpha = (2 * depth) ** 0.25
    tm = 512
    tabs = _rope_tables(positions)
    x2 = x.reshape(t, d)
    for i in range(depth):
        proj = _proj(x2, _prep_w_in(w_in[i]), tabs, min(1024, t))
        pm, sgm = _mixers(proj, pool_w[i], pool_scale[i], sg_ln_g[i], sg_ln_b[i], sg_w[i], sg_b[i], seq, tm)
        qT, chunks, ks, vsT, kw, vwT, gates = _nsa_layouts(proj, b, seq)
        pos = jnp.stack([cmp_k_pos[i], cmp_v_pos[i]]).reshape(2, 1, CMP_LEN * HEAD_DIM)
        cmp = _compress(chunks, pos, jnp.stack([cmp_k_w1[i], cmp_v_w1[i]]).astype(BF16),
                        jnp.stack([cmp_k_w2[i], cmp_v_w2[i]]).astype(BF16))
        oT = _nsa(qT, cmp[0], cmp[1].transpose(0, 1, 3, 2), ks, vsT, kw, vwT, gates, seq)
        nq = seq // Q_BLOCK
        on = oT.reshape(b, NSA_KV_HEADS, HEAD_DIM, nq, GROUP, Q_BLOCK).transpose(0, 3, 5, 1, 4, 2).reshape(t, Q_WIDTH)
        j = i // 2
        moe = i % 2 == 1
        merged = _merge(pm, sgm, on, proj, x2, w_pool_out[i], w_sg_out[i], w_nsa_out[i], w_out[i],
                        ln1_g[i], ln1_b[i], alpha, tm, router=(moe_router[j], moe_router_b[j]) if moe else None)
        if moe:
            x1, route = merged
            row_tok, block_e, pair_rows = _route_tables(route, t)
            rows = _gather_rows(x1, row_tok, EXPERT_BLOCK)
            y = _swiglu(rows, block_e, moe_w_gate[j].astype(BF16), moe_w_up[j].astype(BF16),
                        moe_w_down[j].astype(BF16), EXPERT_BLOCK)
            y = _gather_rows(y, pair_rows, EXPERT_BLOCK)
        else:
            x1, route = merged, None
            y = _swiglu(x1, jnp.zeros((t // tm,), jnp.int32), ffn_w_gate[j][None].astype(BF16),
                        ffn_w_up[j][None].astype(BF16), ffn_w_down[j][None].astype(BF16), tm)
        x2 = _ple_ln2(x1, p[i].reshape(t, PLE_DIM), y, route, ple_gate_w[i], ple_gate_b[i], ple_proj[i],
                      ln2_g[i], ln2_b[i], alpha, tm)
    return x2.reshape(b, seq, d)
```

```python
import functools

import jax
import jax.numpy as jnp
import numpy as np
from jax import lax
from jax.experimental import pallas as pl
from jax.experimental.pallas import tpu as pltpu

F32 = jnp.float32
BF16 = jnp.bfloat16

D_MODEL = 1024
POOL_WIDTH = 512
POOL_WINDOWS = (2, 4, 8, 16)
POOL_HALO = 16
SG_WIDTH = 512
SG_HEADS = 4
SG_CHUNK = 128
NSA_Q_HEADS = 8
NSA_KV_HEADS = 2
GROUP = NSA_Q_HEADS // NSA_KV_HEADS
HEAD_DIM = 64
Q_WIDTH = NSA_Q_HEADS * HEAD_DIM
KV_WIDTH = NSA_KV_HEADS * HEAD_DIM
ROT_DIM = HEAD_DIM // 4
ROPE_THETA = 500000.0
CMP_LEN = 32
CMP_STRIDE = 16
CMP_HIDDEN = 256
SLC_LEN = 64
SLC_TOP = 16
WINDOW = 512
Q_BLOCK = 128
NEG_INF = -1e30
FORCE_SCORE = 1e4
D_FF = 2816
N_EXPERTS = 8
TOP_K = 2
EXPERT_BLOCK = 512
PLE_DIM = 256
LN_EPS = 1e-5

LANE = 128
TOKEN_TILE = D_MODEL // LANE
PROJ_BLOCK = 256
GATE_COL = 2816
BRANCH_COL = 3072
PROJ_COLS = BRANCH_COL + 3 * D_MODEL
SLC_TILE = 512
FF_TILE = 1408
VMEM_LIMIT = 56 * 1024 * 1024


def _params(semantics, vmem=VMEM_LIMIT):
    return pltpu.CompilerParams(dimension_semantics=semantics, vmem_limit_bytes=vmem)


def _dot(a, b):
    return jnp.dot(a, b, preferred_element_type=F32)


def _store_token_tiles(ref, v):
    n = v.shape[0]
    for s in range(TOKEN_TILE):
        ref[pl.ds(s, n, stride=TOKEN_TILE), :] = v[:, s * LANE:(s + 1) * LANE]


def _load_token_tiles(ref):
    n = ref.shape[0] // TOKEN_TILE
    return jnp.concatenate([ref[pl.ds(s, n, stride=TOKEN_TILE), :] for s in range(TOKEN_TILE)], axis=1)


def _layer_norm_rows(v, g, b):
    mu = jnp.mean(v, axis=-1, keepdims=True)
    c = v - mu
    var = jnp.mean(c * c, axis=-1, keepdims=True)
    return c * lax.rsqrt(var + LN_EPS) * g + b


def _rope_tables(positions):
    half = ROT_DIM // 2
    inv = ROPE_THETA ** (-jnp.arange(half, dtype=F32) / half)
    ang = positions.astype(F32)[:, :, None] * inv
    cos, sin = jnp.cos(ang), jnp.sin(ang)
    rest = HEAD_DIM - ROT_DIM
    one = jnp.ones(ang.shape[:2] + (rest,), F32)
    zero_r = jnp.zeros(ang.shape[:2] + (rest,), F32)
    zero_h = jnp.zeros_like(sin)
    c = jnp.concatenate([cos, cos, one], -1)
    s1 = jnp.concatenate([zero_h, sin, zero_r], -1)
    s2 = jnp.concatenate([-sin, zero_h, zero_r], -1)
    reps = LANE // HEAD_DIM
    return tuple(jnp.tile(t, (1, 1, reps)).reshape(-1, LANE) for t in (c, s1, s2))


def _proj_kernel(x_ref, w_ref, c_ref, s1_ref, s2_ref, o_ref, xb_ref):
    j = pl.program_id(1)
    half = ROT_DIM // 2

    @pl.when(j == 0)
    def _():
        xb_ref[...] = x_ref[...].astype(BF16)

    acc = _dot(xb_ref[...], w_ref[...])

    def rope(v):
        return (v * c_ref[...] + pltpu.roll(v, half, 1) * s1_ref[...]
                + pltpu.roll(v, LANE - half, 1) * s2_ref[...])

    q_first = (POOL_WIDTH + 2 * SG_WIDTH) // PROJ_BLOCK
    kv_first = q_first + Q_WIDTH // PROJ_BLOCK
    is_q = (j >= q_first) & (j < kv_first)
    is_kv = (j >= kv_first) & (j < kv_first + 3)

    @pl.when(is_q)
    def _():
        scale = HEAD_DIM ** -0.5
        o_ref[:, :LANE] = (rope(acc[:, :LANE]) * scale).astype(o_ref.dtype)
        o_ref[:, LANE:] = (rope(acc[:, LANE:]) * scale).astype(o_ref.dtype)

    @pl.when(is_kv)
    def _():
        o_ref[:, :LANE] = rope(acc[:, :LANE]).astype(o_ref.dtype)
        o_ref[:, LANE:] = acc[:, LANE:].astype(o_ref.dtype)

    @pl.when(jnp.logical_not(is_q | is_kv))
    def _():
        o_ref[...] = acc.astype(o_ref.dtype)


def _proj(x2, w, tabs, tm):
    t = x2.shape[0]
    nj = PROJ_COLS // PROJ_BLOCK
    row = lambda i, j: (i, 0)
    return pl.pallas_call(
        _proj_kernel,
        grid=(t // tm, nj),
        in_specs=[pl.BlockSpec((tm, D_MODEL), row),
                  pl.BlockSpec((D_MODEL, PROJ_BLOCK), lambda i, j: (0, j)),
                  pl.BlockSpec((tm, LANE), row), pl.BlockSpec((tm, LANE), row), pl.BlockSpec((tm, LANE), row)],
        out_specs=pl.BlockSpec((tm, PROJ_BLOCK), lambda i, j: (i, j)),
        out_shape=jax.ShapeDtypeStruct((t, PROJ_COLS), BF16),
        scratch_shapes=[pltpu.VMEM((tm, D_MODEL), BF16)],
        compiler_params=_params(("parallel", "arbitrary")),
        name="proj",
    )(x2, w, *tabs)


def _prep_w_in(w):
    n_gate = NSA_Q_HEADS * 3
    main, gate, branch = w[:, :GATE_COL], w[:, GATE_COL:GATE_COL + n_gate], w[:, GATE_COL + n_gate:]
    pad = jnp.zeros((w.shape[0], BRANCH_COL - GATE_COL - n_gate), w.dtype)
    return jnp.concatenate([main, gate, pad, branch], axis=1).astype(BF16)


def _mixers_kernel(a_ref, ap_ref, u_ref, v_ref, pw_ref, ps_ref, lg_ref, lb_ref, sw_ref, sb_ref,
                   pm_ref, sg_ref, *, tiles_per_seq):
    i = pl.program_id(0)
    tm = a_ref.shape[0]
    it = i % tiles_per_seq
    gw = POOL_WIDTH // len(POOL_WINDOWS)

    a = a_ref[...].astype(F32)
    prev = jnp.where(it == 0, 0.0, ap_ref[...].astype(F32))
    ext = jnp.concatenate([prev, a], axis=0)
    tpos = (it * tm + lax.broadcasted_iota(jnp.int32, (tm, 1), 0) + 1).astype(F32)
    for g, w in enumerate(POOL_WINDOWS):
        s = ext[:, g * gw:(g + 1) * gw]
        k = 1
        while k < w:
            s = s + pltpu.roll(s, k, 0)
            k *= 2
        d = s[POOL_HALO:] / jnp.minimum(tpos, float(w)) - a[:, g * gw:(g + 1) * gw]
        y = _dot(d.astype(BF16), pw_ref[g]) * ps_ref[:, g * gw:(g + 1) * gw]
        pm_ref[:, g * gw:(g + 1) * gw] = y.astype(pm_ref.dtype)

    u = jax.nn.gelu(u_ref[...].astype(F32))
    v = _layer_norm_rows(jax.nn.gelu(v_ref[...].astype(F32)), lg_ref[...], lb_ref[...]).astype(BF16)
    hc = SG_WIDTH // SG_HEADS
    tri = (lax.broadcasted_iota(jnp.int32, (SG_CHUNK, SG_CHUNK), 0)
           >= lax.broadcasted_iota(jnp.int32, (SG_CHUNK, SG_CHUNK), 1))
    for g in range(SG_HEADS):
        ws = jnp.where(tri, sw_ref[g], 0.0).astype(BF16)
        bias = sb_ref[:, g:g + 1]
        for c in range(tm // SG_CHUNK):
            rows = slice(c * SG_CHUNK, (c + 1) * SG_CHUNK)
            cols = slice(g * hc, (g + 1) * hc)
            mixed = _dot(ws, v[rows, cols]) + bias
            sg_ref[rows, cols] = (u[rows, cols] * mixed).astype(sg_ref.dtype)


def _mixers(proj, pool_w, pool_scale, ln_g, ln_b, sg_w, sg_b, seq, tm):
    t = proj.shape[0]
    halo_blocks = tm // POOL_HALO
    full = lambda shape: pl.BlockSpec(shape, lambda i: (0,) * len(shape))
    return pl.pallas_call(
        functools.partial(_mixers_kernel, tiles_per_seq=seq // tm),
        grid=(t // tm,),
        in_specs=[pl.BlockSpec((tm, POOL_WIDTH), lambda i: (i, 0)),
                  pl.BlockSpec((POOL_HALO, POOL_WIDTH), lambda i: (jnp.maximum(i * halo_blocks - 1, 0), 0)),
                  pl.BlockSpec((tm, SG_WIDTH), lambda i: (i, 1)),
                  pl.BlockSpec((tm, SG_WIDTH), lambda i: (i, 2)),
                  full(pool_w.shape), full((1, POOL_WIDTH)), full((1, SG_WIDTH)), full((1, SG_WIDTH)),
                  full(sg_w.shape), full((SG_CHUNK, SG_HEADS))],
        out_specs=[pl.BlockSpec((tm, POOL_WIDTH), lambda i: (i, 0)),
                   pl.BlockSpec((tm, SG_WIDTH), lambda i: (i, 0))],
        out_shape=[jax.ShapeDtypeStruct((t, POOL_WIDTH), BF16), jax.ShapeDtypeStruct((t, SG_WIDTH), BF16)],
        compiler_params=_params(("parallel",)),
        name="mixers",
    )(proj, proj, proj, proj, pool_w.astype(BF16), pool_scale[None], ln_g[None], ln_b[None], sg_w, sg_b.T)


def _compress_kernel(ch_ref, pos_ref, w1_ref, w2_ref, o_ref):
    nc, half = ch_ref.shape
    ch = ch_ref[...].astype(F32)
    pos = pos_ref[...]
    first = _dot((ch + pos[:, :half]).astype(BF16), w1_ref[:half, :])
    second = _dot((ch + pos[:, half:]).astype(BF16), w1_ref[half:, :])
    pre = first + pltpu.roll(second, nc - 1, 0)
    o_ref[...] = _dot(jax.nn.gelu(pre).astype(BF16), w2_ref[...]).astype(o_ref.dtype)


def _compress(chunks, pos, w1, w2):
    _, b, h, nc, width = chunks.shape
    sel = lambda s, bi, hi: (s, 0, 0)
    return pl.pallas_call(
        _compress_kernel,
        grid=(2, b, h),
        in_specs=[pl.BlockSpec((None, None, None, nc, width), lambda s, bi, hi: (s, bi, hi, 0, 0)),
                  pl.BlockSpec((None, 1, 2 * width), sel),
                  pl.BlockSpec((None, 2 * width, CMP_HIDDEN), sel),
                  pl.BlockSpec((None, CMP_HIDDEN, HEAD_DIM), sel)],
        out_specs=pl.BlockSpec((None, None, None, nc, HEAD_DIM), lambda s, bi, hi: (s, bi, hi, 0, 0)),
        out_shape=jax.ShapeDtypeStruct((2, b, h, nc, HEAD_DIM), BF16),
        compiler_params=_params(("parallel", "parallel", "parallel")),
        name="compress",
    )(chunks, pos, w1, w2)


def _nsa_kernel(qT_ref, kc_ref, vcT_ref, mt_ref, ks_ref, vsT_ref, kw_ref, vwT_ref, g_ref, o_ref, selneg_ref,
                *, n_top):
    n = pl.program_id(2)
    qs = n * Q_BLOCK
    width = GROUP * Q_BLOCK
    qT = qT_ref[...]
    t_row = qs + lax.broadcasted_iota(jnp.int32, (1, width), 1) % Q_BLOCK
    nc = kc_ref.shape[0]
    ns = mt_ref.shape[0]

    s = _dot(kc_ref[...], qT)
    c_end = lax.broadcasted_iota(jnp.int32, (nc, 1), 0) * CMP_STRIDE + (CMP_LEN - 1)
    valid = c_end <= t_row
    s = jnp.where(valid, s, NEG_INF)
    e = jnp.where(valid, jnp.exp(s - jnp.max(s, axis=0, keepdims=True)), 0.0)
    l = jnp.sum(e, axis=0, keepdims=True)
    p = e * jnp.where(l > 0.0, 1.0 / l, 0.0)
    o_cmp = _dot(vcT_ref[...], p.astype(BF16))

    psum = p[:, :Q_BLOCK]
    for g in range(1, GROUP):
        psum = psum + p[:, g * Q_BLOCK:(g + 1) * Q_BLOCK]
    mt = mt_ref[...]
    imp = jnp.zeros((ns, Q_BLOCK), F32)
    rem = psum
    for _ in range(3):
        piece = rem.astype(BF16)
        imp = imp + _dot(mt, piece)
        rem = rem - piece.astype(F32)

    blk = lax.broadcasted_iota(jnp.int32, (ns, Q_BLOCK), 0)
    cur = (qs + lax.broadcasted_iota(jnp.int32, (ns, Q_BLOCK), 1)) // SLC_LEN
    score = jnp.where(blk == 0, FORCE_SCORE, jnp.where(blk == cur, FORCE_SCORE,
                                                      jnp.where(blk == cur - 1, FORCE_SCORE, imp)))
    score = jnp.where(blk <= cur, score, -1.0)
    blk_f = blk.astype(F32)
    selneg = jnp.full((ns, Q_BLOCK), NEG_INF, F32)
    for _ in range(n_top):
        top = jnp.max(score, axis=0, keepdims=True)
        first = jnp.min(jnp.where(score == top, blk_f, float(ns)), axis=0, keepdims=True)
        hit = blk_f == first
        selneg = jnp.where(hit, 0.0, selneg)
        score = jnp.where(hit, -2.0, score)
    selneg_ref[...] = selneg

    blocks_per_tile = SLC_TILE // SLC_LEN

    def slc_tile(kt, carry, causal):
        m, l, acc = carry
        k0 = pl.multiple_of(kt * SLC_TILE, SLC_TILE)
        s = _dot(ks_ref[pl.ds(k0, SLC_TILE), :], qT)
        b0 = pl.multiple_of(kt * blocks_per_tile, blocks_per_tile)
        bias = selneg_ref[pl.ds(b0, blocks_per_tile), :]
        bias = jnp.concatenate([bias] * GROUP, axis=1)
        s = jnp.concatenate([s[r * SLC_LEN:(r + 1) * SLC_LEN] + bias[r:r + 1]
                             for r in range(blocks_per_tile)], axis=0)
        if causal:
            kpos = k0 + lax.broadcasted_iota(jnp.int32, (SLC_TILE, 1), 0)
            s = jnp.where(kpos <= t_row, s, NEG_INF)
        m_new = jnp.maximum(m, jnp.max(s, axis=0, keepdims=True))
        alpha = jnp.exp(m - m_new)
        pr = jnp.exp(s - m_new)
        l = alpha * l + jnp.sum(pr, axis=0, keepdims=True)
        acc = alpha * acc + _dot(vsT_ref[:, pl.ds(k0, SLC_TILE)], pr.astype(BF16))
        return m_new, l, acc

    n_full = n // (SLC_TILE // Q_BLOCK)
    init = (jnp.full((1, width), NEG_INF, F32), jnp.zeros((1, width), F32), jnp.zeros((HEAD_DIM, width), F32))
    carry = lax.fori_loop(0, n_full, lambda kt, c: slc_tile(kt, c, False), init)
    _, l, acc = slc_tile(n_full, carry, True)
    o_slc = acc / l

    span = WINDOW + Q_BLOCK
    w0 = pl.multiple_of(jnp.maximum(qs - WINDOW, 0), Q_BLOCK)
    s = _dot(kw_ref[pl.ds(w0, span), :], qT)
    diff = t_row - (w0 + lax.broadcasted_iota(jnp.int32, (span, 1), 0))
    s = jnp.where(diff >= 0, jnp.where(diff < WINDOW, s, NEG_INF), NEG_INF)
    pr = jnp.exp(s - jnp.max(s, axis=0, keepdims=True))
    l = jnp.sum(pr, axis=0, keepdims=True)
    o_win = _dot(vwT_ref[:, pl.ds(w0, span)], pr.astype(BF16)) / l

    gates = jax.nn.sigmoid(g_ref[...].astype(F32))

    def gate(c):
        return jnp.concatenate([gates[g * 3 + c:g * 3 + c + 1, :] for g in range(GROUP)], axis=1)

    o_ref[...] = (gate(0) * o_cmp + gate(1) * o_slc + gate(2) * o_win).astype(o_ref.dtype)


def _importance_matrix(nc, ns):
    per = SLC_LEN // CMP_STRIDE
    c = np.arange(nc)[None, :]
    j = np.arange(ns)[:, None]
    m = ((c >= per * j - 1) & (c <= per * j + per - 1)).astype(np.float32)
    m += ((c >= per * j) & (c <= per * j + per - 2)).astype(np.float32)
    return jnp.asarray(m, BF16)


def _nsa(qT, kc, vcT, ks, vsT, kw, vwT, gates, seq):
    b, h = qT.shape[:2]
    nq = seq // Q_BLOCK
    nc = kc.shape[2]
    ns = seq // SLC_LEN
    width = GROUP * Q_BLOCK
    per_head = lambda shape: pl.BlockSpec((None, None) + shape, lambda bi, hi, n: (bi, hi, 0, 0))
    return pl.pallas_call(
        functools.partial(_nsa_kernel, n_top=min(SLC_TOP, ns)),
        grid=(b, h, nq),
        in_specs=[pl.BlockSpec((None, None, HEAD_DIM, width), lambda bi, hi, n: (bi, hi, 0, n)),
                  per_head((nc, HEAD_DIM)), per_head((HEAD_DIM, nc)),
                  pl.BlockSpec((ns, nc), lambda bi, hi, n: (0, 0)),
                  per_head((seq, HEAD_DIM)), per_head((HEAD_DIM, seq)),
                  per_head((seq, HEAD_DIM)), per_head((HEAD_DIM, seq)),
                  pl.BlockSpec((None, None, 16, Q_BLOCK), lambda bi, hi, n: (bi, hi, 0, n))],
        out_specs=pl.BlockSpec((None, None, HEAD_DIM, width), lambda bi, hi, n: (bi, hi, 0, n)),
        out_shape=jax.ShapeDtypeStruct((b, h, HEAD_DIM, nq * width), BF16),
        scratch_shapes=[pltpu.VMEM((ns, Q_BLOCK), F32)],
        compiler_params=_params(("parallel", "parallel", "arbitrary")),
        name="nsa",
    )(qT, kc, vcT, _importance_matrix(nc, ns), ks, vsT, kw, vwT, gates)


def _nsa_layouts(proj, b, seq):
    h = NSA_KV_HEADS
    nq = seq // Q_BLOCK
    col = POOL_WIDTH + 2 * SG_WIDTH
    q = proj[:, col:col + Q_WIDTH].reshape(b, nq, Q_BLOCK, h, GROUP, HEAD_DIM)
    qT = q.transpose(0, 3, 5, 1, 4, 2).reshape(b, h, HEAD_DIM, nq * GROUP * Q_BLOCK)
    col += Q_WIDTH
    heads = [proj[:, col + k * KV_WIDTH:col + (k + 1) * KV_WIDTH].reshape(b, seq, h, HEAD_DIM).transpose(0, 2, 1, 3)
             for k in range(6)]
    kc, vc, ks, vs, kw, vw = heads
    chunk = lambda t: t.reshape(b, h, seq // CMP_STRIDE, CMP_STRIDE * HEAD_DIM)
    chunks = jnp.stack([chunk(kc), chunk(vc)])
    gates = proj[:, GATE_COL:GATE_COL + NSA_Q_HEADS * 3].reshape(b, seq, h, GROUP * 3).transpose(0, 2, 3, 1)
    gates = jnp.pad(gates, ((0, 0), (0, 0), (0, 16 - GROUP * 3), (0, 0)))
    return qT, chunks, ks, vs.transpose(0, 1, 3, 2), kw, vw.transpose(0, 1, 3, 2), gates


def _merge_kernel(*refs, with_router, alpha):
    (pm_ref, sg_ref, on_ref, gp_ref, gs_ref, gn_ref, x_ref, wp_ref, ws_ref, wn_ref, wo_ref, lg_ref, lb_ref) = refs[:13]
    if with_router:
        wr_ref, br_ref, x1_ref, route_ref, x1t_ref = refs[13:]
    else:
        (x1_ref,) = refs[13:]
    sig = lambda r: jax.nn.sigmoid(r[...].astype(F32))
    y = (sig(gp_ref) * _dot(pm_ref[...], wp_ref[...]) + sig(gs_ref) * _dot(sg_ref[...], ws_ref[...])
         + sig(gn_ref) * _dot(on_ref[...], wn_ref[...]))
    hmix = _dot(y.astype(BF16), wo_ref[...])
    x1 = _layer_norm_rows(alpha * x_ref[...] + hmix, lg_ref[...], lb_ref[...])
    x1_ref[...] = x1
    if with_router:
        _store_token_tiles(x1t_ref, x1)
        wr = wr_ref[...]
        w_hi = wr.astype(BF16)
        w_lo = (wr - w_hi.astype(F32)).astype(BF16)
        x_hi = x1.astype(BF16)
        x_lo = (x1 - x_hi.astype(F32)).astype(BF16)
        logits = _dot(x_hi, w_hi) + (_dot(x_lo, w_hi) + _dot(x_hi, w_lo)) + br_ref[...]
        lane = lax.broadcasted_iota(jnp.int32, logits.shape, 1)
        lane_f = lane.astype(F32)
        logits = jnp.where(lane < N_EXPERTS, logits, -jnp.inf)
        m1 = jnp.max(logits, axis=-1, keepdims=True)
        i1 = jnp.min(jnp.where(logits == m1, lane_f, float(LANE)), axis=-1, keepdims=True)
        rest = jnp.where(lane_f == i1, -jnp.inf, logits)
        m2 = jnp.max(rest, axis=-1, keepdims=True)
        i2 = jnp.min(jnp.where(rest == m2, lane_f, float(LANE)), axis=-1, keepdims=True)
        e2 = jnp.exp(m2 - m1)
        w1 = 1.0 / (1.0 + e2)
        route_ref[...] = jnp.where(lane == 0, i1, jnp.where(lane == 1, i2, jnp.where(lane == 2, w1,
                                   jnp.where(lane == 3, e2 * w1, 0.0))))


def _merge(pm, sgm, on, proj, x2, wp, ws, wn, wo, lg, lb, alpha, tm, router=None):
    t = x2.shape[0]
    row = lambda width, cb=0: pl.BlockSpec((tm, width), lambda i, cb=cb: (i, cb))
    full = lambda shape: pl.BlockSpec(shape, lambda i: (0,) * len(shape))
    gate_block = BRANCH_COL // D_MODEL
    in_specs = [row(POOL_WIDTH), row(SG_WIDTH), row(Q_WIDTH),
                row(D_MODEL, gate_block), row(D_MODEL, gate_block + 1), row(D_MODEL, gate_block + 2),
                row(D_MODEL), full(wp.shape), full(ws.shape), full(wn.shape), full(wo.shape),
                full((1, D_MODEL)), full((1, D_MODEL))]
    args = [pm, sgm, on, proj, proj, proj, x2, wp.astype(BF16), ws.astype(BF16), wn.astype(BF16), wo.astype(BF16),
            lg[None], lb[None]]
    out_specs = [row(D_MODEL)]
    out_shape = [jax.ShapeDtypeStruct((t, D_MODEL), F32)]
    if router is not None:
        w_router, b_router = router
        pad = LANE - N_EXPERTS
        in_specs += [full((D_MODEL, LANE)), full((1, LANE))]
        args += [jnp.pad(w_router, ((0, 0), (0, pad))), jnp.pad(b_router, (0, pad))[None]]
        out_specs += [row(LANE), pl.BlockSpec((tm * TOKEN_TILE, LANE), lambda i: (i, 0))]
        out_shape += [jax.ShapeDtypeStruct((t, LANE), F32), jax.ShapeDtypeStruct((t * TOKEN_TILE, LANE), F32)]
    out = pl.pallas_call(
        functools.partial(_merge_kernel, with_router=router is not None, alpha=alpha),
        grid=(t // tm,), in_specs=in_specs, out_specs=out_specs, out_shape=out_shape,
        compiler_params=_params(("parallel",)),
        name="merge",
    )(*args)
    return out if router is not None else out[0]


def _gather_kernel(idx_ref, src_ref, out_ref, sem):
    rows = idx_ref.shape[1]
    base = pl.program_id(0) * rows

    def row_copy(r):
        src = pl.multiple_of(idx_ref[0, r] * TOKEN_TILE, TOKEN_TILE)
        dst = pl.multiple_of((base + r) * TOKEN_TILE, TOKEN_TILE)
        return pltpu.make_async_copy(src_ref.at[pl.ds(src, TOKEN_TILE)], out_ref.at[pl.ds(dst, TOKEN_TILE)], sem)

    def start(r, c):
        row_copy(r).start()
        return c

    def wait(r, c):
        row_copy(r).wait()
        return c

    lax.fori_loop(0, rows, start, 0)
    lax.fori_loop(0, rows, wait, 0)


def _gather_rows(src, idx, rows_per_step):
    n = idx.shape[0]
    steps = n // rows_per_step
    return pl.pallas_call(
        _gather_kernel,
        grid=(steps,),
        in_specs=[pl.BlockSpec((None, 1, rows_per_step), lambda i: (i, 0, 0), memory_space=pltpu.SMEM),
                  pl.BlockSpec(memory_space=pl.ANY)],
        out_specs=pl.BlockSpec(memory_space=pl.ANY),
        out_shape=jax.ShapeDtypeStruct((n * TOKEN_TILE, LANE), src.dtype),
        scratch_shapes=[pltpu.SemaphoreType.DMA(())],
        compiler_params=_params(("arbitrary",)),
        name="gather_rows",
    )(idx.reshape(steps, 1, rows_per_step), src)


def _swiglu_kernel(be_ref, x_ref, wg_ref, wu_ref, wd_ref, o_ref, xb_ref, acc_ref, *, token_tiles):
    k = pl.program_id(1)

    @pl.when(k == 0)
    def _():
        if token_tiles:
            for s in range(TOKEN_TILE):
                xb_ref[:, s * LANE:(s + 1) * LANE] = x_ref[pl.ds(s, xb_ref.shape[0], stride=TOKEN_TILE), :].astype(BF16)
        else:
            xb_ref[...] = x_ref[...].astype(BF16)

    xb = xb_ref[...]
    hidden = (jax.nn.silu(_dot(xb, wg_ref[...])) * _dot(xb, wu_ref[...])).astype(BF16)
    part = _dot(hidden, wd_ref[...])

    @pl.when(k == 0)
    def _():
        acc_ref[...] = part

    @pl.when(k == pl.num_programs(1) - 1)
    def _():
        total = acc_ref[...] + part
        if token_tiles:
            _store_token_tiles(o_ref, total)
        else:
            o_ref[...] = total


def _swiglu(rows, block_e, wg, wu, wd, tm, token_tiles):
    kf = D_FF // FF_TILE
    assert kf == 2
    if token_tiles:
        r = rows.shape[0] // TOKEN_TILE
        io_spec = pl.BlockSpec((tm * TOKEN_TILE, LANE), lambda i, k, be: (i, 0))
    else:
        r = rows.shape[0]
        io_spec = pl.BlockSpec((tm, D_MODEL), lambda i, k, be: (i, 0))
    return pl.pallas_call(
        functools.partial(_swiglu_kernel, token_tiles=token_tiles),
        grid_spec=pltpu.PrefetchScalarGridSpec(
            num_scalar_prefetch=1,
            grid=(r // tm, kf),
            in_specs=[io_spec,
                      pl.BlockSpec((None, D_MODEL, FF_TILE), lambda i, k, be: (be[i], 0, k)),
                      pl.BlockSpec((None, D_MODEL, FF_TILE), lambda i, k, be: (be[i], 0, k)),
                      pl.BlockSpec((None, FF_TILE, D_MODEL), lambda i, k, be: (be[i], k, 0))],
            out_specs=io_spec,
            scratch_shapes=[pltpu.VMEM((tm, D_MODEL), BF16), pltpu.VMEM((tm, D_MODEL), F32)]),
        out_shape=jax.ShapeDtypeStruct(rows.shape, F32),
        compiler_params=_params(("parallel", "arbitrary")),
        name="swiglu",
    )(block_e, rows, wg, wu, wd)


def _ple_ln2_kernel(*refs, n_parts, alpha):
    x_ref, p_ref = refs[0], refs[1]
    parts = refs[2:2 + n_parts]
    rest = refs[2 + n_parts:]
    if n_parts > 1:
        route_ref, rest = rest[0], rest[1:]
    wg_ref, bg_ref, wp_ref, lg_ref, lb_ref, o_ref = rest
    x1 = x_ref[...]
    if n_parts > 1:
        route = route_ref[...]
        f = _load_token_tiles(parts[0]) * route[:, TOP_K:TOP_K + 1]
        for k in range(1, n_parts):
            f = f + _load_token_tiles(parts[k]) * route[:, TOP_K + k:TOP_K + k + 1]
    else:
        f = parts[0][...]
    gate = jax.nn.sigmoid(_dot(x1.astype(BF16), wg_ref[...]) + bg_ref[...])
    ple = gate * _dot(p_ref[...].astype(BF16), wp_ref[...])
    o_ref[...] = _layer_norm_rows(alpha * x1 + f + ple, lg_ref[...], lb_ref[...])


def _ple_ln2(x1, p2, y, route, wg, bg, wp, lg, lb, alpha, tm):
    t = x1.shape[0]
    n_parts = 1 if route is None else y.shape[0] // (t * TOKEN_TILE)
    steps = t // tm
    row = lambda width: pl.BlockSpec((tm, width), lambda i: (i, 0))
    full = lambda shape: pl.BlockSpec(shape, lambda i: (0,) * len(shape))
    in_specs = [row(D_MODEL), row(PLE_DIM)]
    if n_parts > 1:
        in_specs += [pl.BlockSpec((tm * TOKEN_TILE, LANE), lambda i, k=k: (k * steps + i, 0)) for k in range(n_parts)]
    else:
        in_specs.append(row(D_MODEL))
    args = [x1, p2] + [y] * n_parts
    if n_parts > 1:
        in_specs.append(row(LANE))
        args.append(route)
    in_specs += [full((D_MODEL, D_MODEL)), full((1, D_MODEL)), full((PLE_DIM, D_MODEL)),
                 full((1, D_MODEL)), full((1, D_MODEL))]
    args += [wg.astype(BF16), bg[None], wp.astype(BF16), lg[None], lb[None]]
    return pl.pallas_call(
        functools.partial(_ple_ln2_kernel, n_parts=n_parts, alpha=alpha),
        grid=(steps,), in_specs=in_specs, out_specs=row(D_MODEL),
        out_shape=jax.ShapeDtypeStruct((t, D_MODEL), F32),
        compiler_params=_params(("parallel",)),
        name="ple_ln2",
    )(*args)


def _route_tables(route, t):
    flat_e = route[:, :TOP_K].astype(jnp.int32).reshape(-1)
    tk = flat_e.shape[0]
    onehot = (flat_e[:, None] == jnp.arange(N_EXPERTS, dtype=jnp.int32)[None, :]).astype(jnp.int32)
    before = jnp.cumsum(onehot, axis=0) - onehot
    rank = jnp.sum(before * onehot, axis=1)
    counts = jnp.sum(onehot, axis=0)
    padded = (counts + EXPERT_BLOCK - 1) // EXPERT_BLOCK * EXPERT_BLOCK
    pend = jnp.cumsum(padded)
    dest = (pend - padded)[flat_e] + rank
    n_blocks = -(-tk // EXPERT_BLOCK) + N_EXPERTS
    n_rows = n_blocks * EXPERT_BLOCK
    row_tok = jnp.zeros((n_rows,), jnp.int32).at[dest].set(jnp.arange(tk, dtype=jnp.int32) // TOP_K)
    block_start = jnp.arange(n_blocks, dtype=jnp.int32) * EXPERT_BLOCK
    block_e = jnp.sum((pend[None, :] <= block_start[:, None]).astype(jnp.int32), axis=1)
    block_e = jnp.minimum(block_e, N_EXPERTS - 1).astype(jnp.int32)
    pair_rows = dest.reshape(t, TOP_K).T.reshape(-1)
    return row_tok, block_e, pair_rows


def kernel(x, p, positions, w_in, pool_w, pool_scale, sg_ln_g, sg_ln_b, sg_w, sg_b, cmp_k_pos, cmp_k_w1, cmp_k_w2, cmp_v_pos, cmp_v_w1, cmp_v_w2, w_pool_out, w_sg_out, w_nsa_out, w_out, ln1_g, ln1_b, ffn_w_gate, ffn_w_up, ffn_w_down, moe_router, moe_router_b, moe_w_gate, moe_w_up, moe_w_down, ple_gate_w, ple_gate_b, ple_proj, ln2_g, ln2_b):
    b, seq, d = x.shape
    depth = w_in.shape[0]
    t = b * seq
    alpha = (2 * depth) ** 0.25
    tm = 512
    tabs = _rope_tables(positions)
    x2 = x.reshape(t, d)
    for i in range(depth):
        proj = _proj(x2, _prep_w_in(w_in[i]), tabs, min(1024, t))
        pm, sgm = _mixers(proj, pool_w[i], pool_scale[i], sg_ln_g[i], sg_ln_b[i], sg_w[i], sg_b[i], seq, tm)
        qT, chunks, ks, vsT, kw, vwT, gates = _nsa_layouts(proj, b, seq)
        pos = jnp.stack([cmp_k_pos[i], cmp_v_pos[i]]).reshape(2, 1, CMP_LEN * HEAD_DIM)
        cmp = _compress(chunks, pos, jnp.stack([cmp_k_w1[i], cmp_v_w1[i]]).astype(BF16),
                        jnp.stack([cmp_k_w2[i], cmp_v_w2[i]]).astype(BF16))
        oT = _nsa(qT, cmp[0], cmp[1].transpose(0, 1, 3, 2), ks, vsT, kw, vwT, gates, seq)
        nq = seq // Q_BLOCK
        on = oT.reshape(b, NSA_KV_HEADS, HEAD_DIM, nq, GROUP, Q_BLOCK).transpose(0, 3, 5, 1, 4, 2).reshape(t, Q_WIDTH)
        j = i // 2
        moe = i % 2 == 1
        merged = _merge(pm, sgm, on, proj, x2, w_pool_out[i], w_sg_out[i], w_nsa_out[i], w_out[i],
                        ln1_g[i], ln1_b[i], alpha, tm, router=(moe_router[j], moe_router_b[j]) if moe else None)
        if moe:
            x1, route, x1_tiles = merged
            row_tok, block_e, pair_rows = _route_tables(route, t)
            rows = _gather_rows(x1_tiles, row_tok, EXPERT_BLOCK)
            y = _swiglu(rows, block_e, moe_w_gate[j].astype(BF16), moe_w_up[j].astype(BF16),
                        moe_w_down[j].astype(BF16), EXPERT_BLOCK, token_tiles=True)
            y = _gather_rows(y, pair_rows, EXPERT_BLOCK)
        else:
            x1, route = merged, None
            y = _swiglu(x1, jnp.zeros((t // tm,), jnp.int32), ffn_w_gate[j][None].astype(BF16),
                        ffn_w_up[j][None].astype(BF16), ffn_w_down[j][None].astype(BF16), tm, token_tiles=False)
        x2 = _ple_ln2(x1, p[i].reshape(t, PLE_DIM), y, route, ple_gate_w[i], ple_gate_b[i], ple_proj[i],
                      ln2_g[i], ln2_b[i], alpha, tm)
    return x2.reshape(b, seq, d)
```

```python
import functools

import jax
import jax.numpy as jnp
import numpy as np
from jax import lax
from jax.experimental import pallas as pl
from jax.experimental.pallas import tpu as pltpu

F32 = jnp.float32
BF16 = jnp.bfloat16

D_MODEL = 1024
POOL_WIDTH = 512
POOL_WINDOWS = (2, 4, 8, 16)
POOL_HALO = 16
SG_WIDTH = 512
SG_HEADS = 4
SG_CHUNK = 128
NSA_Q_HEADS = 8
NSA_KV_HEADS = 2
GROUP = NSA_Q_HEADS // NSA_KV_HEADS
HEAD_DIM = 64
Q_WIDTH = NSA_Q_HEADS * HEAD_DIM
KV_WIDTH = NSA_KV_HEADS * HEAD_DIM
ROT_DIM = HEAD_DIM // 4
ROPE_THETA = 500000.0
CMP_LEN = 32
CMP_STRIDE = 16
CMP_HIDDEN = 256
SLC_LEN = 64
SLC_TOP = 16
WINDOW = 512
Q_BLOCK = 128
NEG_INF = -1e30
FORCE_SCORE = 1e4
D_FF = 2816
N_EXPERTS = 8
TOP_K = 2
EXPERT_BLOCK = 512
PLE_DIM = 256
LN_EPS = 1e-5

LANE = 128
TOKEN_TILE = D_MODEL // LANE
PROJ_BLOCK = 256
GATE_COL = 2816
BRANCH_COL = 3072
PROJ_COLS = BRANCH_COL + 3 * D_MODEL
NSA_STRIP = 64
SLC_BLOCK = 256
CMP_TILE = 256
SEL_GROUP = 8
ONES_PAD = 16
PSUM_PAD = 8
LOG2_E = 1.4426950408889634
FF_TILE = 1408
VMEM_LIMIT = 56 * 1024 * 1024


def _params(semantics, vmem=VMEM_LIMIT):
    return pltpu.CompilerParams(dimension_semantics=semantics, vmem_limit_bytes=vmem)


def _dot(a, b):
    return jnp.dot(a, b, preferred_element_type=F32)


def _store_token_tiles(ref, v):
    n = v.shape[0]
    for s in range(TOKEN_TILE):
        ref[pl.ds(s, n, stride=TOKEN_TILE), :] = v[:, s * LANE:(s + 1) * LANE]


def _load_token_tiles(ref):
    n = ref.shape[0] // TOKEN_TILE
    return jnp.concatenate([ref[pl.ds(s, n, stride=TOKEN_TILE), :] for s in range(TOKEN_TILE)], axis=1)


def _layer_norm_rows(v, g, b):
    mu = jnp.mean(v, axis=-1, keepdims=True)
    c = v - mu
    var = jnp.mean(c * c, axis=-1, keepdims=True)
    return c * lax.rsqrt(var + LN_EPS) * g + b


def _rope_tables(positions):
    half = ROT_DIM // 2
    inv = ROPE_THETA ** (-jnp.arange(half, dtype=F32) / half)
    ang = positions.astype(F32)[:, :, None] * inv
    cos, sin = jnp.cos(ang), jnp.sin(ang)
    rest = HEAD_DIM - ROT_DIM
    one = jnp.ones(ang.shape[:2] + (rest,), F32)
    zero_r = jnp.zeros(ang.shape[:2] + (rest,), F32)
    zero_h = jnp.zeros_like(sin)
    c = jnp.concatenate([cos, cos, one], -1)
    s1 = jnp.concatenate([zero_h, sin, zero_r], -1)
    s2 = jnp.concatenate([-sin, zero_h, zero_r], -1)
    reps = LANE // HEAD_DIM
    return tuple(jnp.tile(t, (1, 1, reps)).reshape(-1, LANE) for t in (c, s1, s2))


def _proj_kernel(x_ref, w_ref, c_ref, s1_ref, s2_ref, o_ref, xb_ref):
    j = pl.program_id(1)
    half = ROT_DIM // 2

    @pl.when(j == 0)
    def _():
        xb_ref[...] = x_ref[...].astype(BF16)

    acc = _dot(xb_ref[...], w_ref[...])

    def rope(v):
        return (v * c_ref[...] + pltpu.roll(v, half, 1) * s1_ref[...]
                + pltpu.roll(v, LANE - half, 1) * s2_ref[...])

    q_first = (POOL_WIDTH + 2 * SG_WIDTH) // PROJ_BLOCK
    kv_first = q_first + Q_WIDTH // PROJ_BLOCK
    is_q = (j >= q_first) & (j < kv_first)
    is_kv = (j >= kv_first) & (j < kv_first + 3)

    @pl.when(is_q)
    def _():
        scale = HEAD_DIM ** -0.5 * LOG2_E
        o_ref[:, :LANE] = (rope(acc[:, :LANE]) * scale).astype(o_ref.dtype)
        o_ref[:, LANE:] = (rope(acc[:, LANE:]) * scale).astype(o_ref.dtype)

    @pl.when(is_kv)
    def _():
        o_ref[:, :LANE] = rope(acc[:, :LANE]).astype(o_ref.dtype)
        o_ref[:, LANE:] = acc[:, LANE:].astype(o_ref.dtype)

    @pl.when(jnp.logical_not(is_q | is_kv))
    def _():
        o_ref[...] = acc.astype(o_ref.dtype)


def _proj(x2, w, tabs, tm):
    t = x2.shape[0]
    nj = PROJ_COLS // PROJ_BLOCK
    row = lambda i, j: (i, 0)
    return pl.pallas_call(
        _proj_kernel,
        grid=(t // tm, nj),
        in_specs=[pl.BlockSpec((tm, D_MODEL), row),
                  pl.BlockSpec((D_MODEL, PROJ_BLOCK), lambda i, j: (0, j)),
                  pl.BlockSpec((tm, LANE), row), pl.BlockSpec((tm, LANE), row), pl.BlockSpec((tm, LANE), row)],
        out_specs=pl.BlockSpec((tm, PROJ_BLOCK), lambda i, j: (i, j)),
        out_shape=jax.ShapeDtypeStruct((t, PROJ_COLS), BF16),
        scratch_shapes=[pltpu.VMEM((tm, D_MODEL), BF16)],
        compiler_params=_params(("parallel", "arbitrary")),
        name="proj",
    )(x2, w, *tabs)


def _prep_w_in(w):
    n_gate = NSA_Q_HEADS * 3
    main, gate, branch = w[:, :GATE_COL], w[:, GATE_COL:GATE_COL + n_gate], w[:, GATE_COL + n_gate:]
    pad = jnp.zeros((w.shape[0], BRANCH_COL - GATE_COL - n_gate), w.dtype)
    return jnp.concatenate([main, gate, pad, branch], axis=1).astype(BF16)


def _mixers_kernel(a_ref, ap_ref, u_ref, v_ref, pw_ref, ps_ref, lg_ref, lb_ref, sw_ref, sb_ref,
                   pm_ref, sg_ref, *, tiles_per_seq):
    i = pl.program_id(0)
    tm = a_ref.shape[0]
    it = i % tiles_per_seq
    gw = POOL_WIDTH // len(POOL_WINDOWS)

    a = a_ref[...].astype(F32)
    prev = jnp.where(it == 0, 0.0, ap_ref[...].astype(F32))
    ext = jnp.concatenate([prev, a], axis=0)
    tpos = (it * tm + lax.broadcasted_iota(jnp.int32, (tm, 1), 0) + 1).astype(F32)
    for g, w in enumerate(POOL_WINDOWS):
        s = ext[:, g * gw:(g + 1) * gw]
        k = 1
        while k < w:
            s = s + pltpu.roll(s, k, 0)
            k *= 2
        d = s[POOL_HALO:] / jnp.minimum(tpos, float(w)) - a[:, g * gw:(g + 1) * gw]
        y = _dot(d.astype(BF16), pw_ref[g]) * ps_ref[:, g * gw:(g + 1) * gw]
        pm_ref[:, g * gw:(g + 1) * gw] = y.astype(pm_ref.dtype)

    u = jax.nn.gelu(u_ref[...].astype(F32))
    v = _layer_norm_rows(jax.nn.gelu(v_ref[...].astype(F32)), lg_ref[...], lb_ref[...]).astype(BF16)
    hc = SG_WIDTH // SG_HEADS
    tri = (lax.broadcasted_iota(jnp.int32, (SG_CHUNK, SG_CHUNK), 0)
           >= lax.broadcasted_iota(jnp.int32, (SG_CHUNK, SG_CHUNK), 1))
    for g in range(SG_HEADS):
        ws = jnp.where(tri, sw_ref[g], 0.0).astype(BF16)
        bias = sb_ref[:, g:g + 1]
        for c in range(tm // SG_CHUNK):
            rows = slice(c * SG_CHUNK, (c + 1) * SG_CHUNK)
            cols = slice(g * hc, (g + 1) * hc)
            mixed = _dot(ws, v[rows, cols]) + bias
            sg_ref[rows, cols] = (u[rows, cols] * mixed).astype(sg_ref.dtype)


def _mixers(proj, pool_w, pool_scale, ln_g, ln_b, sg_w, sg_b, seq, tm):
    t = proj.shape[0]
    halo_blocks = tm // POOL_HALO
    full = lambda shape: pl.BlockSpec(shape, lambda i: (0,) * len(shape))
    return pl.pallas_call(
        functools.partial(_mixers_kernel, tiles_per_seq=seq // tm),
        grid=(t // tm,),
        in_specs=[pl.BlockSpec((tm, POOL_WIDTH), lambda i: (i, 0)),
                  pl.BlockSpec((POOL_HALO, POOL_WIDTH), lambda i: (jnp.maximum(i * halo_blocks - 1, 0), 0)),
                  pl.BlockSpec((tm, SG_WIDTH), lambda i: (i, 1)),
                  pl.BlockSpec((tm, SG_WIDTH), lambda i: (i, 2)),
                  full(pool_w.shape), full((1, POOL_WIDTH)), full((1, SG_WIDTH)), full((1, SG_WIDTH)),
                  full(sg_w.shape), full((SG_CHUNK, SG_HEADS))],
        out_specs=[pl.BlockSpec((tm, POOL_WIDTH), lambda i: (i, 0)),
                   pl.BlockSpec((tm, SG_WIDTH), lambda i: (i, 0))],
        out_shape=[jax.ShapeDtypeStruct((t, POOL_WIDTH), BF16), jax.ShapeDtypeStruct((t, SG_WIDTH), BF16)],
        compiler_params=_params(("parallel",)),
        name="mixers",
    )(proj, proj, proj, proj, pool_w.astype(BF16), pool_scale[None], ln_g[None], ln_b[None], sg_w, sg_b.T)


def _compress_kernel(ch_ref, pos_ref, w1_ref, w2_ref, o_ref):
    nc, half = ch_ref.shape
    ch = ch_ref[...].astype(F32)
    pos = pos_ref[...]
    first = _dot((ch + pos[:, :half]).astype(BF16), w1_ref[:half, :])
    second = _dot((ch + pos[:, half:]).astype(BF16), w1_ref[half:, :])
    pre = first + pltpu.roll(second, nc - 1, 0)
    o_ref[...] = _dot(jax.nn.gelu(pre).astype(BF16), w2_ref[...]).astype(o_ref.dtype)


def _compress(chunks, pos, w1, w2):
    _, b, h, nc, width = chunks.shape
    sel = lambda s, bi, hi: (s, 0, 0)
    return pl.pallas_call(
        _compress_kernel,
        grid=(2, b, h),
        in_specs=[pl.BlockSpec((None, None, None, nc, width), lambda s, bi, hi: (s, bi, hi, 0, 0)),
                  pl.BlockSpec((None, 1, 2 * width), sel),
                  pl.BlockSpec((None, 2 * width, CMP_HIDDEN), sel),
                  pl.BlockSpec((None, CMP_HIDDEN, HEAD_DIM), sel)],
        out_specs=pl.BlockSpec((None, None, None, nc, HEAD_DIM), lambda s, bi, hi: (s, bi, hi, 0, 0)),
        out_shape=jax.ShapeDtypeStruct((2, b, h, nc, HEAD_DIM), BF16),
        compiler_params=_params(("parallel", "parallel", "parallel")),
        name="compress",
    )(chunks, pos, w1, w2)


def _score_strips(k_ref, k0, n_strips, q_aug, s_ref, mask):
    mx = None
    for r in range(n_strips):
        start = pl.multiple_of(k0 + r * NSA_STRIP, NSA_STRIP)
        s = _dot(k_ref[pl.ds(start, NSA_STRIP), :], q_aug)
        if mask is not None:
            s = mask(r, start, s)
        s_ref[r * NSA_STRIP:(r + 1) * NSA_STRIP, :] = s
        mx = s if mx is None else jnp.maximum(mx, s)
    return jnp.max(mx, axis=0, keepdims=True)


def _prob_strips(n_strips, m, s_ref, p_ref):
    for r in range(n_strips):
        rows = slice(r * NSA_STRIP, (r + 1) * NSA_STRIP)
        p_ref[rows, :] = jnp.exp2(s_ref[rows, :] - m).astype(BF16)


def _nsa_kernel(qT_ref, kc_ref, vcT_ref, ks_ref, vsT_ref, kw_ref, vwT_ref, g_ref, o_ref,
                s_ref, p_ref, pc_ref, s2_ref, p2_ref, psum_ref, selneg_ref, *, n_top):
    n = pl.program_id(2)
    qs = n * Q_BLOCK
    width = GROUP * Q_BLOCK
    qT = qT_ref[...]
    t_row = qs + lax.broadcasted_iota(jnp.int32, (1, width), 1) % Q_BLOCK
    nc = kc_ref.shape[0]
    ns = selneg_ref.shape[0]
    strip_iota = lax.broadcasted_iota(jnp.int32, (NSA_STRIP, 1), 0)

    @pl.when(n == 0)
    def _():
        psum_ref[...] = jnp.zeros(psum_ref.shape, F32)
        pc_ref[...] = jnp.zeros(pc_ref.shape, BF16)

    n_ct = (n * (Q_BLOCK // CMP_STRIDE) + (Q_BLOCK - CMP_LEN) // CMP_STRIDE + 1 + CMP_TILE - 1) // CMP_TILE
    strips_per_tile = CMP_TILE // NSA_STRIP

    def cmp_scores(jt, mx):
        for r in range(strips_per_tile):
            start = pl.multiple_of(jt * CMP_TILE + r * NSA_STRIP, NSA_STRIP)
            s = _dot(kc_ref[pl.ds(start, NSA_STRIP), :], qT)
            c_end = (start + strip_iota) * CMP_STRIDE + (CMP_LEN - 1)
            s = jnp.where(c_end <= t_row, s, NEG_INF)
            s_ref[pl.ds(start, NSA_STRIP), :] = s
            mx = jnp.maximum(mx, s)
        return mx

    mx = lax.fori_loop(0, n_ct, cmp_scores, jnp.full((NSA_STRIP, width), NEG_INF, F32))
    m_cmp = jnp.maximum(jnp.max(mx, axis=0, keepdims=True), 0.1 * NEG_INF)

    def cmp_probs(jt, lsum):
        for r in range(strips_per_tile):
            rows = pl.ds(pl.multiple_of(jt * CMP_TILE + r * NSA_STRIP, NSA_STRIP), NSA_STRIP)
            e = jnp.exp2(s_ref[rows, :] - m_cmp)
            s_ref[rows, :] = e
            pc_ref[rows, :] = e.astype(BF16)
            lsum = lsum + e
        return lsum

    lsum = lax.fori_loop(0, n_ct, cmp_probs, jnp.zeros((NSA_STRIP, width), F32))
    l = jnp.sum(lsum, axis=0, keepdims=True)
    inv = jnp.where(l > 0.0, 1.0 / l, 0.0)
    o_cmp = _dot(vcT_ref[...], pc_ref[...]) * inv

    def cmp_group_sum(jt, c):
        for r in range(strips_per_tile):
            start = pl.multiple_of(jt * CMP_TILE + r * NSA_STRIP, NSA_STRIP)
            p = s_ref[pl.ds(start, NSA_STRIP), :] * inv
            psum = p[:, :Q_BLOCK]
            for g in range(1, GROUP):
                psum = psum + p[:, g * Q_BLOCK:(g + 1) * Q_BLOCK]
            psum_ref[pl.ds(pl.multiple_of(PSUM_PAD + start, 8), NSA_STRIP), :] = psum
        return c

    lax.fori_loop(0, n_ct, cmp_group_sum, 0)

    per = SLC_LEN // CMP_STRIDE
    part = [psum_ref[pl.ds(PSUM_PAD + k, ns, stride=per), :] for k in range(per)]
    before = psum_ref[pl.ds(PSUM_PAD - 1, ns, stride=per), :]
    imp = 2.0 * (part[0] + part[1] + part[2]) + part[3] + before

    blk = lax.broadcasted_iota(jnp.int32, (ns, Q_BLOCK), 0)
    cur = (qs + lax.broadcasted_iota(jnp.int32, (ns, Q_BLOCK), 1)) // SLC_LEN
    taken = -2.0
    score = jnp.where(blk <= cur, imp, -1.0)
    score = jnp.where(blk == 0, taken, jnp.where(blk == cur, taken, jnp.where(blk == cur - 1, taken, score)))
    blk_f = blk.astype(F32)
    for _ in range(n_top - 3):
        top = jnp.max(score, axis=0, keepdims=True)
        cand = jnp.where(score == top, blk_f, float(ns))
        first = jnp.min(cand, axis=0, keepdims=True)
        score = jnp.where(cand == first, taken, score)
    selneg_ref[...] = jnp.where(score == taken, 0.0, NEG_INF)

    strips = SLC_BLOCK // NSA_STRIP
    sel_rows = ks_ref.shape[1] - HEAD_DIM - SEL_GROUP

    last_block = ks_ref.shape[0] // SLC_BLOCK - 1

    def past_only(r, start, s):
        return jnp.where(start + strip_iota <= t_row, s, NEG_INF)

    def slc_scores(kb, slot):
        kb = jnp.minimum(kb, last_block)
        k0 = pl.multiple_of(kb * SLC_BLOCK, SLC_BLOCK)
        g0 = pl.multiple_of((kb * strips // SEL_GROUP) * SEL_GROUP, SEL_GROUP)
        bias = jnp.concatenate([selneg_ref[pl.ds(g0, SEL_GROUP), :]] * GROUP, axis=1)
        bias = jnp.concatenate([bias, jnp.zeros((sel_rows, width), F32)], axis=0).astype(BF16)
        q_aug = jnp.concatenate([qT, bias], axis=0)
        return _score_strips(ks_ref, k0, strips, q_aug, s2_ref.at[slot], past_only)

    def slc_consume(kb, slot, m, acc, mx):
        k0 = pl.multiple_of(kb * SLC_BLOCK, SLC_BLOCK)
        m_new = jnp.maximum(m, mx)
        _prob_strips(strips, m_new, s2_ref.at[slot], p2_ref.at[slot])
        pv = _dot(vsT_ref[:, pl.ds(k0, SLC_BLOCK)], p2_ref[slot])
        return m_new, jnp.exp2(m - m_new) * acc + pv

    def slc_pair(j, carry):
        m, acc, mx_even = carry
        kb = 2 * j
        mx_odd = slc_scores(kb + 1, 1)
        m, acc = slc_consume(kb, 0, m, acc, mx_even)
        mx_even = slc_scores(kb + 2, 0)
        m, acc = slc_consume(kb + 1, 1, m, acc, mx_odd)
        return m, acc, mx_even

    n_blocks = n // (SLC_BLOCK // Q_BLOCK) + 1
    init = (jnp.full((1, width), NEG_INF, F32), jnp.zeros((vsT_ref.shape[0], width), F32), slc_scores(0, 0))
    _, acc, _ = lax.fori_loop(0, (n_blocks + 1) // 2, slc_pair, init)
    o_slc = acc[:HEAD_DIM] / acc[HEAD_DIM:HEAD_DIM + 1]

    span = WINDOW + Q_BLOCK
    w_strips = span // NSA_STRIP
    edge = Q_BLOCK // NSA_STRIP
    w0 = pl.multiple_of(jnp.maximum(qs - WINDOW, 0), Q_BLOCK)

    def band_edges(r, start, s):
        if r < edge:
            return jnp.where(t_row - (start + strip_iota) < WINDOW, s, NEG_INF)
        if r >= w_strips - edge:
            return past_only(r, start, s)
        return s

    m_win = lax.cond(n >= WINDOW // Q_BLOCK,
                     lambda: _score_strips(kw_ref, w0, w_strips, qT, s_ref, band_edges),
                     lambda: _score_strips(kw_ref, w0, w_strips, qT, s_ref, past_only))
    _prob_strips(w_strips, m_win, s_ref, p_ref)
    pv = _dot(vwT_ref[:, pl.ds(w0, span)], p_ref[:span, :])
    o_win = pv[:HEAD_DIM] / pv[HEAD_DIM:HEAD_DIM + 1]

    gates = jax.nn.sigmoid(g_ref[...].astype(F32))

    def gate(c):
        return jnp.concatenate([gates[g * 3 + c:g * 3 + c + 1, :] for g in range(GROUP)], axis=1)

    o_ref[...] = (gate(0) * o_cmp + gate(1) * o_slc + gate(2) * o_win).astype(o_ref.dtype)


def _nsa(qT, kc, vcT, ks, vsT, kw, vwT, gates, seq):
    b, h = qT.shape[:2]
    nq = seq // Q_BLOCK
    nc = kc.shape[2]
    ns = seq // SLC_LEN
    width = GROUP * Q_BLOCK
    assert nc == ns * (SLC_LEN // CMP_STRIDE) and nc % CMP_TILE == 0 and seq % SLC_BLOCK == 0
    assert seq >= WINDOW + Q_BLOCK and ns >= SLC_TOP
    per_head = lambda shape: pl.BlockSpec((None, None) + shape, lambda bi, hi, n: (bi, hi, 0, 0))
    return pl.pallas_call(
        functools.partial(_nsa_kernel, n_top=SLC_TOP),
        grid=(b, h, nq),
        in_specs=[pl.BlockSpec((None, None, HEAD_DIM, width), lambda bi, hi, n: (bi, hi, 0, n)),
                  per_head((nc, HEAD_DIM)), per_head((HEAD_DIM, nc)),
                  per_head(ks.shape[2:]), per_head(vsT.shape[2:]),
                  per_head(kw.shape[2:]), per_head(vwT.shape[2:]),
                  pl.BlockSpec((None, None, 16, Q_BLOCK), lambda bi, hi, n: (bi, hi, 0, n))],
        out_specs=pl.BlockSpec((None, None, HEAD_DIM, width), lambda bi, hi, n: (bi, hi, 0, n)),
        out_shape=jax.ShapeDtypeStruct((b, h, HEAD_DIM, nq * width), BF16),
        scratch_shapes=[pltpu.VMEM((max(nc, WINDOW + Q_BLOCK), width), F32),
                        pltpu.VMEM((WINDOW + Q_BLOCK, width), BF16),
                        pltpu.VMEM((nc, width), BF16),
                        pltpu.VMEM((2, SLC_BLOCK, width), F32),
                        pltpu.VMEM((2, SLC_BLOCK, width), BF16),
                        pltpu.VMEM((PSUM_PAD + nc, Q_BLOCK), F32),
                        pltpu.VMEM((ns, Q_BLOCK), F32)],
        compiler_params=_params(("parallel", "parallel", "arbitrary")),
        name="nsa",
    )(qT, kc, vcT, ks, vsT, kw, vwT, gates)


def _nsa_layouts(proj, b, seq):
    h = NSA_KV_HEADS
    nq = seq // Q_BLOCK
    col = POOL_WIDTH + 2 * SG_WIDTH
    q = proj[:, col:col + Q_WIDTH].reshape(b, nq, Q_BLOCK, h, GROUP, HEAD_DIM)
    qT = q.transpose(0, 3, 5, 1, 4, 2).reshape(b, h, HEAD_DIM, nq * GROUP * Q_BLOCK)
    col += Q_WIDTH
    heads = [proj[:, col + k * KV_WIDTH:col + (k + 1) * KV_WIDTH].reshape(b, seq, h, HEAD_DIM).transpose(0, 2, 1, 3)
             for k in range(6)]
    kc, vc, ks, vs, kw, vw = heads
    chunk = lambda t: t.reshape(b, h, seq // CMP_STRIDE, CMP_STRIDE * HEAD_DIM)
    chunks = jnp.stack([chunk(kc), chunk(vc)])
    gates = proj[:, GATE_COL:GATE_COL + NSA_Q_HEADS * 3].reshape(b, seq, h, GROUP * 3).transpose(0, 2, 3, 1)
    gates = jnp.pad(gates, ((0, 0), (0, 0), (0, 16 - GROUP * 3), (0, 0)))
    sel = (jnp.arange(seq, dtype=jnp.int32) // SLC_LEN) % SEL_GROUP
    onehot = (sel[:, None] == jnp.arange(LANE - HEAD_DIM, dtype=jnp.int32)[None, :]).astype(BF16)
    ks = jnp.concatenate([ks, jnp.broadcast_to(onehot, (b, h, seq, LANE - HEAD_DIM))], axis=-1)

    def values_t(v):
        ones = jnp.ones((b, h, 1, seq), BF16)
        return jnp.concatenate([v.transpose(0, 1, 3, 2), ones, jnp.zeros((b, h, ONES_PAD - 1, seq), BF16)], axis=2)

    return qT, chunks, ks, values_t(vs), kw, values_t(vw), gates


def _merge_kernel(*refs, with_router, alpha):
    (pm_ref, sg_ref, on_ref, gp_ref, gs_ref, gn_ref, x_ref, wp_ref, ws_ref, wn_ref, wo_ref, lg_ref, lb_ref) = refs[:13]
    if with_router:
        wr_ref, br_ref, x1_ref, route_ref, x1t_ref = refs[13:]
    else:
        (x1_ref,) = refs[13:]
    sig = lambda r: jax.nn.sigmoid(r[...].astype(F32))
    y = (sig(gp_ref) * _dot(pm_ref[...], wp_ref[...]) + sig(gs_ref) * _dot(sg_ref[...], ws_ref[...])
         + sig(gn_ref) * _dot(on_ref[...], wn_ref[...]))
    hmix = _dot(y.astype(BF16), wo_ref[...])
    x1 = _layer_norm_rows(alpha * x_ref[...] + hmix, lg_ref[...], lb_ref[...])
    x1_ref[...] = x1
    if with_router:
        _store_token_tiles(x1t_ref, x1)
        wr = wr_ref[...]
        w_hi = wr.astype(BF16)
        w_lo = (wr - w_hi.astype(F32)).astype(BF16)
        x_hi = x1.astype(BF16)
        x_lo = (x1 - x_hi.astype(F32)).astype(BF16)
        logits = _dot(x_hi, w_hi) + (_dot(x_lo, w_hi) + _dot(x_hi, w_lo)) + br_ref[...]
        lane = lax.broadcasted_iota(jnp.int32, logits.shape, 1)
        lane_f = lane.astype(F32)
        logits = jnp.where(lane < N_EXPERTS, logits, -jnp.inf)
        m1 = jnp.max(logits, axis=-1, keepdims=True)
        i1 = jnp.min(jnp.where(logits == m1, lane_f, float(LANE)), axis=-1, keepdims=True)
        rest = jnp.where(lane_f == i1, -jnp.inf, logits)
        m2 = jnp.max(rest, axis=-1, keepdims=True)
        i2 = jnp.min(jnp.where(rest == m2, lane_f, float(LANE)), axis=-1, keepdims=True)
        e2 = jnp.exp(m2 - m1)
        w1 = 1.0 / (1.0 + e2)
        route_ref[...] = jnp.where(lane == 0, i1, jnp.where(lane == 1, i2, jnp.where(lane == 2, w1,
                                   jnp.where(lane == 3, e2 * w1, 0.0))))


def _merge(pm, sgm, on, proj, x2, wp, ws, wn, wo, lg, lb, alpha, tm, router=None):
    t = x2.shape[0]
    row = lambda width, cb=0: pl.BlockSpec((tm, width), lambda i, cb=cb: (i, cb))
    full = lambda shape: pl.BlockSpec(shape, lambda i: (0,) * len(shape))
    gate_block = BRANCH_COL // D_MODEL
    in_specs = [row(POOL_WIDTH), row(SG_WIDTH), row(Q_WIDTH),
                row(D_MODEL, gate_block), row(D_MODEL, gate_block + 1), row(D_MODEL, gate_block + 2),
                row(D_MODEL), full(wp.shape), full(ws.shape), full(wn.shape), full(wo.shape),
                full((1, D_MODEL)), full((1, D_MODEL))]
    args = [pm, sgm, on, proj, proj, proj, x2, wp.astype(BF16), ws.astype(BF16), wn.astype(BF16), wo.astype(BF16),
            lg[None], lb[None]]
    out_specs = [row(D_MODEL)]
    out_shape = [jax.ShapeDtypeStruct((t, D_MODEL), F32)]
    if router is not None:
        w_router, b_router = router
        pad = LANE - N_EXPERTS
        in_specs += [full((D_MODEL, LANE)), full((1, LANE))]
        args += [jnp.pad(w_router, ((0, 0), (0, pad))), jnp.pad(b_router, (0, pad))[None]]
        out_specs += [row(LANE), pl.BlockSpec((tm * TOKEN_TILE, LANE), lambda i: (i, 0))]
        out_shape += [jax.ShapeDtypeStruct((t, LANE), F32), jax.ShapeDtypeStruct((t * TOKEN_TILE, LANE), F32)]
    out = pl.pallas_call(
        functools.partial(_merge_kernel, with_router=router is not None, alpha=alpha),
        grid=(t // tm,), in_specs=in_specs, out_specs=out_specs, out_shape=out_shape,
        compiler_params=_params(("parallel",)),
        name="merge",
    )(*args)
    return out if router is not None else out[0]


def _tile_rows(i):
    start = i * TOKEN_TILE if isinstance(i, int) else pl.multiple_of(i * TOKEN_TILE, TOKEN_TILE)
    return pl.ds(start, TOKEN_TILE)


def _for_rows(n, fn):
    def body(r, c):
        fn(r)
        return c
    lax.fori_loop(0, n, body, 0)


def _moe_kernel(be_ref, tok_ref, tok_next_ref, slot_ref, slot_prev_ref, x_hbm, wg_ref, wu_ref, wd_ref, y_hbm,
                xg_ref, xb_ref, acc_ref, yt_ref, gather_sem, scatter_sem):
    i = pl.program_id(0)
    k = pl.program_id(1)
    n_blocks = pl.num_programs(0)
    rows = xb_ref.shape[0]

    def gather(idx_ref, slot):
        return lambda r: pltpu.make_async_copy(x_hbm.at[_tile_rows(idx_ref[0, r])], xg_ref.at[slot, _tile_rows(r)],
                                               gather_sem.at[slot])

    def scatter(idx_ref):
        return lambda r: pltpu.make_async_copy(yt_ref.at[_tile_rows(r)], y_hbm.at[_tile_rows(idx_ref[0, r])],
                                               scatter_sem)

    @pl.when(k == 0)
    def _():
        slot = i % 2

        @pl.when(i == 0)
        def _():
            _for_rows(rows, lambda r: gather(tok_ref, 0)(r).start())

        _for_rows(rows, lambda r: gather(tok_ref, slot)(r).wait())

        @pl.when(i + 1 < n_blocks)
        def _():
            _for_rows(rows, lambda r: gather(tok_next_ref, 1 - slot)(r).start())

        for s in range(TOKEN_TILE):
            xb_ref[:, s * LANE:(s + 1) * LANE] = xg_ref[slot, pl.ds(s, rows, stride=TOKEN_TILE), :].astype(BF16)

    xb = xb_ref[...]
    hidden = (jax.nn.silu(_dot(xb, wg_ref[...])) * _dot(xb, wu_ref[...])).astype(BF16)
    part = _dot(hidden, wd_ref[...])

    @pl.when(k == 0)
    def _():
        acc_ref[...] = part

    @pl.when(k == pl.num_programs(1) - 1)
    def _():
        @pl.when(i > 0)
        def _():
            _for_rows(rows, lambda r: scatter(slot_prev_ref)(r).wait())

        _store_token_tiles(yt_ref, acc_ref[...] + part)
        _for_rows(rows, lambda r: scatter(slot_ref)(r).start())

        @pl.when(i == n_blocks - 1)
        def _():
            _for_rows(rows, lambda r: scatter(slot_ref)(r).wait())


def _moe(x_tiles, row_tok, row_slot, n_slots, block_e, wg, wu, wd):
    rows = EXPERT_BLOCK
    n_blocks = row_tok.shape[0] // rows
    kf = D_FF // FF_TILE
    assert kf == 2
    idx_spec = lambda fn: pl.BlockSpec((None, 1, rows), lambda i, k, be: (fn(i), 0, 0), memory_space=pltpu.SMEM)
    tok3 = row_tok.reshape(n_blocks, 1, rows)
    slot3 = row_slot.reshape(n_blocks, 1, rows)
    return pl.pallas_call(
        _moe_kernel,
        grid_spec=pltpu.PrefetchScalarGridSpec(
            num_scalar_prefetch=1,
            grid=(n_blocks, kf),
            in_specs=[idx_spec(lambda i: i), idx_spec(lambda i: jnp.minimum(i + 1, n_blocks - 1)),
                      idx_spec(lambda i: i), idx_spec(lambda i: jnp.maximum(i - 1, 0)),
                      pl.BlockSpec(memory_space=pl.ANY),
                      pl.BlockSpec((None, D_MODEL, FF_TILE), lambda i, k, be: (be[i], 0, k)),
                      pl.BlockSpec((None, D_MODEL, FF_TILE), lambda i, k, be: (be[i], 0, k)),
                      pl.BlockSpec((None, FF_TILE, D_MODEL), lambda i, k, be: (be[i], k, 0))],
            out_specs=pl.BlockSpec(memory_space=pl.ANY),
            scratch_shapes=[pltpu.VMEM((2, rows * TOKEN_TILE, LANE), F32), pltpu.VMEM((rows, D_MODEL), BF16),
                            pltpu.VMEM((rows, D_MODEL), F32), pltpu.VMEM((rows * TOKEN_TILE, LANE), F32),
                            pltpu.SemaphoreType.DMA((2,)), pltpu.SemaphoreType.DMA(())]),
        out_shape=jax.ShapeDtypeStruct((n_slots * TOKEN_TILE, LANE), F32),
        compiler_params=_params(("arbitrary", "arbitrary")),
        name="moe",
    )(block_e, tok3, tok3, slot3, slot3, x_tiles, wg, wu, wd)


def _swiglu_kernel(x_ref, wg_ref, wu_ref, wd_ref, o_ref, xb_ref):
    k = pl.program_id(1)

    @pl.when(k == 0)
    def _():
        xb_ref[...] = x_ref[...].astype(BF16)

    xb = xb_ref[...]
    hidden = (jax.nn.silu(_dot(xb, wg_ref[...])) * _dot(xb, wu_ref[...])).astype(BF16)
    part = _dot(hidden, wd_ref[...])

    @pl.when(k == 0)
    def _():
        o_ref[...] = part

    @pl.when(k != 0)
    def _():
        o_ref[...] += part


def _swiglu(rows, wg, wu, wd, tm):
    r = rows.shape[0]
    row = pl.BlockSpec((tm, D_MODEL), lambda i, k: (i, 0))
    return pl.pallas_call(
        _swiglu_kernel,
        grid=(r // tm, D_FF // FF_TILE),
        in_specs=[row,
                  pl.BlockSpec((D_MODEL, FF_TILE), lambda i, k: (0, k)),
                  pl.BlockSpec((D_MODEL, FF_TILE), lambda i, k: (0, k)),
                  pl.BlockSpec((FF_TILE, D_MODEL), lambda i, k: (k, 0))],
        out_specs=row,
        out_shape=jax.ShapeDtypeStruct(rows.shape, F32),
        scratch_shapes=[pltpu.VMEM((tm, D_MODEL), BF16)],
        compiler_params=_params(("parallel", "arbitrary")),
        name="swiglu",
    )(rows, wg, wu, wd)


def _ple_ln2_kernel(*refs, n_parts, alpha):
    x_ref, p_ref = refs[0], refs[1]
    parts = refs[2:2 + n_parts]
    rest = refs[2 + n_parts:]
    if n_parts > 1:
        route_ref, rest = rest[0], rest[1:]
    wg_ref, bg_ref, wp_ref, lg_ref, lb_ref, o_ref = rest
    x1 = x_ref[...]
    if n_parts > 1:
        route = route_ref[...]
        f = _load_token_tiles(parts[0]) * route[:, TOP_K:TOP_K + 1]
        for k in range(1, n_parts):
            f = f + _load_token_tiles(parts[k]) * route[:, TOP_K + k:TOP_K + k + 1]
    else:
        f = parts[0][...]
    gate = jax.nn.sigmoid(_dot(x1.astype(BF16), wg_ref[...]) + bg_ref[...])
    ple = gate * _dot(p_ref[...].astype(BF16), wp_ref[...])
    o_ref[...] = _layer_norm_rows(alpha * x1 + f + ple, lg_ref[...], lb_ref[...])


def _ple_ln2(x1, p2, y, route, wg, bg, wp, lg, lb, alpha, tm):
    t = x1.shape[0]
    n_parts = 1 if route is None else TOP_K
    steps = t // tm
    row = lambda width: pl.BlockSpec((tm, width), lambda i: (i, 0))
    full = lambda shape: pl.BlockSpec(shape, lambda i: (0,) * len(shape))
    in_specs = [row(D_MODEL), row(PLE_DIM)]
    if n_parts > 1:
        in_specs += [pl.BlockSpec((tm * TOKEN_TILE, LANE), lambda i, k=k: (k * steps + i, 0)) for k in range(n_parts)]
    else:
        in_specs.append(row(D_MODEL))
    args = [x1, p2] + [y] * n_parts
    if n_parts > 1:
        in_specs.append(row(LANE))
        args.append(route)
    in_specs += [full((D_MODEL, D_MODEL)), full((1, D_MODEL)), full((PLE_DIM, D_MODEL)),
                 full((1, D_MODEL)), full((1, D_MODEL))]
    args += [wg.astype(BF16), bg[None], wp.astype(BF16), lg[None], lb[None]]
    return pl.pallas_call(
        functools.partial(_ple_ln2_kernel, n_parts=n_parts, alpha=alpha),
        grid=(steps,), in_specs=in_specs, out_specs=row(D_MODEL),
        out_shape=jax.ShapeDtypeStruct((t, D_MODEL), F32),
        compiler_params=_params(("parallel",)),
        name="ple_ln2",
    )(*args)


def _route_tables(route, t):
    flat_e = route[:, :TOP_K].astype(jnp.int32).reshape(-1)
    tk = flat_e.shape[0]
    onehot = (flat_e[:, None] == jnp.arange(N_EXPERTS, dtype=jnp.int32)[None, :]).astype(jnp.int32)
    before = jnp.cumsum(onehot, axis=0) - onehot
    rank = jnp.sum(before * onehot, axis=1)
    counts = jnp.sum(onehot, axis=0)
    padded = (counts + EXPERT_BLOCK - 1) // EXPERT_BLOCK * EXPERT_BLOCK
    pend = jnp.cumsum(padded)
    dest = (pend - padded)[flat_e] + rank
    n_blocks = -(-tk // EXPERT_BLOCK) + N_EXPERTS
    n_rows = n_blocks * EXPERT_BLOCK
    pair = jnp.arange(tk, dtype=jnp.int32)
    row_tok = jnp.zeros((n_rows,), jnp.int32).at[dest].set(pair // TOP_K)
    is_pad = jnp.ones((n_rows,), jnp.int32).at[dest].set(0)
    pad_slot = tk + jnp.cumsum(is_pad) - is_pad
    row_slot = pad_slot.astype(jnp.int32).at[dest].set((pair % TOP_K) * t + pair // TOP_K)
    block_start = jnp.arange(n_blocks, dtype=jnp.int32) * EXPERT_BLOCK
    block_e = jnp.sum((pend[None, :] <= block_start[:, None]).astype(jnp.int32), axis=1)
    block_e = jnp.minimum(block_e, N_EXPERTS - 1).astype(jnp.int32)
    return row_tok, row_slot, n_rows, block_e


def kernel(x, p, positions, w_in, pool_w, pool_scale, sg_ln_g, sg_ln_b, sg_w, sg_b, cmp_k_pos, cmp_k_w1, cmp_k_w2, cmp_v_pos, cmp_v_w1, cmp_v_w2, w_pool_out, w_sg_out, w_nsa_out, w_out, ln1_g, ln1_b, ffn_w_gate, ffn_w_up, ffn_w_down, moe_router, moe_router_b, moe_w_gate, moe_w_up, moe_w_down, ple_gate_w, ple_gate_b, ple_proj, ln2_g, ln2_b):
    b, seq, d = x.shape
    depth = w_in.shape[0]
    t = b * seq
    alpha = (2 * depth) ** 0.25
    tm = 512
    tabs = _rope_tables(positions)
    x2 = x.reshape(t, d)
    for i in range(depth):
        proj = _proj(x2, _prep_w_in(w_in[i]), tabs, min(1024, t))
        pm, sgm = _mixers(proj, pool_w[i], pool_scale[i], sg_ln_g[i], sg_ln_b[i], sg_w[i], sg_b[i], seq, tm)
        qT, chunks, ks, vsT, kw, vwT, gates = _nsa_layouts(proj, b, seq)
        pos = jnp.stack([cmp_k_pos[i], cmp_v_pos[i]]).reshape(2, 1, CMP_LEN * HEAD_DIM)
        cmp = _compress(chunks, pos, jnp.stack([cmp_k_w1[i], cmp_v_w1[i]]).astype(BF16),
                        jnp.stack([cmp_k_w2[i], cmp_v_w2[i]]).astype(BF16))
        oT = _nsa(qT, cmp[0], cmp[1].transpose(0, 1, 3, 2), ks, vsT, kw, vwT, gates, seq)
        nq = seq // Q_BLOCK
        on = oT.reshape(b, NSA_KV_HEADS, HEAD_DIM, nq, GROUP, Q_BLOCK).transpose(0, 3, 5, 1, 4, 2).reshape(t, Q_WIDTH)
        j = i // 2
        moe = i % 2 == 1
        merged = _merge(pm, sgm, on, proj, x2, w_pool_out[i], w_sg_out[i], w_nsa_out[i], w_out[i],
                        ln1_g[i], ln1_b[i], alpha, tm, router=(moe_router[j], moe_router_b[j]) if moe else None)
        if moe:
            x1, route, x1_tiles = merged
            row_tok, row_slot, n_slots, block_e = _route_tables(route, t)
            y = _moe(x1_tiles, row_tok, row_slot, n_slots, block_e, moe_w_gate[j].astype(BF16),
                     moe_w_up[j].astype(BF16), moe_w_down[j].astype(BF16))
        else:
            x1, route = merged, None
            y = _swiglu(x1, ffn_w_gate[j].astype(BF16), ffn_w_up[j].astype(BF16), ffn_w_down[j].astype(BF16), tm)
        x2 = _ple_ln2(x1, p[i].reshape(t, PLE_DIM), y, route, ple_gate_w[i], ple_gate_b[i], ple_proj[i],
                      ln2_g[i], ln2_b[i], alpha, tm)
    return x2.reshape(b, seq, d)
```

```python
import functools

import jax
import jax.numpy as jnp
import numpy as np
from jax import lax
from jax.experimental import pallas as pl
from jax.experimental.pallas import tpu as pltpu

F32 = jnp.float32
BF16 = jnp.bfloat16

D_MODEL = 1024
POOL_WIDTH = 512
POOL_WINDOWS = (2, 4, 8, 16)
POOL_HALO = 16
SG_WIDTH = 512
SG_HEADS = 4
SG_CHUNK = 128
NSA_Q_HEADS = 8
NSA_KV_HEADS = 2
GROUP = NSA_Q_HEADS // NSA_KV_HEADS
HEAD_DIM = 64
Q_WIDTH = NSA_Q_HEADS * HEAD_DIM
KV_WIDTH = NSA_KV_HEADS * HEAD_DIM
ROT_DIM = HEAD_DIM // 4
ROPE_THETA = 500000.0
CMP_LEN = 32
CMP_STRIDE = 16
CMP_HIDDEN = 256
SLC_LEN = 64
SLC_TOP = 16
WINDOW = 512
Q_BLOCK = 128
NEG_INF = -1e30
FORCE_SCORE = 1e4
D_FF = 2816
N_EXPERTS = 8
TOP_K = 2
EXPERT_BLOCK = 512
PLE_DIM = 256
LN_EPS = 1e-5

LANE = 128
TOKEN_TILE = D_MODEL // LANE
PROJ_BLOCK = 512
GATE_COL = 2816
BRANCH_COL = 3072
PROJ_COLS = BRANCH_COL + 3 * D_MODEL
NSA_STRIP = 64
SLC_BLOCK = 256
SLC_UNROLL = 4
CMP_TILE = 256
SEL_GROUP = 8
ONES_PAD = 16
PSUM_PAD = 8
LOG2_E = 1.4426950408889634
FF_TILE = 1408
VMEM_LIMIT = 56 * 1024 * 1024


def _params(semantics, vmem=VMEM_LIMIT):
    return pltpu.CompilerParams(dimension_semantics=semantics, vmem_limit_bytes=vmem)


def _dot(a, b):
    return jnp.dot(a, b, preferred_element_type=F32)


def _store_token_tiles(ref, v):
    n = v.shape[0]
    for s in range(TOKEN_TILE):
        ref[pl.ds(s, n, stride=TOKEN_TILE), :] = v[:, s * LANE:(s + 1) * LANE]


def _load_token_tiles(ref):
    n = ref.shape[0] // TOKEN_TILE
    return jnp.concatenate([ref[pl.ds(s, n, stride=TOKEN_TILE), :] for s in range(TOKEN_TILE)], axis=1)


def _layer_norm_rows(v, g, b):
    mu = jnp.mean(v, axis=-1, keepdims=True)
    c = v - mu
    var = jnp.mean(c * c, axis=-1, keepdims=True)
    return c * lax.rsqrt(var + LN_EPS) * g + b


def _rope_tables(positions):
    half = ROT_DIM // 2
    inv = ROPE_THETA ** (-jnp.arange(half, dtype=F32) / half)
    ang = positions.astype(F32)[:, :, None] * inv
    cos, sin = jnp.cos(ang), jnp.sin(ang)
    rest = HEAD_DIM - ROT_DIM
    one = jnp.ones(ang.shape[:2] + (rest,), F32)
    zero_r = jnp.zeros(ang.shape[:2] + (rest,), F32)
    zero_h = jnp.zeros_like(sin)
    c = jnp.concatenate([cos, cos, one], -1)
    s1 = jnp.concatenate([zero_h, sin, zero_r], -1)
    s2 = jnp.concatenate([-sin, zero_h, zero_r], -1)
    reps = LANE // HEAD_DIM
    return tuple(jnp.tile(t, (1, 1, reps)).reshape(-1, LANE) for t in (c, s1, s2))


def _proj_kernel(x_ref, w_ref, c_ref, s1_ref, s2_ref, o_ref):
    half = ROT_DIM // 2
    xb = x_ref[...].astype(BF16)

    def rope(v):
        return (v * c_ref[...] + pltpu.roll(v, half, 1) * s1_ref[...]
                + pltpu.roll(v, LANE - half, 1) * s2_ref[...])

    q_first = POOL_WIDTH + 2 * SG_WIDTH
    kv_first = q_first + Q_WIDTH
    q_scale = HEAD_DIM ** -0.5 * LOG2_E
    for col in range(0, PROJ_COLS, PROJ_BLOCK):
        acc = _dot(xb, w_ref[:, col:col + PROJ_BLOCK])
        for sub in range(0, PROJ_BLOCK, LANE):
            lanes = slice(col + sub, col + sub + LANE)
            v = acc[:, sub:sub + LANE]
            if q_first <= col + sub < kv_first:
                v = rope(v) * q_scale
            elif kv_first <= col + sub < kv_first + 6 * KV_WIDTH and ((col + sub - kv_first) // KV_WIDTH) % 2 == 0:
                v = rope(v)
            o_ref[:, lanes] = v.astype(o_ref.dtype)


def _proj(x2, w, tabs, tm):
    t = x2.shape[0]
    row = lambda i: (i, 0)
    return pl.pallas_call(
        _proj_kernel,
        grid=(t // tm,),
        in_specs=[pl.BlockSpec((tm, D_MODEL), row),
                  pl.BlockSpec((D_MODEL, PROJ_COLS), lambda i: (0, 0), pipeline_mode=pl.Buffered(1)),
                  pl.BlockSpec((tm, LANE), row), pl.BlockSpec((tm, LANE), row), pl.BlockSpec((tm, LANE), row)],
        out_specs=pl.BlockSpec((tm, PROJ_COLS), row),
        out_shape=jax.ShapeDtypeStruct((t, PROJ_COLS), BF16),
        compiler_params=_params(("parallel",)),
        name="proj",
    )(x2, w, *tabs)


def _prep_w_in(w):
    n_gate = NSA_Q_HEADS * 3
    main, gate, branch = w[:, :GATE_COL], w[:, GATE_COL:GATE_COL + n_gate], w[:, GATE_COL + n_gate:]
    pad = jnp.zeros((w.shape[0], BRANCH_COL - GATE_COL - n_gate), w.dtype)
    return jnp.concatenate([main, gate, pad, branch], axis=1).astype(BF16)


def _mixers_kernel(a_ref, ap_ref, u_ref, v_ref, pw_ref, ps_ref, lg_ref, lb_ref, sw_ref, sb_ref,
                   pm_ref, sg_ref, *, tiles_per_seq):
    i = pl.program_id(0)
    tm = a_ref.shape[0]
    it = i % tiles_per_seq
    gw = POOL_WIDTH // len(POOL_WINDOWS)

    a = a_ref[...].astype(F32)
    prev = jnp.where(it == 0, 0.0, ap_ref[...].astype(F32))
    ext = jnp.concatenate([prev, a], axis=0)
    tpos = (it * tm + lax.broadcasted_iota(jnp.int32, (tm, 1), 0) + 1).astype(F32)
    for g, w in enumerate(POOL_WINDOWS):
        s = ext[:, g * gw:(g + 1) * gw]
        k = 1
        while k < w:
            s = s + pltpu.roll(s, k, 0)
            k *= 2
        d = s[POOL_HALO:] / jnp.minimum(tpos, float(w)) - a[:, g * gw:(g + 1) * gw]
        y = _dot(d.astype(BF16), pw_ref[g]) * ps_ref[:, g * gw:(g + 1) * gw]
        pm_ref[:, g * gw:(g + 1) * gw] = y.astype(pm_ref.dtype)

    u = jax.nn.gelu(u_ref[...].astype(F32))
    v = _layer_norm_rows(jax.nn.gelu(v_ref[...].astype(F32)), lg_ref[...], lb_ref[...]).astype(BF16)
    hc = SG_WIDTH // SG_HEADS
    tri = (lax.broadcasted_iota(jnp.int32, (SG_CHUNK, SG_CHUNK), 0)
           >= lax.broadcasted_iota(jnp.int32, (SG_CHUNK, SG_CHUNK), 1))
    for g in range(SG_HEADS):
        ws = jnp.where(tri, sw_ref[g], 0.0).astype(BF16)
        bias = sb_ref[:, g:g + 1]
        for c in range(tm // SG_CHUNK):
            rows = slice(c * SG_CHUNK, (c + 1) * SG_CHUNK)
            cols = slice(g * hc, (g + 1) * hc)
            mixed = _dot(ws, v[rows, cols]) + bias
            sg_ref[rows, cols] = (u[rows, cols] * mixed).astype(sg_ref.dtype)


def _mixers(proj, pool_w, pool_scale, ln_g, ln_b, sg_w, sg_b, seq, tm):
    t = proj.shape[0]
    halo_blocks = tm // POOL_HALO
    full = lambda shape: pl.BlockSpec(shape, lambda i: (0,) * len(shape))
    return pl.pallas_call(
        functools.partial(_mixers_kernel, tiles_per_seq=seq // tm),
        grid=(t // tm,),
        in_specs=[pl.BlockSpec((tm, POOL_WIDTH), lambda i: (i, 0)),
                  pl.BlockSpec((POOL_HALO, POOL_WIDTH), lambda i: (jnp.maximum(i * halo_blocks - 1, 0), 0)),
                  pl.BlockSpec((tm, SG_WIDTH), lambda i: (i, 1)),
                  pl.BlockSpec((tm, SG_WIDTH), lambda i: (i, 2)),
                  full(pool_w.shape), full((1, POOL_WIDTH)), full((1, SG_WIDTH)), full((1, SG_WIDTH)),
                  full(sg_w.shape), full((SG_CHUNK, SG_HEADS))],
        out_specs=[pl.BlockSpec((tm, POOL_WIDTH), lambda i: (i, 0)),
                   pl.BlockSpec((tm, SG_WIDTH), lambda i: (i, 0))],
        out_shape=[jax.ShapeDtypeStruct((t, POOL_WIDTH), BF16), jax.ShapeDtypeStruct((t, SG_WIDTH), BF16)],
        compiler_params=_params(("parallel",)),
        name="mixers",
    )(proj, proj, proj, proj, pool_w.astype(BF16), pool_scale[None], ln_g[None], ln_b[None], sg_w, sg_b.T)


def _compress_kernel(ch_ref, pos_ref, w1_ref, w2_ref, o_ref):
    nc, half = ch_ref.shape
    ch = ch_ref[...].astype(F32)
    pos = pos_ref[...]
    first = _dot((ch + pos[:, :half]).astype(BF16), w1_ref[:half, :])
    second = _dot((ch + pos[:, half:]).astype(BF16), w1_ref[half:, :])
    pre = first + pltpu.roll(second, nc - 1, 0)
    o_ref[...] = _dot(jax.nn.gelu(pre).astype(BF16), w2_ref[...]).astype(o_ref.dtype)


def _compress(chunks, pos, w1, w2):
    _, b, h, nc, width = chunks.shape
    sel = lambda s, bi, hi: (s, 0, 0)
    return pl.pallas_call(
        _compress_kernel,
        grid=(2, b, h),
        in_specs=[pl.BlockSpec((None, None, None, nc, width), lambda s, bi, hi: (s, bi, hi, 0, 0)),
                  pl.BlockSpec((None, 1, 2 * width), sel),
                  pl.BlockSpec((None, 2 * width, CMP_HIDDEN), sel),
                  pl.BlockSpec((None, CMP_HIDDEN, HEAD_DIM), sel)],
        out_specs=pl.BlockSpec((None, None, None, nc, HEAD_DIM), lambda s, bi, hi: (s, bi, hi, 0, 0)),
        out_shape=jax.ShapeDtypeStruct((2, b, h, nc, HEAD_DIM), BF16),
        compiler_params=_params(("parallel", "parallel", "parallel")),
        name="compress",
    )(chunks, pos, w1, w2)


def _score_strips(k_ref, k0, n_strips, q_aug, s_ref, mask):
    mx = None
    for r in range(n_strips):
        start = pl.multiple_of(k0 + r * NSA_STRIP, NSA_STRIP)
        s = _dot(k_ref[pl.ds(start, NSA_STRIP), :], q_aug)
        if mask is not None:
            s = mask(r, start, s)
        s_ref[r * NSA_STRIP:(r + 1) * NSA_STRIP, :] = s
        mx = s if mx is None else jnp.maximum(mx, s)
    return jnp.max(mx, axis=0, keepdims=True)


def _prob_strips(n_strips, m, s_ref, p_ref):
    for r in range(n_strips):
        rows = slice(r * NSA_STRIP, (r + 1) * NSA_STRIP)
        p_ref[rows, :] = jnp.exp2(s_ref[rows, :] - m).astype(BF16)


def _nsa_kernel(qT_ref, kc_ref, vcT_ref, ks_ref, vsT_ref, kw_ref, vwT_ref, g_ref, bound_ref, o_ref,
                s_ref, p_ref, pc_ref, s2_ref, p2_ref, psum_ref, selneg_ref, *, n_top):
    n = pl.program_id(2)
    qs = n * Q_BLOCK
    width = GROUP * Q_BLOCK
    qT = qT_ref[...]
    t_row = qs + lax.broadcasted_iota(jnp.int32, (1, width), 1) % Q_BLOCK
    nc = kc_ref.shape[0]
    ns = selneg_ref.shape[0]
    strip_iota = lax.broadcasted_iota(jnp.int32, (NSA_STRIP, 1), 0)

    @pl.when(n == 0)
    def _():
        psum_ref[...] = jnp.zeros(psum_ref.shape, F32)
        pc_ref[...] = jnp.zeros(pc_ref.shape, BF16)

    n_ct = (n * (Q_BLOCK // CMP_STRIDE) + (Q_BLOCK - CMP_LEN) // CMP_STRIDE + 1 + CMP_TILE - 1) // CMP_TILE
    strips_per_tile = CMP_TILE // NSA_STRIP

    def cmp_scores(jt, mx):
        for r in range(strips_per_tile):
            start = pl.multiple_of(jt * CMP_TILE + r * NSA_STRIP, NSA_STRIP)
            s = _dot(kc_ref[pl.ds(start, NSA_STRIP), :], qT)
            c_end = (start + strip_iota) * CMP_STRIDE + (CMP_LEN - 1)
            s = jnp.where(c_end <= t_row, s, NEG_INF)
            s_ref[pl.ds(start, NSA_STRIP), :] = s
            mx = jnp.maximum(mx, s)
        return mx

    mx = lax.fori_loop(0, n_ct, cmp_scores, jnp.full((NSA_STRIP, width), NEG_INF, F32))
    m_cmp = jnp.maximum(jnp.max(mx, axis=0, keepdims=True), 0.1 * NEG_INF)

    def cmp_probs(jt, lsum):
        for r in range(strips_per_tile):
            rows = pl.ds(pl.multiple_of(jt * CMP_TILE + r * NSA_STRIP, NSA_STRIP), NSA_STRIP)
            e = jnp.exp2(s_ref[rows, :] - m_cmp)
            s_ref[rows, :] = e
            pc_ref[rows, :] = e.astype(BF16)
            lsum = lsum + e
        return lsum

    lsum = lax.fori_loop(0, n_ct, cmp_probs, jnp.zeros((NSA_STRIP, width), F32))
    l = jnp.sum(lsum, axis=0, keepdims=True)
    inv = jnp.where(l > 0.0, 1.0 / l, 0.0)
    o_cmp = _dot(vcT_ref[...], pc_ref[...]) * inv

    def cmp_group_sum(jt, c):
        for r in range(strips_per_tile):
            start = pl.multiple_of(jt * CMP_TILE + r * NSA_STRIP, NSA_STRIP)
            p = s_ref[pl.ds(start, NSA_STRIP), :] * inv
            psum = p[:, :Q_BLOCK]
            for g in range(1, GROUP):
                psum = psum + p[:, g * Q_BLOCK:(g + 1) * Q_BLOCK]
            psum_ref[pl.ds(pl.multiple_of(PSUM_PAD + start, 8), NSA_STRIP), :] = psum
        return c

    lax.fori_loop(0, n_ct, cmp_group_sum, 0)

    per = SLC_LEN // CMP_STRIDE
    part = [psum_ref[pl.ds(PSUM_PAD + k, ns, stride=per), :] for k in range(per)]
    before = psum_ref[pl.ds(PSUM_PAD - 1, ns, stride=per), :]
    imp = 2.0 * (part[0] + part[1] + part[2]) + part[3] + before

    blk = lax.broadcasted_iota(jnp.int32, (ns, Q_BLOCK), 0)
    cur = (qs + lax.broadcasted_iota(jnp.int32, (ns, Q_BLOCK), 1)) // SLC_LEN
    taken = -2.0
    score = jnp.where(blk <= cur, imp, -1.0)
    score = jnp.where(blk == 0, taken, jnp.where(blk == cur, taken, jnp.where(blk == cur - 1, taken, score)))
    blk_f = blk.astype(F32)
    for _ in range(n_top - 3):
        top = jnp.max(score, axis=0, keepdims=True)
        cand = jnp.where(score == top, blk_f, float(ns))
        first = jnp.min(cand, axis=0, keepdims=True)
        score = jnp.where(cand == first, taken, score)
    selneg_ref[...] = jnp.where(score == taken, 0.0, NEG_INF)

    strips = SLC_BLOCK // NSA_STRIP
    sel_rows = ks_ref.shape[1] - HEAD_DIM - SEL_GROUP

    last_block = ks_ref.shape[0] // SLC_BLOCK - 1

    def past_only(r, start, s):
        return jnp.where(start + strip_iota <= t_row, s, NEG_INF)

    n_diag = n // (SLC_BLOCK // Q_BLOCK)

    def slc_scores(kb, slot):
        variant = jnp.where(kb < n_diag, 0, jnp.where(kb > n_diag, 3, 1 + n % (SLC_BLOCK // Q_BLOCK)))

        def limit(r, start, s):
            return jnp.minimum(s, bound_ref[variant, r * NSA_STRIP:(r + 1) * NSA_STRIP, :])

        kb = jnp.minimum(kb, last_block)
        k0 = pl.multiple_of(kb * SLC_BLOCK, SLC_BLOCK)
        g0 = pl.multiple_of((kb * strips // SEL_GROUP) * SEL_GROUP, SEL_GROUP)
        bias = jnp.concatenate([selneg_ref[pl.ds(g0, SEL_GROUP), :]] * GROUP, axis=1)
        bias = jnp.concatenate([bias, jnp.zeros((sel_rows, width), F32)], axis=0).astype(BF16)
        q_aug = jnp.concatenate([qT, bias], axis=0)
        return _score_strips(ks_ref, k0, strips, q_aug, s2_ref.at[slot], limit)

    def slc_consume(kb, slot, m, acc, mx):
        k0 = pl.multiple_of(kb * SLC_BLOCK, SLC_BLOCK)
        m_new = jnp.maximum(m, mx)
        _prob_strips(strips, m_new, s2_ref.at[slot], p2_ref.at[slot])
        pv = _dot(vsT_ref[:, pl.ds(k0, SLC_BLOCK)], p2_ref[slot])
        return m_new, jnp.exp2(m - m_new) * acc + pv

    def slc_group(j, carry):
        m, acc, mx_next = carry
        for u in range(SLC_UNROLL):
            kb = SLC_UNROLL * j + u
            mx_cur = mx_next
            mx_next = slc_scores(kb + 1, (u + 1) % 2)
            m, acc = slc_consume(kb, u % 2, m, acc, mx_cur)
        return m, acc, mx_next

    init = (jnp.full((1, width), NEG_INF, F32), jnp.zeros((vsT_ref.shape[0], width), F32), slc_scores(0, 0))
    _, acc, _ = lax.fori_loop(0, (n_diag + SLC_UNROLL) // SLC_UNROLL, slc_group, init)
    o_slc = acc[:HEAD_DIM] / acc[HEAD_DIM:HEAD_DIM + 1]

    span = WINDOW + Q_BLOCK
    w_strips = span // NSA_STRIP
    edge = Q_BLOCK // NSA_STRIP
    w0 = pl.multiple_of(jnp.maximum(qs - WINDOW, 0), Q_BLOCK)

    def band_edges(r, start, s):
        if r < edge:
            return jnp.where(t_row - (start + strip_iota) < WINDOW, s, NEG_INF)
        if r >= w_strips - edge:
            return past_only(r, start, s)
        return s

    m_win = lax.cond(n >= WINDOW // Q_BLOCK,
                     lambda: _score_strips(kw_ref, w0, w_strips, qT, s_ref, band_edges),
                     lambda: _score_strips(kw_ref, w0, w_strips, qT, s_ref, past_only))
    _prob_strips(w_strips, m_win, s_ref, p_ref)
    pv = _dot(vwT_ref[:, pl.ds(w0, span)], p_ref[:span, :])
    o_win = pv[:HEAD_DIM] / pv[HEAD_DIM:HEAD_DIM + 1]

    gates = jax.nn.sigmoid(g_ref[...].astype(F32))

    def gate(c):
        return jnp.concatenate([gates[g * 3 + c:g * 3 + c + 1, :] for g in range(GROUP)], axis=1)

    o_ref[...] = (gate(0) * o_cmp + gate(1) * o_slc + gate(2) * o_win).astype(o_ref.dtype)


def _causal_bounds():
    big = np.float32(3.0e38)
    key = np.arange(SLC_BLOCK)[:, None]
    query = np.tile(np.arange(Q_BLOCK), GROUP)[None, :]
    tables = [np.full((SLC_BLOCK, GROUP * Q_BLOCK), big, np.float32)]
    for v in range(SLC_BLOCK // Q_BLOCK):
        tables.append(np.where(key <= query + v * Q_BLOCK, big, np.float32(NEG_INF)).astype(np.float32))
    tables.append(np.full((SLC_BLOCK, GROUP * Q_BLOCK), NEG_INF, np.float32))
    return jnp.asarray(np.stack(tables))


def _nsa(qT, kc, vcT, ks, vsT, kw, vwT, gates, seq):
    b, h = qT.shape[:2]
    nq = seq // Q_BLOCK
    nc = kc.shape[2]
    ns = seq // SLC_LEN
    width = GROUP * Q_BLOCK
    assert nc == ns * (SLC_LEN // CMP_STRIDE) and nc % CMP_TILE == 0 and seq % (SLC_BLOCK * SLC_UNROLL) == 0
    bounds = _causal_bounds()
    assert seq >= WINDOW + Q_BLOCK and ns >= SLC_TOP
    per_head = lambda shape: pl.BlockSpec((None, None) + shape, lambda bi, hi, n: (bi, hi, 0, 0))
    return pl.pallas_call(
        functools.partial(_nsa_kernel, n_top=SLC_TOP),
        grid=(b, h, nq),
        in_specs=[pl.BlockSpec((None, None, HEAD_DIM, width), lambda bi, hi, n: (bi, hi, 0, n)),
                  per_head((nc, HEAD_DIM)), per_head((HEAD_DIM, nc)),
                  per_head(ks.shape[2:]), per_head(vsT.shape[2:]),
                  per_head(kw.shape[2:]), per_head(vwT.shape[2:]),
                  pl.BlockSpec((None, None, 16, Q_BLOCK), lambda bi, hi, n: (bi, hi, 0, n)),
                  pl.BlockSpec(bounds.shape, lambda bi, hi, n: (0, 0, 0))],
        out_specs=pl.BlockSpec((None, None, HEAD_DIM, width), lambda bi, hi, n: (bi, hi, 0, n)),
        out_shape=jax.ShapeDtypeStruct((b, h, HEAD_DIM, nq * width), BF16),
        scratch_shapes=[pltpu.VMEM((max(nc, WINDOW + Q_BLOCK), width), F32),
                        pltpu.VMEM((WINDOW + Q_BLOCK, width), BF16),
                        pltpu.VMEM((nc, width), BF16),
                        pltpu.VMEM((2, SLC_BLOCK, width), F32),
                        pltpu.VMEM((2, SLC_BLOCK, width), BF16),
                        pltpu.VMEM((PSUM_PAD + nc, Q_BLOCK), F32),
                        pltpu.VMEM((ns, Q_BLOCK), F32)],
        compiler_params=_params(("parallel", "parallel", "arbitrary")),
        name="nsa",
    )(qT, kc, vcT, ks, vsT, kw, vwT, gates, bounds)


def _nsa_layouts(proj, b, seq):
    h = NSA_KV_HEADS
    nq = seq // Q_BLOCK
    col = POOL_WIDTH + 2 * SG_WIDTH
    q = proj[:, col:col + Q_WIDTH].reshape(b, nq, Q_BLOCK, h, GROUP, HEAD_DIM)
    qT = q.transpose(0, 3, 5, 1, 4, 2).reshape(b, h, HEAD_DIM, nq * GROUP * Q_BLOCK)
    col += Q_WIDTH
    heads = [proj[:, col + k * KV_WIDTH:col + (k + 1) * KV_WIDTH].reshape(b, seq, h, HEAD_DIM).transpose(0, 2, 1, 3)
             for k in range(6)]
    kc, vc, ks, vs, kw, vw = heads
    chunk = lambda t: t.reshape(b, h, seq // CMP_STRIDE, CMP_STRIDE * HEAD_DIM)
    chunks = jnp.stack([chunk(kc), chunk(vc)])
    gates = proj[:, GATE_COL:GATE_COL + NSA_Q_HEADS * 3].reshape(b, seq, h, GROUP * 3).transpose(0, 2, 3, 1)
    gates = jnp.pad(gates, ((0, 0), (0, 0), (0, 16 - GROUP * 3), (0, 0)))
    sel = (jnp.arange(seq, dtype=jnp.int32) // SLC_LEN) % SEL_GROUP
    onehot = (sel[:, None] == jnp.arange(LANE - HEAD_DIM, dtype=jnp.int32)[None, :]).astype(BF16)
    ks = jnp.concatenate([ks, jnp.broadcast_to(onehot, (b, h, seq, LANE - HEAD_DIM))], axis=-1)

    def values_t(v):
        ones = jnp.ones((b, h, 1, seq), BF16)
        return jnp.concatenate([v.transpose(0, 1, 3, 2), ones, jnp.zeros((b, h, ONES_PAD - 1, seq), BF16)], axis=2)

    return qT, chunks, ks, values_t(vs), kw, values_t(vw), gates


def _merge_kernel(*refs, with_router, alpha):
    (pm_ref, sg_ref, on_ref, gp_ref, gs_ref, gn_ref, x_ref, wp_ref, ws_ref, wn_ref, wo_ref, lg_ref, lb_ref) = refs[:13]
    if with_router:
        wr_ref, br_ref, x1_ref, route_ref, x1t_ref = refs[13:]
    else:
        (x1_ref,) = refs[13:]
    sig = lambda r: jax.nn.sigmoid(r[...].astype(F32))
    y = (sig(gp_ref) * _dot(pm_ref[...], wp_ref[...]) + sig(gs_ref) * _dot(sg_ref[...], ws_ref[...])
         + sig(gn_ref) * _dot(on_ref[...], wn_ref[...]))
    hmix = _dot(y.astype(BF16), wo_ref[...])
    x1 = _layer_norm_rows(alpha * x_ref[...] + hmix, lg_ref[...], lb_ref[...])
    x1_ref[...] = x1
    if with_router:
        _store_token_tiles(x1t_ref, x1)
        wr = wr_ref[...]
        w_hi = wr.astype(BF16)
        w_lo = (wr - w_hi.astype(F32)).astype(BF16)
        x_hi = x1.astype(BF16)
        x_lo = (x1 - x_hi.astype(F32)).astype(BF16)
        logits = _dot(x_hi, w_hi) + (_dot(x_lo, w_hi) + _dot(x_hi, w_lo)) + br_ref[...]
        lane = lax.broadcasted_iota(jnp.int32, logits.shape, 1)
        lane_f = lane.astype(F32)
        logits = jnp.where(lane < N_EXPERTS, logits, -jnp.inf)
        m1 = jnp.max(logits, axis=-1, keepdims=True)
        i1 = jnp.min(jnp.where(logits == m1, lane_f, float(LANE)), axis=-1, keepdims=True)
        rest = jnp.where(lane_f == i1, -jnp.inf, logits)
        m2 = jnp.max(rest, axis=-1, keepdims=True)
        i2 = jnp.min(jnp.where(rest == m2, lane_f, float(LANE)), axis=-1, keepdims=True)
        e2 = jnp.exp(m2 - m1)
        w1 = 1.0 / (1.0 + e2)
        route_ref[...] = jnp.where(lane == 0, i1, jnp.where(lane == 1, i2, jnp.where(lane == 2, w1,
                                   jnp.where(lane == 3, e2 * w1, 0.0))))


def _merge(pm, sgm, on, proj, x2, wp, ws, wn, wo, lg, lb, alpha, tm, router=None):
    t = x2.shape[0]
    row = lambda width, cb=0: pl.BlockSpec((tm, width), lambda i, cb=cb: (i, cb))
    full = lambda shape: pl.BlockSpec(shape, lambda i: (0,) * len(shape))
    gate_block = BRANCH_COL // D_MODEL
    in_specs = [row(POOL_WIDTH), row(SG_WIDTH), row(Q_WIDTH),
                row(D_MODEL, gate_block), row(D_MODEL, gate_block + 1), row(D_MODEL, gate_block + 2),
                row(D_MODEL), full(wp.shape), full(ws.shape), full(wn.shape), full(wo.shape),
                full((1, D_MODEL)), full((1, D_MODEL))]
    args = [pm, sgm, on, proj, proj, proj, x2, wp.astype(BF16), ws.astype(BF16), wn.astype(BF16), wo.astype(BF16),
            lg[None], lb[None]]
    out_specs = [row(D_MODEL)]
    out_shape = [jax.ShapeDtypeStruct((t, D_MODEL), F32)]
    if router is not None:
        w_router, b_router = router
        pad = LANE - N_EXPERTS
        in_specs += [full((D_MODEL, LANE)), full((1, LANE))]
        args += [jnp.pad(w_router, ((0, 0), (0, pad))), jnp.pad(b_router, (0, pad))[None]]
        out_specs += [row(LANE), pl.BlockSpec((tm * TOKEN_TILE, LANE), lambda i: (i, 0))]
        out_shape += [jax.ShapeDtypeStruct((t, LANE), F32), jax.ShapeDtypeStruct((t * TOKEN_TILE, LANE), F32)]
    out = pl.pallas_call(
        functools.partial(_merge_kernel, with_router=router is not None, alpha=alpha),
        grid=(t // tm,), in_specs=in_specs, out_specs=out_specs, out_shape=out_shape,
        compiler_params=_params(("parallel",)),
        name="merge",
    )(*args)
    return out if router is not None else out[0]


def _tile_rows(i):
    start = i * TOKEN_TILE if isinstance(i, int) else pl.multiple_of(i * TOKEN_TILE, TOKEN_TILE)
    return pl.ds(start, TOKEN_TILE)


def _for_rows(n, fn):
    def body(r, c):
        fn(r)
        return c
    lax.fori_loop(0, n, body, 0, unroll=8)


def _moe_kernel(be_ref, tok_ref, tok_next_ref, slot_ref, x_hbm, wg_ref, wu_ref, wd_ref, y_hbm,
                xg_ref, xb_ref, acc_ref, yt_ref, gather_sem, scatter_sem):
    i = pl.program_id(0)
    k = pl.program_id(1)
    n_blocks = pl.num_programs(0)
    rows = xb_ref.shape[0]

    def gather(idx_ref, slot):
        return lambda r: pltpu.make_async_copy(x_hbm.at[_tile_rows(idx_ref[0, r])], xg_ref.at[slot, _tile_rows(r)],
                                               gather_sem.at[slot])

    def scatter(idx_ref):
        return lambda r: pltpu.make_async_copy(yt_ref.at[_tile_rows(r)], y_hbm.at[_tile_rows(idx_ref[0, r])],
                                               scatter_sem)

    @pl.when(k == 0)
    def _():
        slot = i % 2

        @pl.when(i == 0)
        def _():
            _for_rows(rows, lambda r: gather(tok_ref, 0)(r).start())

        pltpu.make_async_copy(x_hbm.at[pl.ds(0, rows * TOKEN_TILE)], xg_ref.at[slot], gather_sem.at[slot]).wait()

        @pl.when(i + 1 < n_blocks)
        def _():
            _for_rows(rows, lambda r: gather(tok_next_ref, 1 - slot)(r).start())

        for s in range(TOKEN_TILE):
            xb_ref[:, s * LANE:(s + 1) * LANE] = xg_ref[slot, pl.ds(s, rows, stride=TOKEN_TILE), :].astype(BF16)

    xb = xb_ref[...]
    hidden = (jax.nn.silu(_dot(xb, wg_ref[...])) * _dot(xb, wu_ref[...])).astype(BF16)
    part = _dot(hidden, wd_ref[...])

    @pl.when(k == 0)
    def _():
        acc_ref[...] = part

    @pl.when(k == pl.num_programs(1) - 1)
    def _():
        def wait_scatter():
            pltpu.make_async_copy(yt_ref, y_hbm.at[pl.ds(0, rows * TOKEN_TILE)], scatter_sem).wait()

        pl.when(i > 0)(wait_scatter)
        _store_token_tiles(yt_ref, acc_ref[...] + part)
        _for_rows(rows, lambda r: scatter(slot_ref)(r).start())
        pl.when(i == n_blocks - 1)(wait_scatter)


def _moe(x_tiles, row_tok, row_slot, n_slots, block_e, wg, wu, wd):
    rows = EXPERT_BLOCK
    n_blocks = row_tok.shape[0] // rows
    kf = D_FF // FF_TILE
    assert kf == 2
    idx_spec = lambda fn: pl.BlockSpec((None, 1, rows), lambda i, k, be: (fn(i), 0, 0), memory_space=pltpu.SMEM)
    tok3 = row_tok.reshape(n_blocks, 1, rows)
    slot3 = row_slot.reshape(n_blocks, 1, rows)
    return pl.pallas_call(
        _moe_kernel,
        grid_spec=pltpu.PrefetchScalarGridSpec(
            num_scalar_prefetch=1,
            grid=(n_blocks, kf),
            in_specs=[idx_spec(lambda i: i), idx_spec(lambda i: jnp.minimum(i + 1, n_blocks - 1)),
                      idx_spec(lambda i: i),
                      pl.BlockSpec(memory_space=pl.ANY),
                      pl.BlockSpec((None, D_MODEL, FF_TILE), lambda i, k, be: (be[i], 0, k)),
                      pl.BlockSpec((None, D_MODEL, FF_TILE), lambda i, k, be: (be[i], 0, k)),
                      pl.BlockSpec((None, FF_TILE, D_MODEL), lambda i, k, be: (be[i], k, 0))],
            out_specs=pl.BlockSpec(memory_space=pl.ANY),
            scratch_shapes=[pltpu.VMEM((2, rows * TOKEN_TILE, LANE), F32), pltpu.VMEM((rows, D_MODEL), BF16),
                            pltpu.VMEM((rows, D_MODEL), F32), pltpu.VMEM((rows * TOKEN_TILE, LANE), F32),
                            pltpu.SemaphoreType.DMA((2,)), pltpu.SemaphoreType.DMA(())]),
        out_shape=jax.ShapeDtypeStruct((n_slots * TOKEN_TILE, LANE), F32),
        compiler_params=_params(("arbitrary", "arbitrary")),
        name="moe",
    )(block_e, tok3, tok3, slot3, x_tiles, wg, wu, wd)


def _swiglu_kernel(x_ref, wg_ref, wu_ref, wd_ref, o_ref, xb_ref):
    k = pl.program_id(1)

    @pl.when(k == 0)
    def _():
        xb_ref[...] = x_ref[...].astype(BF16)

    xb = xb_ref[...]
    hidden = (jax.nn.silu(_dot(xb, wg_ref[...])) * _dot(xb, wu_ref[...])).astype(BF16)
    part = _dot(hidden, wd_ref[...])

    @pl.when(k == 0)
    def _():
        o_ref[...] = part

    @pl.when(k != 0)
    def _():
        o_ref[...] += part


def _swiglu(rows, wg, wu, wd, tm):
    r = rows.shape[0]
    row = pl.BlockSpec((tm, D_MODEL), lambda i, k: (i, 0))
    return pl.pallas_call(
        _swiglu_kernel,
        grid=(r // tm, D_FF // FF_TILE),
        in_specs=[row,
                  pl.BlockSpec((D_MODEL, FF_TILE), lambda i, k: (0, k)),
                  pl.BlockSpec((D_MODEL, FF_TILE), lambda i, k: (0, k)),
                  pl.BlockSpec((FF_TILE, D_MODEL), lambda i, k: (k, 0))],
        out_specs=row,
        out_shape=jax.ShapeDtypeStruct(rows.shape, F32),
        scratch_shapes=[pltpu.VMEM((tm, D_MODEL), BF16)],
        compiler_params=_params(("parallel", "arbitrary")),
        name="swiglu",
    )(rows, wg, wu, wd)


def _ple_ln2_kernel(*refs, n_parts, alpha):
    x_ref, p_ref = refs[0], refs[1]
    parts = refs[2:2 + n_parts]
    rest = refs[2 + n_parts:]
    if n_parts > 1:
        route_ref, rest = rest[0], rest[1:]
    wg_ref, bg_ref, wp_ref, lg_ref, lb_ref, o_ref = rest
    x1 = x_ref[...]
    if n_parts > 1:
        route = route_ref[...]
        f = _load_token_tiles(parts[0]) * route[:, TOP_K:TOP_K + 1]
        for k in range(1, n_parts):
            f = f + _load_token_tiles(parts[k]) * route[:, TOP_K + k:TOP_K + k + 1]
    else:
        f = parts[0][...]
    gate = jax.nn.sigmoid(_dot(x1.astype(BF16), wg_ref[...]) + bg_ref[...])
    ple = gate * _dot(p_ref[...].astype(BF16), wp_ref[...])
    o_ref[...] = _layer_norm_rows(alpha * x1 + f + ple, lg_ref[...], lb_ref[...])


def _ple_ln2(x1, p2, y, route, wg, bg, wp, lg, lb, alpha, tm):
    t = x1.shape[0]
    n_parts = 1 if route is None else TOP_K
    steps = t // tm
    row = lambda width: pl.BlockSpec((tm, width), lambda i: (i, 0))
    full = lambda shape: pl.BlockSpec(shape, lambda i: (0,) * len(shape))
    in_specs = [row(D_MODEL), row(PLE_DIM)]
    if n_parts > 1:
        in_specs += [pl.BlockSpec((tm * TOKEN_TILE, LANE), lambda i, k=k: (k * steps + i, 0)) for k in range(n_parts)]
    else:
        in_specs.append(row(D_MODEL))
    args = [x1, p2] + [y] * n_parts
    if n_parts > 1:
        in_specs.append(row(LANE))
        args.append(route)
    in_specs += [full((D_MODEL, D_MODEL)), full((1, D_MODEL)), full((PLE_DIM, D_MODEL)),
                 full((1, D_MODEL)), full((1, D_MODEL))]
    args += [wg.astype(BF16), bg[None], wp.astype(BF16), lg[None], lb[None]]
    return pl.pallas_call(
        functools.partial(_ple_ln2_kernel, n_parts=n_parts, alpha=alpha),
        grid=(steps,), in_specs=in_specs, out_specs=row(D_MODEL),
        out_shape=jax.ShapeDtypeStruct((t, D_MODEL), F32),
        compiler_params=_params(("parallel",)),
        name="ple_ln2",
    )(*args)


def _route_tables(route, t):
    flat_e = route[:, :TOP_K].astype(jnp.int32).reshape(-1)
    tk = flat_e.shape[0]
    onehot = (flat_e[:, None] == jnp.arange(N_EXPERTS, dtype=jnp.int32)[None, :]).astype(jnp.int32)
    before = jnp.cumsum(onehot, axis=0) - onehot
    rank = jnp.sum(before * onehot, axis=1)
    counts = jnp.sum(onehot, axis=0)
    padded = (counts + EXPERT_BLOCK - 1) // EXPERT_BLOCK * EXPERT_BLOCK
    pend = jnp.cumsum(padded)
    dest = (pend - padded)[flat_e] + rank
    n_blocks = -(-tk // EXPERT_BLOCK) + N_EXPERTS
    n_rows = n_blocks * EXPERT_BLOCK
    pair = jnp.arange(tk, dtype=jnp.int32)
    row_tok = jnp.zeros((n_rows,), jnp.int32).at[dest].set(pair // TOP_K)
    is_pad = jnp.ones((n_rows,), jnp.int32).at[dest].set(0)
    pad_slot = tk + jnp.cumsum(is_pad) - is_pad
    row_slot = pad_slot.astype(jnp.int32).at[dest].set((pair % TOP_K) * t + pair // TOP_K)
    block_start = jnp.arange(n_blocks, dtype=jnp.int32) * EXPERT_BLOCK
    block_e = jnp.sum((pend[None, :] <= block_start[:, None]).astype(jnp.int32), axis=1)
    block_e = jnp.minimum(block_e, N_EXPERTS - 1).astype(jnp.int32)
    return row_tok, row_slot, n_rows, block_e


def kernel(x, p, positions, w_in, pool_w, pool_scale, sg_ln_g, sg_ln_b, sg_w, sg_b, cmp_k_pos, cmp_k_w1, cmp_k_w2, cmp_v_pos, cmp_v_w1, cmp_v_w2, w_pool_out, w_sg_out, w_nsa_out, w_out, ln1_g, ln1_b, ffn_w_gate, ffn_w_up, ffn_w_down, moe_router, moe_router_b, moe_w_gate, moe_w_up, moe_w_down, ple_gate_w, ple_gate_b, ple_proj, ln2_g, ln2_b):
    b, seq, d = x.shape
    depth = w_in.shape[0]
    t = b * seq
    alpha = (2 * depth) ** 0.25
    tm = 512
    tabs = _rope_tables(positions)
    x2 = x.reshape(t, d)
    for i in range(depth):
        proj = _proj(x2, _prep_w_in(w_in[i]), tabs, tm)
        pm, sgm = _mixers(proj, pool_w[i], pool_scale[i], sg_ln_g[i], sg_ln_b[i], sg_w[i], sg_b[i], seq, tm)
        qT, chunks, ks, vsT, kw, vwT, gates = _nsa_layouts(proj, b, seq)
        pos = jnp.stack([cmp_k_pos[i], cmp_v_pos[i]]).reshape(2, 1, CMP_LEN * HEAD_DIM)
        cmp = _compress(chunks, pos, jnp.stack([cmp_k_w1[i], cmp_v_w1[i]]).astype(BF16),
                        jnp.stack([cmp_k_w2[i], cmp_v_w2[i]]).astype(BF16))
        oT = _nsa(qT, cmp[0], cmp[1].transpose(0, 1, 3, 2), ks, vsT, kw, vwT, gates, seq)
        nq = seq // Q_BLOCK
        on = oT.reshape(b, NSA_KV_HEADS, HEAD_DIM, nq, GROUP, Q_BLOCK).transpose(0, 3, 5, 1, 4, 2).reshape(t, Q_WIDTH)
        j = i // 2
        moe = i % 2 == 1
        merged = _merge(pm, sgm, on, proj, x2, w_pool_out[i], w_sg_out[i], w_nsa_out[i], w_out[i],
                        ln1_g[i], ln1_b[i], alpha, tm, router=(moe_router[j], moe_router_b[j]) if moe else None)
        if moe:
            x1, route, x1_tiles = merged
            row_tok, row_slot, n_slots, block_e = _route_tables(route, t)
            y = _moe(x1_tiles, row_tok, row_slot, n_slots, block_e, moe_w_gate[j].astype(BF16),
                     moe_w_up[j].astype(BF16), moe_w_down[j].astype(BF16))
        else:
            x1, route = merged, None
            y = _swiglu(x1, ffn_w_gate[j].astype(BF16), ffn_w_up[j].astype(BF16), ffn_w_down[j].astype(BF16), tm)
        x2 = _ple_ln2(x1, p[i].reshape(t, PLE_DIM), y, route, ple_gate_w[i], ple_gate_b[i], ple_proj[i],
                      ln2_g[i], ln2_b[i], alpha, tm)
    return x2.reshape(b, seq, d)
```

```python
import functools

import jax
import jax.numpy as jnp
import numpy as np
from jax import lax
from jax.experimental import pallas as pl
from jax.experimental.pallas import tpu as pltpu

F32 = jnp.float32
BF16 = jnp.bfloat16

D_MODEL = 1024
POOL_WIDTH = 512
POOL_WINDOWS = (2, 4, 8, 16)
POOL_HALO = 16
SG_WIDTH = 512
SG_HEADS = 4
SG_CHUNK = 128
NSA_Q_HEADS = 8
NSA_KV_HEADS = 2
GROUP = NSA_Q_HEADS // NSA_KV_HEADS
HEAD_DIM = 64
Q_WIDTH = NSA_Q_HEADS * HEAD_DIM
KV_WIDTH = NSA_KV_HEADS * HEAD_DIM
ROT_DIM = HEAD_DIM // 4
ROPE_THETA = 500000.0
CMP_LEN = 32
CMP_STRIDE = 16
CMP_HIDDEN = 256
SLC_LEN = 64
SLC_TOP = 16
WINDOW = 512
Q_BLOCK = 128
NEG_INF = -1e30
D_FF = 2816
N_EXPERTS = 8
TOP_K = 2
EXPERT_BLOCK = 512
PLE_DIM = 256
LN_EPS = 1e-5

LANE = 128
TOKEN_TILE = D_MODEL // LANE
PROJ_BLOCK = 512
NSA_STRIP = 64
SLC_BLOCK = 256
SLC_UNROLL = 4
CMP_TILE = 256
SEL_GROUP = 8
ONES_PAD = 16
PSUM_PAD = 8
LOG2_E = 1.4426950408889634
FF_TILE = 1408
VMEM_LIMIT = 56 * 1024 * 1024


def _params(semantics, vmem=VMEM_LIMIT):
    return pltpu.CompilerParams(dimension_semantics=semantics, vmem_limit_bytes=vmem)


def _dot(a, b):
    return jnp.dot(a, b, preferred_element_type=F32)


def _dot_nt(a, b):
    return lax.dot_general(a, b, (((1,), (1,)), ((), ())), preferred_element_type=F32)


def _store_token_tiles(ref, v):
    n = v.shape[0]
    for s in range(TOKEN_TILE):
        ref[pl.ds(s, n, stride=TOKEN_TILE), :] = v[:, s * LANE:(s + 1) * LANE]


def _load_token_tiles(ref):
    n = ref.shape[0] // TOKEN_TILE
    return jnp.concatenate([ref[pl.ds(s, n, stride=TOKEN_TILE), :] for s in range(TOKEN_TILE)], axis=1)


def _layer_norm_rows(v, g, b):
    mu = jnp.mean(v, axis=-1, keepdims=True)
    c = v - mu
    var = jnp.mean(c * c, axis=-1, keepdims=True)
    return c * lax.rsqrt(var + LN_EPS) * g + b


def _rope_tables(positions):
    half = ROT_DIM // 2
    inv = ROPE_THETA ** (-jnp.arange(half, dtype=F32) / half)
    ang = positions.astype(F32)[:, :, None] * inv
    cos, sin = jnp.cos(ang), jnp.sin(ang)
    rest = HEAD_DIM - ROT_DIM
    one = jnp.ones(ang.shape[:2] + (rest,), F32)
    zero_r = jnp.zeros(ang.shape[:2] + (rest,), F32)
    zero_h = jnp.zeros_like(sin)
    c = jnp.concatenate([cos, cos, one], -1)
    s1 = jnp.concatenate([zero_h, sin, zero_r], -1)
    s2 = jnp.concatenate([-sin, zero_h, zero_r], -1)
    reps = LANE // HEAD_DIM
    lane_tabs = tuple(jnp.tile(t, (1, 1, reps)).reshape(-1, LANE) for t in (c, s1, s2))
    return lane_tabs + (cos.reshape(-1, half).T, sin.reshape(-1, half).T)


def _proj_kernel(x_ref, wrow_ref, wkey_ref, wq_ref, wval_ref, wgate_ref, c_ref, s1_ref, s2_ref, cos_ref, sin_ref,
                 mix_ref, br_ref, qT_ref, keys_ref, cmp_ref, valT_ref, gateT_ref, *, tiles_per_seq):
    tm = x_ref.shape[0]
    half = ROT_DIM // 2
    xb = x_ref[...].astype(BF16)

    for ref, col0 in ((mix_ref, 0), (br_ref, mix_ref.shape[1])):
        for col in range(0, ref.shape[1], PROJ_BLOCK):
            ref[:, col:col + PROJ_BLOCK] = _dot(xb, wrow_ref[:, col0 + col:col0 + col + PROJ_BLOCK]).astype(ref.dtype)

    def rope(v):
        return (v * c_ref[...] + pltpu.roll(v, half, 1) * s1_ref[...]
                + pltpu.roll(v, LANE - half, 1) * s2_ref[...])

    tpos = (pl.program_id(0) % tiles_per_seq) * tm + lax.broadcasted_iota(jnp.int32, (tm, LANE), 0)
    lane = lax.broadcasted_iota(jnp.int32, (tm, LANE), 1)
    onehot = jnp.where(lane == HEAD_DIM + (tpos // SLC_LEN) % SEL_GROUP, 1.0, 0.0)
    for k in range(4):
        acc = _dot(xb, wkey_ref[:, k * 2 * LANE:(k + 1) * 2 * LANE])
        for h in range(NSA_KV_HEADS):
            v = acc[:, h * LANE:(h + 1) * LANE]
            if k < 3:
                v = rope(v)
            if k == 0:
                cmp_ref[0, h] = v
            elif k == 1:
                keys_ref[0, h] = (v + onehot).astype(keys_ref.dtype)
            elif k == 2:
                keys_ref[1, h] = v.astype(keys_ref.dtype)
            else:
                cmp_ref[1, h] = v

    cos, sin = cos_ref[...], sin_ref[...]
    q_scale = HEAD_DIM ** -0.5 * LOG2_E
    for h in range(NSA_KV_HEADS):
        qt = _dot_nt(wq_ref[h], xb)
        for g in range(GROUP):
            blk = qt[g * HEAD_DIM:(g + 1) * HEAD_DIM]
            x1, x2 = blk[:half], blk[half:ROT_DIM]
            y = jnp.concatenate([x1 * cos - x2 * sin, x2 * cos + x1 * sin, blk[ROT_DIM:]], axis=0) * q_scale
            y = y.astype(qT_ref.dtype)
            for nn in range(tm // Q_BLOCK):
                dst = nn * GROUP * Q_BLOCK + g * Q_BLOCK
                qT_ref[h, :, dst:dst + Q_BLOCK] = y[:, nn * Q_BLOCK:(nn + 1) * Q_BLOCK]

    vt = _dot_nt(wval_ref[...], xb)
    ones_pad = jnp.where(lax.broadcasted_iota(jnp.int32, (ONES_PAD, tm), 0) == 0, 1.0, 0.0).astype(valT_ref.dtype)
    for k in range(2):
        for h in range(NSA_KV_HEADS):
            row = (k * NSA_KV_HEADS + h) * HEAD_DIM
            valT_ref[k, h, :HEAD_DIM, :] = vt[row:row + HEAD_DIM].astype(valT_ref.dtype)
            valT_ref[k, h, HEAD_DIM:, :] = ones_pad

    gt = _dot_nt(wgate_ref[...], xb)
    for h in range(NSA_KV_HEADS):
        gateT_ref[h] = gt[h * 16:(h + 1) * 16].astype(gateT_ref.dtype)


def _proj(x2, weights, tabs, b, seq, tm):
    t = x2.shape[0]
    tps = seq // tm
    wrow, wkey, wq, wval, wgate = weights
    h = NSA_KV_HEADS
    n_mix = POOL_WIDTH + 2 * SG_WIDTH
    row = lambda i: (i, 0)
    const = lambda shape: pl.BlockSpec(shape, lambda i: (0,) * len(shape), pipeline_mode=pl.Buffered(1))
    return pl.pallas_call(
        functools.partial(_proj_kernel, tiles_per_seq=tps),
        grid=(t // tm,),
        in_specs=[pl.BlockSpec((tm, D_MODEL), row), const(wrow.shape), const(wkey.shape), const(wq.shape),
                  const(wval.shape), const(wgate.shape),
                  pl.BlockSpec((tm, LANE), row), pl.BlockSpec((tm, LANE), row), pl.BlockSpec((tm, LANE), row),
                  pl.BlockSpec((ROT_DIM // 2, tm), lambda i: (0, i)), pl.BlockSpec((ROT_DIM // 2, tm), lambda i: (0, i))],
        out_specs=[pl.BlockSpec((tm, n_mix), row), pl.BlockSpec((tm, 3 * D_MODEL), row),
                   pl.BlockSpec((None, h, HEAD_DIM, tm * GROUP), lambda i: (i // tps, 0, 0, i % tps)),
                   pl.BlockSpec((2, None, h, tm, LANE), lambda i: (0, i // tps, 0, i % tps, 0)),
                   pl.BlockSpec((2, None, h, tm, LANE), lambda i: (0, i // tps, 0, i % tps, 0)),
                   pl.BlockSpec((2, None, h, HEAD_DIM + ONES_PAD, tm), lambda i: (0, i // tps, 0, 0, i % tps)),
                   pl.BlockSpec((None, h, 16, tm), lambda i: (i // tps, 0, 0, i % tps))],
        out_shape=[jax.ShapeDtypeStruct((t, n_mix), BF16), jax.ShapeDtypeStruct((t, 3 * D_MODEL), BF16),
                   jax.ShapeDtypeStruct((b, h, HEAD_DIM, seq * GROUP), BF16),
                   jax.ShapeDtypeStruct((2, b, h, seq, LANE), BF16),
                   jax.ShapeDtypeStruct((2, b, h, seq, LANE), F32),
                   jax.ShapeDtypeStruct((2, b, h, HEAD_DIM + ONES_PAD, seq), BF16),
                   jax.ShapeDtypeStruct((b, h, 16, seq), BF16)],
        compiler_params=_params(("parallel",)),
        name="proj",
    )(x2, wrow, wkey, wq, wval, wgate, *tabs)


def _prep_w_in(w):
    d = w.shape[0]
    h = NSA_KV_HEADS
    n_mix = POOL_WIDTH + 2 * SG_WIDTH
    kv0 = n_mix + Q_WIDTH
    n_gate = NSA_Q_HEADS * 3
    gate0 = kv0 + 6 * KV_WIDTH
    kv = lambda k: w[:, kv0 + k * KV_WIDTH:kv0 + (k + 1) * KV_WIDTH]
    wrow = jnp.concatenate([w[:, :n_mix], w[:, gate0 + n_gate:]], axis=1)
    pad_heads = lambda m: jnp.pad(m.reshape(d, h, HEAD_DIM), ((0, 0), (0, 0), (0, LANE - HEAD_DIM))).reshape(d, h * LANE)
    wkey = jnp.concatenate([pad_heads(kv(0)), pad_heads(kv(2)), pad_heads(kv(4)), pad_heads(kv(1))], axis=1)
    wq = w[:, n_mix:kv0].T.reshape(h, GROUP * HEAD_DIM, d)
    wval = jnp.concatenate([kv(3), kv(5)], axis=1).T
    wgate = jnp.pad(w[:, gate0:gate0 + n_gate].T.reshape(h, GROUP * 3, d), ((0, 0), (0, 16 - GROUP * 3), (0, 0)))
    return tuple(m.astype(BF16) for m in (wrow, wkey, wq, wval, wgate.reshape(h * 16, d)))


def _mixers_kernel(a_ref, ap_ref, u_ref, v_ref, pw_ref, ps_ref, lg_ref, lb_ref, sw_ref, sb_ref,
                   pm_ref, sg_ref, *, tiles_per_seq):
    i = pl.program_id(0)
    tm = a_ref.shape[0]
    it = i % tiles_per_seq
    gw = POOL_WIDTH // len(POOL_WINDOWS)

    a = a_ref[...].astype(F32)
    prev = jnp.where(it == 0, 0.0, ap_ref[...].astype(F32))
    ext = jnp.concatenate([prev, a], axis=0)
    tpos = (it * tm + lax.broadcasted_iota(jnp.int32, (tm, 1), 0) + 1).astype(F32)
    for g, w in enumerate(POOL_WINDOWS):
        s = ext[:, g * gw:(g + 1) * gw]
        k = 1
        while k < w:
            s = s + pltpu.roll(s, k, 0)
            k *= 2
        d = s[POOL_HALO:] / jnp.minimum(tpos, float(w)) - a[:, g * gw:(g + 1) * gw]
        y = _dot(d.astype(BF16), pw_ref[g]) * ps_ref[:, g * gw:(g + 1) * gw]
        pm_ref[:, g * gw:(g + 1) * gw] = y.astype(pm_ref.dtype)

    u = jax.nn.gelu(u_ref[...].astype(F32))
    v = _layer_norm_rows(jax.nn.gelu(v_ref[...].astype(F32)), lg_ref[...], lb_ref[...]).astype(BF16)
    hc = SG_WIDTH // SG_HEADS
    tri = (lax.broadcasted_iota(jnp.int32, (SG_CHUNK, SG_CHUNK), 0)
           >= lax.broadcasted_iota(jnp.int32, (SG_CHUNK, SG_CHUNK), 1))
    for g in range(SG_HEADS):
        ws = jnp.where(tri, sw_ref[g], 0.0).astype(BF16)
        bias = sb_ref[:, g:g + 1]
        for c in range(tm // SG_CHUNK):
            rows = slice(c * SG_CHUNK, (c + 1) * SG_CHUNK)
            cols = slice(g * hc, (g + 1) * hc)
            mixed = _dot(ws, v[rows, cols]) + bias
            sg_ref[rows, cols] = (u[rows, cols] * mixed).astype(sg_ref.dtype)


def _mixers(mix, pool_w, pool_scale, ln_g, ln_b, sg_w, sg_b, seq, tm):
    t = mix.shape[0]
    halo_blocks = tm // POOL_HALO
    full = lambda shape: pl.BlockSpec(shape, lambda i: (0,) * len(shape))
    return pl.pallas_call(
        functools.partial(_mixers_kernel, tiles_per_seq=seq // tm),
        grid=(t // tm,),
        in_specs=[pl.BlockSpec((tm, POOL_WIDTH), lambda i: (i, 0)),
                  pl.BlockSpec((POOL_HALO, POOL_WIDTH), lambda i: (jnp.maximum(i * halo_blocks - 1, 0), 0)),
                  pl.BlockSpec((tm, SG_WIDTH), lambda i: (i, 1)),
                  pl.BlockSpec((tm, SG_WIDTH), lambda i: (i, 2)),
                  full(pool_w.shape), full((1, POOL_WIDTH)), full((1, SG_WIDTH)), full((1, SG_WIDTH)),
                  full(sg_w.shape), full((SG_CHUNK, SG_HEADS))],
        out_specs=[pl.BlockSpec((tm, POOL_WIDTH), lambda i: (i, 0)),
                   pl.BlockSpec((tm, SG_WIDTH), lambda i: (i, 0))],
        out_shape=[jax.ShapeDtypeStruct((t, POOL_WIDTH), BF16), jax.ShapeDtypeStruct((t, SG_WIDTH), BF16)],
        compiler_params=_params(("parallel",)),
        name="mixers",
    )(mix, mix, mix, mix, pool_w.astype(BF16), pool_scale[None], ln_g[None], ln_b[None], sg_w, sg_b.T)


def _compress_kernel(src_ref, pos_ref, w1_ref, w2_ref, w2T_ref, o_ref, oT_ref):
    nc = o_ref.shape[0]
    first = jnp.zeros((nc, CMP_HIDDEN), F32)
    second = jnp.zeros((nc, CMP_HIDDEN), F32)
    for p in range(CMP_STRIDE):
        rows = src_ref[pl.ds(p, nc, stride=CMP_STRIDE), :]
        first = first + _dot((rows + pos_ref[p:p + 1, :]).astype(BF16), w1_ref[p])
        q = CMP_STRIDE + p
        second = second + _dot((rows + pos_ref[q:q + 1, :]).astype(BF16), w1_ref[q])
    pre = first + pltpu.roll(second, nc - 1, 0)
    hidden = jax.nn.gelu(pre).astype(BF16)
    o_ref[...] = _dot(hidden, w2_ref[...]).astype(o_ref.dtype)
    oT_ref[...] = _dot_nt(w2T_ref[...], hidden).astype(oT_ref.dtype)


def _compress(src, pos, w1, w2, w2T):
    _, b, h, seq, _ = src.shape
    nc = seq // CMP_STRIDE
    sel = lambda n: pl.BlockSpec((None,) + n, lambda s, bi, hi: (s,) + (0,) * len(n))
    return pl.pallas_call(
        _compress_kernel,
        grid=(2, b, h),
        in_specs=[pl.BlockSpec((None, None, None, seq, LANE), lambda s, bi, hi: (s, bi, hi, 0, 0)),
                  sel(pos.shape[1:]), sel(w1.shape[1:]), sel(w2.shape[1:]), sel(w2T.shape[1:])],
        out_specs=[pl.BlockSpec((None, None, None, nc, HEAD_DIM), lambda s, bi, hi: (s, bi, hi, 0, 0)),
                   pl.BlockSpec((None, None, None, HEAD_DIM, nc), lambda s, bi, hi: (s, bi, hi, 0, 0))],
        out_shape=[jax.ShapeDtypeStruct((2, b, h, nc, HEAD_DIM), BF16),
                   jax.ShapeDtypeStruct((2, b, h, HEAD_DIM, nc), BF16)],
        compiler_params=_params(("parallel", "parallel", "parallel")),
        name="compress",
    )(src, pos, w1, w2, w2T)


def _prep_compress(k_pos, k_w1, k_w2, v_pos, v_w1, v_w2):
    pad = LANE - HEAD_DIM
    pos = jnp.pad(jnp.stack([k_pos, v_pos]), ((0, 0), (0, 0), (0, pad)))
    w1 = jnp.stack([k_w1, v_w1]).reshape(2, CMP_LEN, HEAD_DIM, CMP_HIDDEN)
    w1 = jnp.pad(w1, ((0, 0), (0, 0), (0, pad), (0, 0))).astype(BF16)
    w2 = jnp.stack([k_w2, v_w2]).astype(BF16)
    return pos, w1, w2, w2.transpose(0, 2, 1)


def _score_strips(k_ref, k0, n_strips, q_aug, s_ref, mask):
    mx = None
    for r in range(n_strips):
        start = pl.multiple_of(k0 + r * NSA_STRIP, NSA_STRIP)
        s = _dot(k_ref[pl.ds(start, NSA_STRIP), :], q_aug)
        if mask is not None:
            s = mask(r, start, s)
        s_ref[r * NSA_STRIP:(r + 1) * NSA_STRIP, :] = s
        mx = s if mx is None else jnp.maximum(mx, s)
    return jnp.max(mx, axis=0, keepdims=True)


def _prob_strips(n_strips, m, s_ref, p_ref):
    for r in range(n_strips):
        rows = slice(r * NSA_STRIP, (r + 1) * NSA_STRIP)
        p_ref[rows, :] = jnp.exp2(s_ref[rows, :] - m).astype(BF16)


def _nsa_kernel(qT_ref, kc_ref, vcT_ref, ks_ref, vsT_ref, kw_ref, vwT_ref, g_ref, bound_ref, o_ref,
                s_ref, p_ref, pc_ref, s2_ref, p2_ref, psum_ref, selneg_ref, *, n_top):
    n = pl.program_id(2)
    qs = n * Q_BLOCK
    width = GROUP * Q_BLOCK
    qT = qT_ref[...]
    t_row = qs + lax.broadcasted_iota(jnp.int32, (1, width), 1) % Q_BLOCK
    nc = kc_ref.shape[0]
    ns = selneg_ref.shape[0]
    strip_iota = lax.broadcasted_iota(jnp.int32, (NSA_STRIP, 1), 0)

    @pl.when(n == 0)
    def _():
        psum_ref[...] = jnp.zeros(psum_ref.shape, F32)
        pc_ref[...] = jnp.zeros(pc_ref.shape, BF16)

    n_ct = (n * (Q_BLOCK // CMP_STRIDE) + (Q_BLOCK - CMP_LEN) // CMP_STRIDE + 1 + CMP_TILE - 1) // CMP_TILE
    strips_per_tile = CMP_TILE // NSA_STRIP

    def cmp_scores(jt, mx):
        for r in range(strips_per_tile):
            start = pl.multiple_of(jt * CMP_TILE + r * NSA_STRIP, NSA_STRIP)
            s = _dot(kc_ref[pl.ds(start, NSA_STRIP), :], qT)
            c_end = (start + strip_iota) * CMP_STRIDE + (CMP_LEN - 1)
            s = jnp.where(c_end <= t_row, s, NEG_INF)
            s_ref[pl.ds(start, NSA_STRIP), :] = s
            mx = jnp.maximum(mx, s)
        return mx

    mx = lax.fori_loop(0, n_ct, cmp_scores, jnp.full((NSA_STRIP, width), NEG_INF, F32))
    m_cmp = jnp.maximum(jnp.max(mx, axis=0, keepdims=True), 0.1 * NEG_INF)

    def cmp_probs(jt, lsum):
        for r in range(strips_per_tile):
            rows = pl.ds(pl.multiple_of(jt * CMP_TILE + r * NSA_STRIP, NSA_STRIP), NSA_STRIP)
            e = jnp.exp2(s_ref[rows, :] - m_cmp)
            s_ref[rows, :] = e
            pc_ref[rows, :] = e.astype(BF16)
            lsum = lsum + e
        return lsum

    lsum = lax.fori_loop(0, n_ct, cmp_probs, jnp.zeros((NSA_STRIP, width), F32))
    l = jnp.sum(lsum, axis=0, keepdims=True)
    inv = jnp.where(l > 0.0, 1.0 / l, 0.0)
    o_cmp = _dot(vcT_ref[...], pc_ref[...]) * inv

    def cmp_group_sum(jt, c):
        for r in range(strips_per_tile):
            start = pl.multiple_of(jt * CMP_TILE + r * NSA_STRIP, NSA_STRIP)
            p = s_ref[pl.ds(start, NSA_STRIP), :] * inv
            psum = p[:, :Q_BLOCK]
            for g in range(1, GROUP):
                psum = psum + p[:, g * Q_BLOCK:(g + 1) * Q_BLOCK]
            psum_ref[pl.ds(pl.multiple_of(PSUM_PAD + start, 8), NSA_STRIP), :] = psum
        return c

    lax.fori_loop(0, n_ct, cmp_group_sum, 0)

    per = SLC_LEN // CMP_STRIDE
    part = [psum_ref[pl.ds(PSUM_PAD + k, ns, stride=per), :] for k in range(per)]
    before = psum_ref[pl.ds(PSUM_PAD - 1, ns, stride=per), :]
    imp = 2.0 * (part[0] + part[1] + part[2]) + part[3] + before

    blk = lax.broadcasted_iota(jnp.int32, (ns, Q_BLOCK), 0)
    cur = (qs + lax.broadcasted_iota(jnp.int32, (ns, Q_BLOCK), 1)) // SLC_LEN
    taken = -2.0
    score = jnp.where(blk <= cur, imp, -1.0)
    score = jnp.where(blk == 0, taken, jnp.where(blk == cur, taken, jnp.where(blk == cur - 1, taken, score)))
    blk_f = blk.astype(F32)
    for _ in range(n_top - 3):
        top = jnp.max(score, axis=0, keepdims=True)
        cand = jnp.where(score == top, blk_f, float(ns))
        first = jnp.min(cand, axis=0, keepdims=True)
        score = jnp.where(cand == first, taken, score)
    selneg_ref[...] = jnp.where(score == taken, 0.0, NEG_INF)

    strips = SLC_BLOCK // NSA_STRIP
    sel_rows = ks_ref.shape[1] - HEAD_DIM - SEL_GROUP
    last_block = ks_ref.shape[0] // SLC_BLOCK - 1
    n_diag = n // (SLC_BLOCK // Q_BLOCK)

    def past_only(r, start, s):
        return jnp.where(start + strip_iota <= t_row, s, NEG_INF)

    def slc_scores(kb, slot):
        variant = jnp.where(kb < n_diag, 0, jnp.where(kb > n_diag, 3, 1 + n % (SLC_BLOCK // Q_BLOCK)))

        def limit(r, start, s):
            return jnp.minimum(s, bound_ref[variant, r * NSA_STRIP:(r + 1) * NSA_STRIP, :])

        kb = jnp.minimum(kb, last_block)
        k0 = pl.multiple_of(kb * SLC_BLOCK, SLC_BLOCK)
        g0 = pl.multiple_of((kb * strips // SEL_GROUP) * SEL_GROUP, SEL_GROUP)
        bias = jnp.concatenate([selneg_ref[pl.ds(g0, SEL_GROUP), :]] * GROUP, axis=1)
        bias = jnp.concatenate([bias, jnp.zeros((sel_rows, width), F32)], axis=0).astype(BF16)
        q_aug = jnp.concatenate([qT, bias], axis=0)
        return _score_strips(ks_ref, k0, strips, q_aug, s2_ref.at[slot], limit)

    def slc_consume(kb, slot, m, acc, mx):
        k0 = pl.multiple_of(kb * SLC_BLOCK, SLC_BLOCK)
        m_new = jnp.maximum(m, mx)
        _prob_strips(strips, m_new, s2_ref.at[slot], p2_ref.at[slot])
        pv = _dot(vsT_ref[:, pl.ds(k0, SLC_BLOCK)], p2_ref[slot])
        return m_new, jnp.exp2(m - m_new) * acc + pv

    def slc_group(j, carry):
        m, acc, mx_next = carry
        for u in range(SLC_UNROLL):
            kb = SLC_UNROLL * j + u
            mx_cur = mx_next
            mx_next = slc_scores(kb + 1, (u + 1) % 2)
            m, acc = slc_consume(kb, u % 2, m, acc, mx_cur)
        return m, acc, mx_next

    init = (jnp.full((1, width), NEG_INF, F32), jnp.zeros((vsT_ref.shape[0], width), F32), slc_scores(0, 0))
    _, acc, _ = lax.fori_loop(0, (n_diag + SLC_UNROLL) // SLC_UNROLL, slc_group, init)
    o_slc = acc[:HEAD_DIM] / acc[HEAD_DIM:HEAD_DIM + 1]

    span = WINDOW + Q_BLOCK
    w_strips = span // NSA_STRIP
    edge = Q_BLOCK // NSA_STRIP
    w0 = pl.multiple_of(jnp.maximum(qs - WINDOW, 0), Q_BLOCK)

    def band_edges(r, start, s):
        if r < edge:
            return jnp.where(t_row - (start + strip_iota) < WINDOW, s, NEG_INF)
        if r >= w_strips - edge:
            return past_only(r, start, s)
        return s

    q_win = jnp.concatenate([qT, jnp.zeros((kw_ref.shape[1] - HEAD_DIM, width), BF16)], axis=0)
    m_win = lax.cond(n >= WINDOW // Q_BLOCK,
                     lambda: _score_strips(kw_ref, w0, w_strips, q_win, s_ref, band_edges),
                     lambda: _score_strips(kw_ref, w0, w_strips, q_win, s_ref, past_only))
    _prob_strips(w_strips, m_win, s_ref, p_ref)
    pv = _dot(vwT_ref[:, pl.ds(w0, span)], p_ref[:span, :])
    o_win = pv[:HEAD_DIM] / pv[HEAD_DIM:HEAD_DIM + 1]

    gates = jax.nn.sigmoid(g_ref[...].astype(F32))

    def gate(c):
        return jnp.concatenate([gates[g * 3 + c:g * 3 + c + 1, :] for g in range(GROUP)], axis=1)

    o = gate(0) * o_cmp + gate(1) * o_slc + gate(2) * o_win
    for g in range(GROUP):
        o_ref[:, g * HEAD_DIM:(g + 1) * HEAD_DIM] = o[:, g * Q_BLOCK:(g + 1) * Q_BLOCK].T.astype(o_ref.dtype)


def _causal_bounds():
    big = np.float32(3.0e38)
    key = np.arange(SLC_BLOCK)[:, None]
    query = np.tile(np.arange(Q_BLOCK), GROUP)[None, :]
    tables = [np.full((SLC_BLOCK, GROUP * Q_BLOCK), big, np.float32)]
    for v in range(SLC_BLOCK // Q_BLOCK):
        tables.append(np.where(key <= query + v * Q_BLOCK, big, np.float32(NEG_INF)).astype(np.float32))
    tables.append(np.full((SLC_BLOCK, GROUP * Q_BLOCK), NEG_INF, np.float32))
    return jnp.asarray(np.stack(tables))


def _nsa(qT, cmp, cmpT, keys, valT, gates, seq):
    b, h = qT.shape[:2]
    nq = seq // Q_BLOCK
    nc = cmp.shape[3]
    ns = seq // SLC_LEN
    width = GROUP * Q_BLOCK
    assert nc == ns * (SLC_LEN // CMP_STRIDE) and nc % CMP_TILE == 0 and seq % (SLC_BLOCK * SLC_UNROLL) == 0
    assert seq >= WINDOW + Q_BLOCK and ns >= SLC_TOP
    bounds = _causal_bounds()
    per_head = lambda arr, k: pl.BlockSpec((None, None, None) + arr.shape[3:], lambda bi, hi, n, k=k: (k, bi, hi, 0, 0))
    return pl.pallas_call(
        functools.partial(_nsa_kernel, n_top=SLC_TOP),
        grid=(b, h, nq),
        in_specs=[pl.BlockSpec((None, None, HEAD_DIM, width), lambda bi, hi, n: (bi, hi, 0, n)),
                  per_head(cmp, 0), per_head(cmpT, 1),
                  per_head(keys, 0), per_head(valT, 0), per_head(keys, 1), per_head(valT, 1),
                  pl.BlockSpec((None, None, 16, Q_BLOCK), lambda bi, hi, n: (bi, hi, 0, n)),
                  pl.BlockSpec(bounds.shape, lambda bi, hi, n: (0, 0, 0))],
        out_specs=pl.BlockSpec((Q_BLOCK, GROUP * HEAD_DIM), lambda bi, hi, n: (bi * nq + n, hi)),
        out_shape=jax.ShapeDtypeStruct((b * seq, Q_WIDTH), BF16),
        scratch_shapes=[pltpu.VMEM((max(nc, WINDOW + Q_BLOCK), width), F32),
                        pltpu.VMEM((WINDOW + Q_BLOCK, width), BF16),
                        pltpu.VMEM((nc, width), BF16),
                        pltpu.VMEM((2, SLC_BLOCK, width), F32),
                        pltpu.VMEM((2, SLC_BLOCK, width), BF16),
                        pltpu.VMEM((PSUM_PAD + nc, Q_BLOCK), F32),
                        pltpu.VMEM((ns, Q_BLOCK), F32)],
        compiler_params=_params(("parallel", "parallel", "arbitrary")),
        name="nsa",
    )(qT, cmp, cmpT, keys, valT, keys, valT, gates, bounds)


def _merge_kernel(*refs, with_router, alpha):
    (pm_ref, sg_ref, on_ref, gp_ref, gs_ref, gn_ref, x_ref, wp_ref, ws_ref, wn_ref, wo_ref, lg_ref, lb_ref) = refs[:13]
    if with_router:
        wr_ref, br_ref, x1_ref, route_ref, x1t_ref = refs[13:]
    else:
        (x1_ref,) = refs[13:]
    sig = lambda r: jax.nn.sigmoid(r[...].astype(F32))
    y = (sig(gp_ref) * _dot(pm_ref[...], wp_ref[...]) + sig(gs_ref) * _dot(sg_ref[...], ws_ref[...])
         + sig(gn_ref) * _dot(on_ref[...], wn_ref[...]))
    hmix = _dot(y.astype(BF16), wo_ref[...])
    x1 = _layer_norm_rows(alpha * x_ref[...] + hmix, lg_ref[...], lb_ref[...])
    x1_ref[...] = x1
    if with_router:
        _store_token_tiles(x1t_ref, x1)
        wr = wr_ref[...]
        w_hi = wr.astype(BF16)
        w_lo = (wr - w_hi.astype(F32)).astype(BF16)
        x_hi = x1.astype(BF16)
        x_lo = (x1 - x_hi.astype(F32)).astype(BF16)
        logits = _dot(x_hi, w_hi) + (_dot(x_lo, w_hi) + _dot(x_hi, w_lo)) + br_ref[...]
        lane = lax.broadcasted_iota(jnp.int32, logits.shape, 1)
        lane_f = lane.astype(F32)
        logits = jnp.where(lane < N_EXPERTS, logits, -jnp.inf)
        m1 = jnp.max(logits, axis=-1, keepdims=True)
        i1 = jnp.min(jnp.where(logits == m1, lane_f, float(LANE)), axis=-1, keepdims=True)
        rest = jnp.where(lane_f == i1, -jnp.inf, logits)
        m2 = jnp.max(rest, axis=-1, keepdims=True)
        i2 = jnp.min(jnp.where(rest == m2, lane_f, float(LANE)), axis=-1, keepdims=True)
        e2 = jnp.exp(m2 - m1)
        w1 = 1.0 / (1.0 + e2)
        route_ref[...] = jnp.where(lane == 0, i1, jnp.where(lane == 1, i2, jnp.where(lane == 2, w1,
                                   jnp.where(lane == 3, e2 * w1, 0.0))))


def _merge(pm, sgm, on, br, x2, wp, ws, wn, wo, lg, lb, alpha, tm, router=None):
    t = x2.shape[0]
    row = lambda width, cb=0: pl.BlockSpec((tm, width), lambda i, cb=cb: (i, cb))
    full = lambda shape: pl.BlockSpec(shape, lambda i: (0,) * len(shape))
    in_specs = [row(POOL_WIDTH), row(SG_WIDTH), row(Q_WIDTH),
                row(D_MODEL, 0), row(D_MODEL, 1), row(D_MODEL, 2),
                row(D_MODEL), full(wp.shape), full(ws.shape), full(wn.shape), full(wo.shape),
                full((1, D_MODEL)), full((1, D_MODEL))]
    args = [pm, sgm, on, br, br, br, x2, wp.astype(BF16), ws.astype(BF16), wn.astype(BF16), wo.astype(BF16),
            lg[None], lb[None]]
    out_specs = [row(D_MODEL)]
    out_shape = [jax.ShapeDtypeStruct((t, D_MODEL), F32)]
    if router is not None:
        w_router, b_router = router
        pad = LANE - N_EXPERTS
        in_specs += [full((D_MODEL, LANE)), full((1, LANE))]
        args += [jnp.pad(w_router, ((0, 0), (0, pad))), jnp.pad(b_router, (0, pad))[None]]
        out_specs += [row(LANE), pl.BlockSpec((tm * TOKEN_TILE, LANE), lambda i: (i, 0))]
        out_shape += [jax.ShapeDtypeStruct((t, LANE), F32), jax.ShapeDtypeStruct((t * TOKEN_TILE, LANE), F32)]
    out = pl.pallas_call(
        functools.partial(_merge_kernel, with_router=router is not None, alpha=alpha),
        grid=(t // tm,), in_specs=in_specs, out_specs=out_specs, out_shape=out_shape,
        compiler_params=_params(("parallel",)),
        name="merge",
    )(*args)
    return out if router is not None else out[0]


def _tile_rows(i):
    start = i * TOKEN_TILE if isinstance(i, int) else pl.multiple_of(i * TOKEN_TILE, TOKEN_TILE)
    return pl.ds(start, TOKEN_TILE)


def _for_rows(n, fn):
    def body(r, c):
        fn(r)
        return c
    lax.fori_loop(0, n, body, 0, unroll=8)


def _moe_kernel(be_ref, tok_ref, tok_next_ref, slot_ref, x_hbm, wg_ref, wu_ref, wd_ref, y_hbm,
                xg_ref, xb_ref, acc_ref, yt_ref, gather_sem, scatter_sem):
    i = pl.program_id(0)
    k = pl.program_id(1)
    n_blocks = pl.num_programs(0)
    rows = xb_ref.shape[0]

    def gather(idx_ref, slot):
        return lambda r: pltpu.make_async_copy(x_hbm.at[_tile_rows(idx_ref[0, r])], xg_ref.at[slot, _tile_rows(r)],
                                               gather_sem.at[slot])

    def scatter(idx_ref):
        return lambda r: pltpu.make_async_copy(yt_ref.at[_tile_rows(r)], y_hbm.at[_tile_rows(idx_ref[0, r])],
                                               scatter_sem)

    @pl.when(k == 0)
    def _():
        slot = i % 2

        @pl.when(i == 0)
        def _():
            _for_rows(rows, lambda r: gather(tok_ref, 0)(r).start())

        pltpu.make_async_copy(x_hbm.at[pl.ds(0, rows * TOKEN_TILE)], xg_ref.at[slot], gather_sem.at[slot]).wait()

        @pl.when(i + 1 < n_blocks)
        def _():
            _for_rows(rows, lambda r: gather(tok_next_ref, 1 - slot)(r).start())

        for s in range(TOKEN_TILE):
            xb_ref[:, s * LANE:(s + 1) * LANE] = xg_ref[slot, pl.ds(s, rows, stride=TOKEN_TILE), :].astype(BF16)

    xb = xb_ref[...]
    hidden = (jax.nn.silu(_dot(xb, wg_ref[...])) * _dot(xb, wu_ref[...])).astype(BF16)
    part = _dot(hidden, wd_ref[...])

    @pl.when(k == 0)
    def _():
        acc_ref[...] = part

    @pl.when(k == pl.num_programs(1) - 1)
    def _():
        def wait_scatter():
            pltpu.make_async_copy(yt_ref, y_hbm.at[pl.ds(0, rows * TOKEN_TILE)], scatter_sem).wait()

        pl.when(i > 0)(wait_scatter)
        _store_token_tiles(yt_ref, acc_ref[...] + part)
        _for_rows(rows, lambda r: scatter(slot_ref)(r).start())
        pl.when(i == n_blocks - 1)(wait_scatter)


def _moe(x_tiles, row_tok, row_slot, n_slots, block_e, wg, wu, wd):
    rows = EXPERT_BLOCK
    n_blocks = row_tok.shape[0] // rows
    kf = D_FF // FF_TILE
    assert kf == 2
    idx_spec = lambda fn: pl.BlockSpec((None, 1, rows), lambda i, k, be: (fn(i), 0, 0), memory_space=pltpu.SMEM)
    tok3 = row_tok.reshape(n_blocks, 1, rows)
    slot3 = row_slot.reshape(n_blocks, 1, rows)
    return pl.pallas_call(
        _moe_kernel,
        grid_spec=pltpu.PrefetchScalarGridSpec(
            num_scalar_prefetch=1,
            grid=(n_blocks, kf),
            in_specs=[idx_spec(lambda i: i), idx_spec(lambda i: jnp.minimum(i + 1, n_blocks - 1)),
                      idx_spec(lambda i: i),
                      pl.BlockSpec(memory_space=pl.ANY),
                      pl.BlockSpec((None, D_MODEL, FF_TILE), lambda i, k, be: (be[i], 0, k)),
                      pl.BlockSpec((None, D_MODEL, FF_TILE), lambda i, k, be: (be[i], 0, k)),
                      pl.BlockSpec((None, FF_TILE, D_MODEL), lambda i, k, be: (be[i], k, 0))],
            out_specs=pl.BlockSpec(memory_space=pl.ANY),
            scratch_shapes=[pltpu.VMEM((2, rows * TOKEN_TILE, LANE), F32), pltpu.VMEM((rows, D_MODEL), BF16),
                            pltpu.VMEM((rows, D_MODEL), F32), pltpu.VMEM((rows * TOKEN_TILE, LANE), F32),
                            pltpu.SemaphoreType.DMA((2,)), pltpu.SemaphoreType.DMA(())]),
        out_shape=jax.ShapeDtypeStruct((n_slots * TOKEN_TILE, LANE), F32),
        compiler_params=_params(("arbitrary", "arbitrary")),
        name="moe",
    )(block_e, tok3, tok3, slot3, x_tiles, wg, wu, wd)


def _swiglu_kernel(x_ref, wg_ref, wu_ref, wd_ref, o_ref, xb_ref):
    k = pl.program_id(1)

    @pl.when(k == 0)
    def _():
        xb_ref[...] = x_ref[...].astype(BF16)

    xb = xb_ref[...]
    hidden = (jax.nn.silu(_dot(xb, wg_ref[...])) * _dot(xb, wu_ref[...])).astype(BF16)
    part = _dot(hidden, wd_ref[...])

    @pl.when(k == 0)
    def _():
        o_ref[...] = part

    @pl.when(k != 0)
    def _():
        o_ref[...] += part


def _swiglu(rows, wg, wu, wd, tm):
    r = rows.shape[0]
    row = pl.BlockSpec((tm, D_MODEL), lambda i, k: (i, 0))
    return pl.pallas_call(
        _swiglu_kernel,
        grid=(r // tm, D_FF // FF_TILE),
        in_specs=[row,
                  pl.BlockSpec((D_MODEL, FF_TILE), lambda i, k: (0, k)),
                  pl.BlockSpec((D_MODEL, FF_TILE), lambda i, k: (0, k)),
                  pl.BlockSpec((FF_TILE, D_MODEL), lambda i, k: (k, 0))],
        out_specs=row,
        out_shape=jax.ShapeDtypeStruct(rows.shape, F32),
        scratch_shapes=[pltpu.VMEM((tm, D_MODEL), BF16)],
        compiler_params=_params(("parallel", "arbitrary")),
        name="swiglu",
    )(rows, wg, wu, wd)


def _ple_ln2_kernel(*refs, n_parts, alpha):
    x_ref, p_ref = refs[0], refs[1]
    parts = refs[2:2 + n_parts]
    rest = refs[2 + n_parts:]
    if n_parts > 1:
        route_ref, rest = rest[0], rest[1:]
    wg_ref, bg_ref, wp_ref, lg_ref, lb_ref, o_ref = rest
    x1 = x_ref[...]
    if n_parts > 1:
        route = route_ref[...]
        f = _load_token_tiles(parts[0]) * route[:, TOP_K:TOP_K + 1]
        for k in range(1, n_parts):
            f = f + _load_token_tiles(parts[k]) * route[:, TOP_K + k:TOP_K + k + 1]
    else:
        f = parts[0][...]
    gate = jax.nn.sigmoid(_dot(x1.astype(BF16), wg_ref[...]) + bg_ref[...])
    ple = gate * _dot(p_ref[...].astype(BF16), wp_ref[...])
    o_ref[...] = _layer_norm_rows(alpha * x1 + f + ple, lg_ref[...], lb_ref[...])


def _ple_ln2(x1, p2, y, route, wg, bg, wp, lg, lb, alpha, tm):
    t = x1.shape[0]
    n_parts = 1 if route is None else TOP_K
    steps = t // tm
    row = lambda width: pl.BlockSpec((tm, width), lambda i: (i, 0))
    full = lambda shape: pl.BlockSpec(shape, lambda i: (0,) * len(shape))
    in_specs = [row(D_MODEL), row(PLE_DIM)]
    if n_parts > 1:
        in_specs += [pl.BlockSpec((tm * TOKEN_TILE, LANE), lambda i, k=k: (k * steps + i, 0)) for k in range(n_parts)]
    else:
        in_specs.append(row(D_MODEL))
    args = [x1, p2] + [y] * n_parts
    if n_parts > 1:
        in_specs.append(row(LANE))
        args.append(route)
    in_specs += [full((D_MODEL, D_MODEL)), full((1, D_MODEL)), full((PLE_DIM, D_MODEL)),
                 full((1, D_MODEL)), full((1, D_MODEL))]
    args += [wg.astype(BF16), bg[None], wp.astype(BF16), lg[None], lb[None]]
    return pl.pallas_call(
        functools.partial(_ple_ln2_kernel, n_parts=n_parts, alpha=alpha),
        grid=(steps,), in_specs=in_specs, out_specs=row(D_MODEL),
        out_shape=jax.ShapeDtypeStruct((t, D_MODEL), F32),
        compiler_params=_params(("parallel",)),
        name="ple_ln2",
    )(*args)


def _route_tables(route, t):
    flat_e = route[:, :TOP_K].astype(jnp.int32).reshape(-1)
    tk = flat_e.shape[0]
    onehot = (flat_e[:, None] == jnp.arange(N_EXPERTS, dtype=jnp.int32)[None, :]).astype(jnp.int32)
    before = jnp.cumsum(onehot, axis=0) - onehot
    rank = jnp.sum(before * onehot, axis=1)
    counts = jnp.sum(onehot, axis=0)
    padded = (counts + EXPERT_BLOCK - 1) // EXPERT_BLOCK * EXPERT_BLOCK
    pend = jnp.cumsum(padded)
    dest = (pend - padded)[flat_e] + rank
    n_blocks = -(-tk // EXPERT_BLOCK) + N_EXPERTS
    n_rows = n_blocks * EXPERT_BLOCK
    row_pair = jnp.full((n_rows,), -1, jnp.int32).at[dest].set(jnp.arange(tk, dtype=jnp.int32))
    is_pad = (row_pair < 0).astype(jnp.int32)
    row_tok = jnp.where(row_pair < 0, 0, row_pair // TOP_K)
    pad_slot = tk + jnp.cumsum(is_pad) - is_pad
    row_slot = jnp.where(row_pair < 0, pad_slot, (row_pair % TOP_K) * t + row_pair // TOP_K).astype(jnp.int32)
    block_start = jnp.arange(n_blocks, dtype=jnp.int32) * EXPERT_BLOCK
    block_e = jnp.sum((pend[None, :] <= block_start[:, None]).astype(jnp.int32), axis=1)
    block_e = jnp.minimum(block_e, N_EXPERTS - 1).astype(jnp.int32)
    return row_tok, row_slot, n_rows, block_e


def kernel(x, p, positions, w_in, pool_w, pool_scale, sg_ln_g, sg_ln_b, sg_w, sg_b, cmp_k_pos, cmp_k_w1, cmp_k_w2, cmp_v_pos, cmp_v_w1, cmp_v_w2, w_pool_out, w_sg_out, w_nsa_out, w_out, ln1_g, ln1_b, ffn_w_gate, ffn_w_up, ffn_w_down, moe_router, moe_router_b, moe_w_gate, moe_w_up, moe_w_down, ple_gate_w, ple_gate_b, ple_proj, ln2_g, ln2_b):
    b, seq, d = x.shape
    depth = w_in.shape[0]
    t = b * seq
    alpha = (2 * depth) ** 0.25
    tm = 512
    tabs = _rope_tables(positions)
    x2 = x.reshape(t, d)
    for i in range(depth):
        mix, br, qT, keys, cmp_src, valT, gates = _proj(x2, _prep_w_in(w_in[i]), tabs, b, seq, tm)
        pm, sgm = _mixers(mix, pool_w[i], pool_scale[i], sg_ln_g[i], sg_ln_b[i], sg_w[i], sg_b[i], seq, tm)
        cmp, cmpT = _compress(cmp_src, *_prep_compress(cmp_k_pos[i], cmp_k_w1[i], cmp_k_w2[i],
                                                       cmp_v_pos[i], cmp_v_w1[i], cmp_v_w2[i]))
        on = _nsa(qT, cmp, cmpT, keys, valT, gates, seq)
        j = i // 2
        moe = i % 2 == 1
        merged = _merge(pm, sgm, on, br, x2, w_pool_out[i], w_sg_out[i], w_nsa_out[i], w_out[i],
                        ln1_g[i], ln1_b[i], alpha, tm, router=(moe_router[j], moe_router_b[j]) if moe else None)
        if moe:
            x1, route, x1_tiles = merged
            row_tok, row_slot, n_slots, block_e = _route_tables(route, t)
            y = _moe(x1_tiles, row_tok, row_slot, n_slots, block_e, moe_w_gate[j].astype(BF16),
                     moe_w_up[j].astype(BF16), moe_w_down[j].astype(BF16))
        else:
            x1, route = merged, None
            y = _swiglu(x1, ffn_w_gate[j].astype(BF16), ffn_w_up[j].astype(BF16), ffn_w_down[j].astype(BF16), tm)
        x2 = _ple_ln2(x1, p[i].reshape(t, PLE_DIM), y, route, ple_gate_w[i], ple_gate_b[i], ple_proj[i],
                      ln2_g[i], ln2_b[i], alpha, tm)
    return x2.reshape(b, seq, d)
```

```python
import functools

import jax
import jax.numpy as jnp
import numpy as np
from jax import lax
from jax.experimental import pallas as pl
from jax.experimental.pallas import tpu as pltpu

F32 = jnp.float32
BF16 = jnp.bfloat16

D_MODEL = 1024
POOL_WIDTH = 512
POOL_WINDOWS = (2, 4, 8, 16)
POOL_HALO = 16
SG_WIDTH = 512
SG_HEADS = 4
SG_CHUNK = 128
NSA_Q_HEADS = 8
NSA_KV_HEADS = 2
GROUP = NSA_Q_HEADS // NSA_KV_HEADS
HEAD_DIM = 64
Q_WIDTH = NSA_Q_HEADS * HEAD_DIM
KV_WIDTH = NSA_KV_HEADS * HEAD_DIM
ROT_DIM = HEAD_DIM // 4
ROPE_THETA = 500000.0
CMP_LEN = 32
CMP_STRIDE = 16
CMP_HIDDEN = 256
SLC_LEN = 64
SLC_TOP = 16
WINDOW = 512
Q_BLOCK = 128
NEG_INF = -1e30
D_FF = 2816
N_EXPERTS = 8
TOP_K = 2
EXPERT_BLOCK = 512
PLE_DIM = 256
LN_EPS = 1e-5

LANE = 128
TOKEN_TILE = D_MODEL // LANE
PROJ_BLOCK = 512
NSA_STRIP = 64
SLC_BLOCK = 256
SLC_UNROLL = 4
SLC_AHEAD = 2
CMP_TILE = 256
SEL_GROUP = 8
ONES_PAD = 16
PSUM_PAD = 8
LOG2_E = 1.4426950408889634
FF_TILE = 1408
VMEM_LIMIT = 56 * 1024 * 1024


def _params(semantics, vmem=VMEM_LIMIT):
    return pltpu.CompilerParams(dimension_semantics=semantics, vmem_limit_bytes=vmem)


def _dot(a, b):
    return jnp.dot(a, b, preferred_element_type=F32)


def _dot_nt(a, b):
    return lax.dot_general(a, b, (((1,), (1,)), ((), ())), preferred_element_type=F32)


def _store_token_tiles(ref, v):
    n = v.shape[0]
    for s in range(TOKEN_TILE):
        ref[pl.ds(s, n, stride=TOKEN_TILE), :] = v[:, s * LANE:(s + 1) * LANE]


def _load_token_tiles(ref):
    n = ref.shape[0] // TOKEN_TILE
    return jnp.concatenate([ref[pl.ds(s, n, stride=TOKEN_TILE), :] for s in range(TOKEN_TILE)], axis=1)


def _layer_norm_rows(v, g, b):
    mu = jnp.mean(v, axis=-1, keepdims=True)
    c = v - mu
    var = jnp.mean(c * c, axis=-1, keepdims=True)
    return c * lax.rsqrt(var + LN_EPS) * g + b


def _rope_tables(positions):
    half = ROT_DIM // 2
    inv = ROPE_THETA ** (-jnp.arange(half, dtype=F32) / half)
    ang = positions.astype(F32)[:, :, None] * inv
    cos, sin = jnp.cos(ang), jnp.sin(ang)
    rest = HEAD_DIM - ROT_DIM
    one = jnp.ones(ang.shape[:2] + (rest,), F32)
    zero_r = jnp.zeros(ang.shape[:2] + (rest,), F32)
    zero_h = jnp.zeros_like(sin)
    c = jnp.concatenate([cos, cos, one], -1)
    s1 = jnp.concatenate([zero_h, sin, zero_r], -1)
    s2 = jnp.concatenate([-sin, zero_h, zero_r], -1)
    reps = LANE // HEAD_DIM
    lane_tabs = tuple(jnp.tile(t, (1, 1, reps)).reshape(-1, LANE) for t in (c, s1, s2))
    return lane_tabs + (cos.reshape(-1, half).T, sin.reshape(-1, half).T)


def _proj_kernel(x_ref, wrow_ref, wkey_ref, wq_ref, wval_ref, wgate_ref, c_ref, s1_ref, s2_ref, cos_ref, sin_ref,
                 mix_ref, br_ref, qT_ref, keys_ref, cmp_ref, valT_ref, gateT_ref, *, tiles_per_seq):
    tm = x_ref.shape[0]
    half = ROT_DIM // 2
    xb = x_ref[...].astype(BF16)

    for ref, col0 in ((mix_ref, 0), (br_ref, mix_ref.shape[1])):
        for col in range(0, ref.shape[1], PROJ_BLOCK):
            ref[:, col:col + PROJ_BLOCK] = _dot(xb, wrow_ref[:, col0 + col:col0 + col + PROJ_BLOCK]).astype(ref.dtype)

    def rope(v):
        return (v * c_ref[...] + pltpu.roll(v, half, 1) * s1_ref[...]
                + pltpu.roll(v, LANE - half, 1) * s2_ref[...])

    tpos = (pl.program_id(0) % tiles_per_seq) * tm + lax.broadcasted_iota(jnp.int32, (tm, LANE), 0)
    lane = lax.broadcasted_iota(jnp.int32, (tm, LANE), 1)
    onehot = jnp.where(lane == HEAD_DIM + (tpos // SLC_LEN) % SEL_GROUP, 1.0, 0.0)
    for k in range(4):
        acc = _dot(xb, wkey_ref[:, k * 2 * LANE:(k + 1) * 2 * LANE])
        for h in range(NSA_KV_HEADS):
            v = acc[:, h * LANE:(h + 1) * LANE]
            if k < 3:
                v = rope(v)
            if k == 0:
                cmp_ref[0, h] = v
            elif k == 1:
                keys_ref[0, h] = (v + onehot).astype(keys_ref.dtype)
            elif k == 2:
                keys_ref[1, h] = v.astype(keys_ref.dtype)
            else:
                cmp_ref[1, h] = v

    cos, sin = cos_ref[...], sin_ref[...]
    q_scale = HEAD_DIM ** -0.5 * LOG2_E
    for h in range(NSA_KV_HEADS):
        qt = _dot_nt(wq_ref[h], xb)
        for g in range(GROUP):
            blk = qt[g * HEAD_DIM:(g + 1) * HEAD_DIM]
            x1, x2 = blk[:half], blk[half:ROT_DIM]
            y = jnp.concatenate([x1 * cos - x2 * sin, x2 * cos + x1 * sin, blk[ROT_DIM:]], axis=0) * q_scale
            y = y.astype(qT_ref.dtype)
            for nn in range(tm // Q_BLOCK):
                dst = nn * GROUP * Q_BLOCK + g * Q_BLOCK
                qT_ref[h, :, dst:dst + Q_BLOCK] = y[:, nn * Q_BLOCK:(nn + 1) * Q_BLOCK]

    vt = _dot_nt(wval_ref[...], xb)
    ones_pad = jnp.where(lax.broadcasted_iota(jnp.int32, (ONES_PAD, tm), 0) == 0, 1.0, 0.0).astype(valT_ref.dtype)
    for k in range(2):
        for h in range(NSA_KV_HEADS):
            row = (k * NSA_KV_HEADS + h) * HEAD_DIM
            valT_ref[k, h, :HEAD_DIM, :] = vt[row:row + HEAD_DIM].astype(valT_ref.dtype)
            valT_ref[k, h, HEAD_DIM:, :] = ones_pad

    gt = _dot_nt(wgate_ref[...], xb)
    for h in range(NSA_KV_HEADS):
        gateT_ref[h] = gt[h * 16:(h + 1) * 16].astype(gateT_ref.dtype)


def _proj(x2, weights, tabs, b, seq, tm):
    t = x2.shape[0]
    tps = seq // tm
    wrow, wkey, wq, wval, wgate = weights
    h = NSA_KV_HEADS
    n_mix = POOL_WIDTH + 2 * SG_WIDTH
    row = lambda i: (i, 0)
    const = lambda shape: pl.BlockSpec(shape, lambda i: (0,) * len(shape), pipeline_mode=pl.Buffered(1))
    return pl.pallas_call(
        functools.partial(_proj_kernel, tiles_per_seq=tps),
        grid=(t // tm,),
        in_specs=[pl.BlockSpec((tm, D_MODEL), row), const(wrow.shape), const(wkey.shape), const(wq.shape),
                  const(wval.shape), const(wgate.shape),
                  pl.BlockSpec((tm, LANE), row), pl.BlockSpec((tm, LANE), row), pl.BlockSpec((tm, LANE), row),
                  pl.BlockSpec((ROT_DIM // 2, tm), lambda i: (0, i)), pl.BlockSpec((ROT_DIM // 2, tm), lambda i: (0, i))],
        out_specs=[pl.BlockSpec((tm, n_mix), row), pl.BlockSpec((tm, 3 * D_MODEL), row),
                   pl.BlockSpec((None, h, HEAD_DIM, tm * GROUP), lambda i: (i // tps, 0, 0, i % tps)),
                   pl.BlockSpec((2, None, h, tm, LANE), lambda i: (0, i // tps, 0, i % tps, 0)),
                   pl.BlockSpec((2, None, h, tm, LANE), lambda i: (0, i // tps, 0, i % tps, 0)),
                   pl.BlockSpec((2, None, h, HEAD_DIM + ONES_PAD, tm), lambda i: (0, i // tps, 0, 0, i % tps)),
                   pl.BlockSpec((None, h, 16, tm), lambda i: (i // tps, 0, 0, i % tps))],
        out_shape=[jax.ShapeDtypeStruct((t, n_mix), BF16), jax.ShapeDtypeStruct((t, 3 * D_MODEL), BF16),
                   jax.ShapeDtypeStruct((b, h, HEAD_DIM, seq * GROUP), BF16),
                   jax.ShapeDtypeStruct((2, b, h, seq, LANE), BF16),
                   jax.ShapeDtypeStruct((2, b, h, seq, LANE), F32),
                   jax.ShapeDtypeStruct((2, b, h, HEAD_DIM + ONES_PAD, seq), BF16),
                   jax.ShapeDtypeStruct((b, h, 16, seq), BF16)],
        compiler_params=_params(("parallel",)),
        name="proj",
    )(x2, wrow, wkey, wq, wval, wgate, *tabs)


def _prep_w_in(w):
    d = w.shape[0]
    h = NSA_KV_HEADS
    n_mix = POOL_WIDTH + 2 * SG_WIDTH
    kv0 = n_mix + Q_WIDTH
    n_gate = NSA_Q_HEADS * 3
    gate0 = kv0 + 6 * KV_WIDTH
    kv = lambda k: w[:, kv0 + k * KV_WIDTH:kv0 + (k + 1) * KV_WIDTH]
    wrow = jnp.concatenate([w[:, :n_mix], w[:, gate0 + n_gate:]], axis=1)
    pad_heads = lambda m: jnp.pad(m.reshape(d, h, HEAD_DIM), ((0, 0), (0, 0), (0, LANE - HEAD_DIM))).reshape(d, h * LANE)
    wkey = jnp.concatenate([pad_heads(kv(0)), pad_heads(kv(2)), pad_heads(kv(4)), pad_heads(kv(1))], axis=1)
    wq = w[:, n_mix:kv0].T.reshape(h, GROUP * HEAD_DIM, d)
    wval = jnp.concatenate([kv(3), kv(5)], axis=1).T
    wgate = jnp.pad(w[:, gate0:gate0 + n_gate].T.reshape(h, GROUP * 3, d), ((0, 0), (0, 16 - GROUP * 3), (0, 0)))
    return tuple(m.astype(BF16) for m in (wrow, wkey, wq, wval, wgate.reshape(h * 16, d)))


def _mixers_kernel(a_ref, ap_ref, u_ref, v_ref, pw_ref, ps_ref, lg_ref, lb_ref, sw_ref, sb_ref,
                   pm_ref, sg_ref, *, tiles_per_seq):
    i = pl.program_id(0)
    tm = a_ref.shape[0]
    it = i % tiles_per_seq
    gw = POOL_WIDTH // len(POOL_WINDOWS)

    a = a_ref[...].astype(F32)
    prev = jnp.where(it == 0, 0.0, ap_ref[...].astype(F32))
    ext = jnp.concatenate([prev, a], axis=0)
    tpos = (it * tm + lax.broadcasted_iota(jnp.int32, (tm, 1), 0) + 1).astype(F32)
    for g, w in enumerate(POOL_WINDOWS):
        s = ext[:, g * gw:(g + 1) * gw]
        k = 1
        while k < w:
            s = s + pltpu.roll(s, k, 0)
            k *= 2
        d = s[POOL_HALO:] / jnp.minimum(tpos, float(w)) - a[:, g * gw:(g + 1) * gw]
        y = _dot(d.astype(BF16), pw_ref[g]) * ps_ref[:, g * gw:(g + 1) * gw]
        pm_ref[:, g * gw:(g + 1) * gw] = y.astype(pm_ref.dtype)

    u = jax.nn.gelu(u_ref[...].astype(F32))
    v = _layer_norm_rows(jax.nn.gelu(v_ref[...].astype(F32)), lg_ref[...], lb_ref[...]).astype(BF16)
    hc = SG_WIDTH // SG_HEADS
    tri = (lax.broadcasted_iota(jnp.int32, (SG_CHUNK, SG_CHUNK), 0)
           >= lax.broadcasted_iota(jnp.int32, (SG_CHUNK, SG_CHUNK), 1))
    for g in range(SG_HEADS):
        ws = jnp.where(tri, sw_ref[g], 0.0).astype(BF16)
        bias = sb_ref[:, g:g + 1]
        for c in range(tm // SG_CHUNK):
            rows = slice(c * SG_CHUNK, (c + 1) * SG_CHUNK)
            cols = slice(g * hc, (g + 1) * hc)
            mixed = _dot(ws, v[rows, cols]) + bias
            sg_ref[rows, cols] = (u[rows, cols] * mixed).astype(sg_ref.dtype)


def _mixers(mix, pool_w, pool_scale, ln_g, ln_b, sg_w, sg_b, seq, tm):
    t = mix.shape[0]
    halo_blocks = tm // POOL_HALO
    full = lambda shape: pl.BlockSpec(shape, lambda i: (0,) * len(shape))
    return pl.pallas_call(
        functools.partial(_mixers_kernel, tiles_per_seq=seq // tm),
        grid=(t // tm,),
        in_specs=[pl.BlockSpec((tm, POOL_WIDTH), lambda i: (i, 0)),
                  pl.BlockSpec((POOL_HALO, POOL_WIDTH), lambda i: (jnp.maximum(i * halo_blocks - 1, 0), 0)),
                  pl.BlockSpec((tm, SG_WIDTH), lambda i: (i, 1)),
                  pl.BlockSpec((tm, SG_WIDTH), lambda i: (i, 2)),
                  full(pool_w.shape), full((1, POOL_WIDTH)), full((1, SG_WIDTH)), full((1, SG_WIDTH)),
                  full(sg_w.shape), full((SG_CHUNK, SG_HEADS))],
        out_specs=[pl.BlockSpec((tm, POOL_WIDTH), lambda i: (i, 0)),
                   pl.BlockSpec((tm, SG_WIDTH), lambda i: (i, 0))],
        out_shape=[jax.ShapeDtypeStruct((t, POOL_WIDTH), BF16), jax.ShapeDtypeStruct((t, SG_WIDTH), BF16)],
        compiler_params=_params(("parallel",)),
        name="mixers",
    )(mix, mix, mix, mix, pool_w.astype(BF16), pool_scale[None], ln_g[None], ln_b[None], sg_w, sg_b.T)


def _compress_kernel(src_ref, pos_ref, w1_ref, w2_ref, w2T_ref, o_ref, oT_ref):
    nc = o_ref.shape[0]
    first = jnp.zeros((nc, CMP_HIDDEN), F32)
    second = jnp.zeros((nc, CMP_HIDDEN), F32)
    for p in range(CMP_STRIDE):
        rows = src_ref[pl.ds(p, nc, stride=CMP_STRIDE), :]
        first = first + _dot((rows + pos_ref[p:p + 1, :]).astype(BF16), w1_ref[p])
        q = CMP_STRIDE + p
        second = second + _dot((rows + pos_ref[q:q + 1, :]).astype(BF16), w1_ref[q])
    pre = first + pltpu.roll(second, nc - 1, 0)
    hidden = jax.nn.gelu(pre).astype(BF16)
    o_ref[...] = _dot(hidden, w2_ref[...]).astype(o_ref.dtype)
    oT_ref[...] = _dot_nt(w2T_ref[...], hidden).astype(oT_ref.dtype)


def _compress(src, pos, w1, w2, w2T):
    _, b, h, seq, _ = src.shape
    nc = seq // CMP_STRIDE
    sel = lambda n: pl.BlockSpec((None,) + n, lambda s, bi, hi: (s,) + (0,) * len(n))
    return pl.pallas_call(
        _compress_kernel,
        grid=(2, b, h),
        in_specs=[pl.BlockSpec((None, None, None, seq, LANE), lambda s, bi, hi: (s, bi, hi, 0, 0)),
                  sel(pos.shape[1:]), sel(w1.shape[1:]), sel(w2.shape[1:]), sel(w2T.shape[1:])],
        out_specs=[pl.BlockSpec((None, None, None, nc, HEAD_DIM), lambda s, bi, hi: (s, bi, hi, 0, 0)),
                   pl.BlockSpec((None, None, None, HEAD_DIM, nc), lambda s, bi, hi: (s, bi, hi, 0, 0))],
        out_shape=[jax.ShapeDtypeStruct((2, b, h, nc, HEAD_DIM), BF16),
                   jax.ShapeDtypeStruct((2, b, h, HEAD_DIM, nc), BF16)],
        compiler_params=_params(("parallel", "parallel", "parallel")),
        name="compress",
    )(src, pos, w1, w2, w2T)


def _prep_compress(k_pos, k_w1, k_w2, v_pos, v_w1, v_w2):
    pad = LANE - HEAD_DIM
    pos = jnp.pad(jnp.stack([k_pos, v_pos]), ((0, 0), (0, 0), (0, pad)))
    w1 = jnp.stack([k_w1, v_w1]).reshape(2, CMP_LEN, HEAD_DIM, CMP_HIDDEN)
    w1 = jnp.pad(w1, ((0, 0), (0, 0), (0, pad), (0, 0))).astype(BF16)
    w2 = jnp.stack([k_w2, v_w2]).astype(BF16)
    return pos, w1, w2, w2.transpose(0, 2, 1)


def _score_strips(k_ref, k0, n_strips, q_aug, s_ref, mask):
    mx = None
    for r in range(n_strips):
        start = pl.multiple_of(k0 + r * NSA_STRIP, NSA_STRIP)
        s = _dot(k_ref[pl.ds(start, NSA_STRIP), :], q_aug)
        if mask is not None:
            s = mask(r, start, s)
        s_ref[r * NSA_STRIP:(r + 1) * NSA_STRIP, :] = s
        mx = s if mx is None else jnp.maximum(mx, s)
    return jnp.max(mx, axis=0, keepdims=True)


def _prob_strips(n_strips, m, s_ref, p_ref):
    for r in range(n_strips):
        rows = slice(r * NSA_STRIP, (r + 1) * NSA_STRIP)
        p_ref[rows, :] = jnp.exp2(s_ref[rows, :] - m).astype(BF16)


def _nsa_kernel(qT_ref, kc_ref, vcT_ref, ks_ref, vsT_ref, kw_ref, vwT_ref, g_ref, bound_ref, o_ref,
                s_ref, p_ref, pc_ref, s2_ref, p2_ref, psum_ref, selneg_ref, *, n_top):
    n = pl.program_id(2)
    qs = n * Q_BLOCK
    width = GROUP * Q_BLOCK
    qT = qT_ref[...]
    t_row = qs + lax.broadcasted_iota(jnp.int32, (1, width), 1) % Q_BLOCK
    nc = kc_ref.shape[0]
    ns = selneg_ref.shape[0]
    strip_iota = lax.broadcasted_iota(jnp.int32, (NSA_STRIP, 1), 0)

    @pl.when(n == 0)
    def _():
        psum_ref[...] = jnp.zeros(psum_ref.shape, F32)
        pc_ref[...] = jnp.zeros(pc_ref.shape, BF16)

    n_ct = (n * (Q_BLOCK // CMP_STRIDE) + (Q_BLOCK - CMP_LEN) // CMP_STRIDE + 1 + CMP_TILE - 1) // CMP_TILE
    strips_per_tile = CMP_TILE // NSA_STRIP

    def cmp_scores(jt, mx):
        for r in range(strips_per_tile):
            start = pl.multiple_of(jt * CMP_TILE + r * NSA_STRIP, NSA_STRIP)
            s = _dot(kc_ref[pl.ds(start, NSA_STRIP), :], qT)
            c_end = (start + strip_iota) * CMP_STRIDE + (CMP_LEN - 1)
            s = jnp.where(c_end <= t_row, s, NEG_INF)
            s_ref[pl.ds(start, NSA_STRIP), :] = s
            mx = jnp.maximum(mx, s)
        return mx

    mx = lax.fori_loop(0, n_ct, cmp_scores, jnp.full((NSA_STRIP, width), NEG_INF, F32))
    m_cmp = jnp.maximum(jnp.max(mx, axis=0, keepdims=True), 0.1 * NEG_INF)

    def cmp_probs(jt, lsum):
        for r in range(strips_per_tile):
            rows = pl.ds(pl.multiple_of(jt * CMP_TILE + r * NSA_STRIP, NSA_STRIP), NSA_STRIP)
            e = jnp.exp2(s_ref[rows, :] - m_cmp)
            s_ref[rows, :] = e
            pc_ref[rows, :] = e.astype(BF16)
            lsum = lsum + e
        return lsum

    lsum = lax.fori_loop(0, n_ct, cmp_probs, jnp.zeros((NSA_STRIP, width), F32))
    l = jnp.sum(lsum, axis=0, keepdims=True)
    inv = jnp.where(l > 0.0, 1.0 / l, 0.0)
    o_cmp = _dot(vcT_ref[...], pc_ref[...]) * inv

    def cmp_group_sum(jt, c):
        for r in range(strips_per_tile):
            start = pl.multiple_of(jt * CMP_TILE + r * NSA_STRIP, NSA_STRIP)
            p = s_ref[pl.ds(start, NSA_STRIP), :] * inv
            psum = p[:, :Q_BLOCK]
            for g in range(1, GROUP):
                psum = psum + p[:, g * Q_BLOCK:(g + 1) * Q_BLOCK]
            psum_ref[pl.ds(pl.multiple_of(PSUM_PAD + start, 8), NSA_STRIP), :] = psum
        return c

    lax.fori_loop(0, n_ct, cmp_group_sum, 0)

    per = SLC_LEN // CMP_STRIDE
    part = [psum_ref[pl.ds(PSUM_PAD + k, ns, stride=per), :] for k in range(per)]
    before = psum_ref[pl.ds(PSUM_PAD - 1, ns, stride=per), :]
    imp = 2.0 * (part[0] + part[1] + part[2]) + part[3] + before

    blk = lax.broadcasted_iota(jnp.int32, (ns, Q_BLOCK), 0)
    cur = (qs + lax.broadcasted_iota(jnp.int32, (ns, Q_BLOCK), 1)) // SLC_LEN
    taken = -2.0
    score = jnp.where(blk <= cur, imp, -1.0)
    score = jnp.where(blk == 0, taken, jnp.where(blk == cur, taken, jnp.where(blk == cur - 1, taken, score)))
    blk_f = blk.astype(F32)
    for _ in range(n_top - 3):
        top = jnp.max(score, axis=0, keepdims=True)
        cand = jnp.where(score == top, blk_f, float(ns))
        first = jnp.min(cand, axis=0, keepdims=True)
        score = jnp.where(cand == first, taken, score)
    selneg_ref[...] = jnp.where(score == taken, 0.0, NEG_INF)

    strips = SLC_BLOCK // NSA_STRIP
    sel_rows = ks_ref.shape[1] - HEAD_DIM - SEL_GROUP
    last_block = ks_ref.shape[0] // SLC_BLOCK - 1
    n_diag = n // (SLC_BLOCK // Q_BLOCK)

    def past_only(r, start, s):
        return jnp.where(start + strip_iota <= t_row, s, NEG_INF)

    def slc_scores(kb, slot):
        variant = jnp.where(kb < n_diag, 0, jnp.where(kb > n_diag, 3, 1 + n % (SLC_BLOCK // Q_BLOCK)))

        def limit(r, start, s):
            return jnp.minimum(s, bound_ref[variant, r * NSA_STRIP:(r + 1) * NSA_STRIP, :])

        kb = jnp.minimum(kb, last_block)
        k0 = pl.multiple_of(kb * SLC_BLOCK, SLC_BLOCK)
        g0 = pl.multiple_of((kb * strips // SEL_GROUP) * SEL_GROUP, SEL_GROUP)
        bias = jnp.concatenate([selneg_ref[pl.ds(g0, SEL_GROUP), :]] * GROUP, axis=1)
        bias = jnp.concatenate([bias, jnp.zeros((sel_rows, width), F32)], axis=0).astype(BF16)
        q_aug = jnp.concatenate([qT, bias], axis=0)
        return _score_strips(ks_ref, k0, strips, q_aug, s2_ref.at[slot], limit)

    def slc_consume(kb, slot, m, acc, mx):
        k0 = pl.multiple_of(kb * SLC_BLOCK, SLC_BLOCK)
        m_new = jnp.maximum(m, mx)
        _prob_strips(strips, m_new, s2_ref.at[slot], p2_ref.at[slot])
        pv = _dot(vsT_ref[:, pl.ds(k0, SLC_BLOCK)], p2_ref[slot])
        return m_new, jnp.exp2(m - m_new) * acc + pv

    def slc_group(j, carry):
        m, acc, pending = carry[0], carry[1], list(carry[2:])
        for u in range(SLC_UNROLL):
            kb = SLC_UNROLL * j + u
            pending.append(slc_scores(kb + SLC_AHEAD, (u + SLC_AHEAD) % SLC_UNROLL))
            m, acc = slc_consume(kb, u, m, acc, pending.pop(0))
        return (m, acc, *pending)

    init = (jnp.full((1, width), NEG_INF, F32), jnp.zeros((vsT_ref.shape[0], width), F32),
            *[slc_scores(a, a) for a in range(SLC_AHEAD)])
    acc = lax.fori_loop(0, (n_diag + SLC_UNROLL) // SLC_UNROLL, slc_group, init)[1]
    o_slc = acc[:HEAD_DIM] / acc[HEAD_DIM:HEAD_DIM + 1]

    span = WINDOW + Q_BLOCK
    w_strips = span // NSA_STRIP
    edge = Q_BLOCK // NSA_STRIP
    w0 = pl.multiple_of(jnp.maximum(qs - WINDOW, 0), Q_BLOCK)

    def band_edges(r, start, s):
        if r < edge:
            return jnp.where(t_row - (start + strip_iota) < WINDOW, s, NEG_INF)
        if r >= w_strips - edge:
            return past_only(r, start, s)
        return s

    q_win = jnp.concatenate([qT, jnp.zeros((kw_ref.shape[1] - HEAD_DIM, width), BF16)], axis=0)
    m_win = lax.cond(n >= WINDOW // Q_BLOCK,
                     lambda: _score_strips(kw_ref, w0, w_strips, q_win, s_ref, band_edges),
                     lambda: _score_strips(kw_ref, w0, w_strips, q_win, s_ref, past_only))
    _prob_strips(w_strips, m_win, s_ref, p_ref)
    pv = _dot(vwT_ref[:, pl.ds(w0, span)], p_ref[:span, :])
    o_win = pv[:HEAD_DIM] / pv[HEAD_DIM:HEAD_DIM + 1]

    gates = jax.nn.sigmoid(g_ref[...].astype(F32))

    def gate(c):
        return jnp.concatenate([gates[g * 3 + c:g * 3 + c + 1, :] for g in range(GROUP)], axis=1)

    o = gate(0) * o_cmp + gate(1) * o_slc + gate(2) * o_win
    for g in range(GROUP):
        o_ref[:, g * HEAD_DIM:(g + 1) * HEAD_DIM] = o[:, g * Q_BLOCK:(g + 1) * Q_BLOCK].T.astype(o_ref.dtype)


def _causal_bounds():
    big = np.float32(3.0e38)
    key = np.arange(SLC_BLOCK)[:, None]
    query = np.tile(np.arange(Q_BLOCK), GROUP)[None, :]
    tables = [np.full((SLC_BLOCK, GROUP * Q_BLOCK), big, np.float32)]
    for v in range(SLC_BLOCK // Q_BLOCK):
        tables.append(np.where(key <= query + v * Q_BLOCK, big, np.float32(NEG_INF)).astype(np.float32))
    tables.append(np.full((SLC_BLOCK, GROUP * Q_BLOCK), NEG_INF, np.float32))
    return jnp.asarray(np.stack(tables))


def _nsa(qT, cmp, cmpT, keys, valT, gates, seq):
    b, h = qT.shape[:2]
    nq = seq // Q_BLOCK
    nc = cmp.shape[3]
    ns = seq // SLC_LEN
    width = GROUP * Q_BLOCK
    assert nc == ns * (SLC_LEN // CMP_STRIDE) and nc % CMP_TILE == 0 and seq % (SLC_BLOCK * SLC_UNROLL) == 0
    assert seq >= WINDOW + Q_BLOCK and ns >= SLC_TOP
    bounds = _causal_bounds()
    per_head = lambda arr, k: pl.BlockSpec((None, None, None) + arr.shape[3:], lambda bi, hi, n, k=k: (k, bi, hi, 0, 0))
    return pl.pallas_call(
        functools.partial(_nsa_kernel, n_top=SLC_TOP),
        grid=(b, h, nq),
        in_specs=[pl.BlockSpec((None, None, HEAD_DIM, width), lambda bi, hi, n: (bi, hi, 0, n)),
                  per_head(cmp, 0), per_head(cmpT, 1),
                  per_head(keys, 0), per_head(valT, 0), per_head(keys, 1), per_head(valT, 1),
                  pl.BlockSpec((None, None, 16, Q_BLOCK), lambda bi, hi, n: (bi, hi, 0, n)),
                  pl.BlockSpec(bounds.shape, lambda bi, hi, n: (0, 0, 0))],
        out_specs=pl.BlockSpec((Q_BLOCK, GROUP * HEAD_DIM), lambda bi, hi, n: (bi * nq + n, hi)),
        out_shape=jax.ShapeDtypeStruct((b * seq, Q_WIDTH), BF16),
        scratch_shapes=[pltpu.VMEM((max(nc, WINDOW + Q_BLOCK), width), F32),
                        pltpu.VMEM((WINDOW + Q_BLOCK, width), BF16),
                        pltpu.VMEM((nc, width), BF16),
                        pltpu.VMEM((SLC_UNROLL, SLC_BLOCK, width), F32),
                        pltpu.VMEM((SLC_UNROLL, SLC_BLOCK, width), BF16),
                        pltpu.VMEM((PSUM_PAD + nc, Q_BLOCK), F32),
                        pltpu.VMEM((ns, Q_BLOCK), F32)],
        compiler_params=_params(("parallel", "parallel", "arbitrary")),
        name="nsa",
    )(qT, cmp, cmpT, keys, valT, keys, valT, gates, bounds)


def _merge_kernel(*refs, with_router, alpha):
    (pm_ref, sg_ref, on_ref, gp_ref, gs_ref, gn_ref, x_ref, wp_ref, ws_ref, wn_ref, wo_ref, lg_ref, lb_ref) = refs[:13]
    if with_router:
        wr_ref, br_ref, x1_ref, route_ref, x1t_ref = refs[13:]
    else:
        (x1_ref,) = refs[13:]
    sig = lambda r: jax.nn.sigmoid(r[...].astype(F32))
    y = (sig(gp_ref) * _dot(pm_ref[...], wp_ref[...]) + sig(gs_ref) * _dot(sg_ref[...], ws_ref[...])
         + sig(gn_ref) * _dot(on_ref[...], wn_ref[...]))
    hmix = _dot(y.astype(BF16), wo_ref[...])
    x1 = _layer_norm_rows(alpha * x_ref[...] + hmix, lg_ref[...], lb_ref[...])
    x1_ref[...] = x1
    if with_router:
        _store_token_tiles(x1t_ref, x1)
        wr = wr_ref[...]
        w_hi = wr.astype(BF16)
        w_lo = (wr - w_hi.astype(F32)).astype(BF16)
        x_hi = x1.astype(BF16)
        x_lo = (x1 - x_hi.astype(F32)).astype(BF16)
        logits = _dot(x_hi, w_hi) + (_dot(x_lo, w_hi) + _dot(x_hi, w_lo)) + br_ref[...]
        lane = lax.broadcasted_iota(jnp.int32, logits.shape, 1)
        lane_f = lane.astype(F32)
        logits = jnp.where(lane < N_EXPERTS, logits, -jnp.inf)
        m1 = jnp.max(logits, axis=-1, keepdims=True)
        i1 = jnp.min(jnp.where(logits == m1, lane_f, float(LANE)), axis=-1, keepdims=True)
        rest = jnp.where(lane_f == i1, -jnp.inf, logits)
        m2 = jnp.max(rest, axis=-1, keepdims=True)
        i2 = jnp.min(jnp.where(rest == m2, lane_f, float(LANE)), axis=-1, keepdims=True)
        e2 = jnp.exp(m2 - m1)
        w1 = 1.0 / (1.0 + e2)
        route_ref[...] = jnp.where(lane == 0, i1, jnp.where(lane == 1, i2, jnp.where(lane == 2, w1,
                                   jnp.where(lane == 3, e2 * w1, 0.0))))


def _merge(pm, sgm, on, br, x2, wp, ws, wn, wo, lg, lb, alpha, tm, router=None):
    t = x2.shape[0]
    row = lambda width, cb=0: pl.BlockSpec((tm, width), lambda i, cb=cb: (i, cb))
    full = lambda shape: pl.BlockSpec(shape, lambda i: (0,) * len(shape))
    in_specs = [row(POOL_WIDTH), row(SG_WIDTH), row(Q_WIDTH),
                row(D_MODEL, 0), row(D_MODEL, 1), row(D_MODEL, 2),
                row(D_MODEL), full(wp.shape), full(ws.shape), full(wn.shape), full(wo.shape),
                full((1, D_MODEL)), full((1, D_MODEL))]
    args = [pm, sgm, on, br, br, br, x2, wp.astype(BF16), ws.astype(BF16), wn.astype(BF16), wo.astype(BF16),
            lg[None], lb[None]]
    out_specs = [row(D_MODEL)]
    out_shape = [jax.ShapeDtypeStruct((t, D_MODEL), F32)]
    if router is not None:
        w_router, b_router = router
        pad = LANE - N_EXPERTS
        in_specs += [full((D_MODEL, LANE)), full((1, LANE))]
        args += [jnp.pad(w_router, ((0, 0), (0, pad))), jnp.pad(b_router, (0, pad))[None]]
        out_specs += [row(LANE), pl.BlockSpec((tm * TOKEN_TILE, LANE), lambda i: (i, 0))]
        out_shape += [jax.ShapeDtypeStruct((t, LANE), F32), jax.ShapeDtypeStruct((t * TOKEN_TILE, LANE), F32)]
    out = pl.pallas_call(
        functools.partial(_merge_kernel, with_router=router is not None, alpha=alpha),
        grid=(t // tm,), in_specs=in_specs, out_specs=out_specs, out_shape=out_shape,
        compiler_params=_params(("parallel",)),
        name="merge",
    )(*args)
    return out if router is not None else out[0]


def _tile_rows(i):
    start = i * TOKEN_TILE if isinstance(i, int) else pl.multiple_of(i * TOKEN_TILE, TOKEN_TILE)
    return pl.ds(start, TOKEN_TILE)


def _for_rows(n, fn):
    def body(r, c):
        fn(r)
        return c
    lax.fori_loop(0, n, body, 0, unroll=8)


def _moe_kernel(be_ref, tok_ref, tok_next_ref, slot_ref, x_hbm, wg_ref, wu_ref, wd_ref, y_hbm,
                xg_ref, xb_ref, acc_ref, yt_ref, gather_sem, scatter_sem):
    i = pl.program_id(0)
    k = pl.program_id(1)
    n_blocks = pl.num_programs(0)
    rows = xb_ref.shape[0]

    def gather(idx_ref, slot):
        return lambda r: pltpu.make_async_copy(x_hbm.at[_tile_rows(idx_ref[0, r])], xg_ref.at[slot, _tile_rows(r)],
                                               gather_sem.at[slot])

    def scatter(idx_ref):
        return lambda r: pltpu.make_async_copy(yt_ref.at[_tile_rows(r)], y_hbm.at[_tile_rows(idx_ref[0, r])],
                                               scatter_sem)

    @pl.when(k == 0)
    def _():
        slot = i % 2

        @pl.when(i == 0)
        def _():
            _for_rows(rows, lambda r: gather(tok_ref, 0)(r).start())

        pltpu.make_async_copy(x_hbm.at[pl.ds(0, rows * TOKEN_TILE)], xg_ref.at[slot], gather_sem.at[slot]).wait()

        @pl.when(i + 1 < n_blocks)
        def _():
            _for_rows(rows, lambda r: gather(tok_next_ref, 1 - slot)(r).start())

        for s in range(TOKEN_TILE):
            xb_ref[:, s * LANE:(s + 1) * LANE] = xg_ref[slot, pl.ds(s, rows, stride=TOKEN_TILE), :].astype(BF16)

    xb = xb_ref[...]
    hidden = (jax.nn.silu(_dot(xb, wg_ref[...])) * _dot(xb, wu_ref[...])).astype(BF16)
    part = _dot(hidden, wd_ref[...])

    @pl.when(k == 0)
    def _():
        acc_ref[...] = part

    @pl.when(k == pl.num_programs(1) - 1)
    def _():
        def wait_scatter():
            pltpu.make_async_copy(yt_ref, y_hbm.at[pl.ds(0, rows * TOKEN_TILE)], scatter_sem).wait()

        pl.when(i > 0)(wait_scatter)
        _store_token_tiles(yt_ref, acc_ref[...] + part)
        _for_rows(rows, lambda r: scatter(slot_ref)(r).start())
        pl.when(i == n_blocks - 1)(wait_scatter)


def _moe(x_tiles, row_tok, row_slot, n_slots, block_e, wg, wu, wd):
    rows = EXPERT_BLOCK
    n_blocks = row_tok.shape[0] // rows
    kf = D_FF // FF_TILE
    assert kf == 2
    idx_spec = lambda fn: pl.BlockSpec((None, 1, rows), lambda i, k, be: (fn(i), 0, 0), memory_space=pltpu.SMEM)
    tok3 = row_tok.reshape(n_blocks, 1, rows)
    slot3 = row_slot.reshape(n_blocks, 1, rows)
    return pl.pallas_call(
        _moe_kernel,
        grid_spec=pltpu.PrefetchScalarGridSpec(
            num_scalar_prefetch=1,
            grid=(n_blocks, kf),
            in_specs=[idx_spec(lambda i: i), idx_spec(lambda i: jnp.minimum(i + 1, n_blocks - 1)),
                      idx_spec(lambda i: i),
                      pl.BlockSpec(memory_space=pl.ANY),
                      pl.BlockSpec((None, D_MODEL, FF_TILE), lambda i, k, be: (be[i], 0, k)),
                      pl.BlockSpec((None, D_MODEL, FF_TILE), lambda i, k, be: (be[i], 0, k)),
                      pl.BlockSpec((None, FF_TILE, D_MODEL), lambda i, k, be: (be[i], k, 0))],
            out_specs=pl.BlockSpec(memory_space=pl.ANY),
            scratch_shapes=[pltpu.VMEM((2, rows * TOKEN_TILE, LANE), F32), pltpu.VMEM((rows, D_MODEL), BF16),
                            pltpu.VMEM((rows, D_MODEL), F32), pltpu.VMEM((rows * TOKEN_TILE, LANE), F32),
                            pltpu.SemaphoreType.DMA((2,)), pltpu.SemaphoreType.DMA(())]),
        out_shape=jax.ShapeDtypeStruct((n_slots * TOKEN_TILE, LANE), F32),
        compiler_params=_params(("arbitrary", "arbitrary")),
        name="moe",
    )(block_e, tok3, tok3, slot3, x_tiles, wg, wu, wd)


def _swiglu_kernel(x_ref, wg_ref, wu_ref, wd_ref, o_ref, xb_ref):
    k = pl.program_id(1)

    @pl.when(k == 0)
    def _():
        xb_ref[...] = x_ref[...].astype(BF16)

    xb = xb_ref[...]
    hidden = (jax.nn.silu(_dot(xb, wg_ref[...])) * _dot(xb, wu_ref[...])).astype(BF16)
    part = _dot(hidden, wd_ref[...])

    @pl.when(k == 0)
    def _():
        o_ref[...] = part

    @pl.when(k != 0)
    def _():
        o_ref[...] += part


def _swiglu(rows, wg, wu, wd, tm):
    r = rows.shape[0]
    row = pl.BlockSpec((tm, D_MODEL), lambda i, k: (i, 0))
    return pl.pallas_call(
        _swiglu_kernel,
        grid=(r // tm, D_FF // FF_TILE),
        in_specs=[row,
                  pl.BlockSpec((D_MODEL, FF_TILE), lambda i, k: (0, k)),
                  pl.BlockSpec((D_MODEL, FF_TILE), lambda i, k: (0, k)),
                  pl.BlockSpec((FF_TILE, D_MODEL), lambda i, k: (k, 0))],
        out_specs=row,
        out_shape=jax.ShapeDtypeStruct(rows.shape, F32),
        scratch_shapes=[pltpu.VMEM((tm, D_MODEL), BF16)],
        compiler_params=_params(("parallel", "arbitrary")),
        name="swiglu",
    )(rows, wg, wu, wd)


def _ple_ln2_kernel(*refs, n_parts, alpha):
    x_ref, p_ref = refs[0], refs[1]
    parts = refs[2:2 + n_parts]
    rest = refs[2 + n_parts:]
    if n_parts > 1:
        route_ref, rest = rest[0], rest[1:]
    wg_ref, bg_ref, wp_ref, lg_ref, lb_ref, o_ref = rest
    x1 = x_ref[...]
    if n_parts > 1:
        route = route_ref[...]
        f = _load_token_tiles(parts[0]) * route[:, TOP_K:TOP_K + 1]
        for k in range(1, n_parts):
            f = f + _load_token_tiles(parts[k]) * route[:, TOP_K + k:TOP_K + k + 1]
    else:
        f = parts[0][...]
    gate = jax.nn.sigmoid(_dot(x1.astype(BF16), wg_ref[...]) + bg_ref[...])
    ple = gate * _dot(p_ref[...].astype(BF16), wp_ref[...])
    o_ref[...] = _layer_norm_rows(alpha * x1 + f + ple, lg_ref[...], lb_ref[...])


def _ple_ln2(x1, p2, y, route, wg, bg, wp, lg, lb, alpha, tm):
    t = x1.shape[0]
    n_parts = 1 if route is None else TOP_K
    steps = t // tm
    row = lambda width: pl.BlockSpec((tm, width), lambda i: (i, 0))
    full = lambda shape: pl.BlockSpec(shape, lambda i: (0,) * len(shape))
    in_specs = [row(D_MODEL), row(PLE_DIM)]
    if n_parts > 1:
        in_specs += [pl.BlockSpec((tm * TOKEN_TILE, LANE), lambda i, k=k: (k * steps + i, 0)) for k in range(n_parts)]
    else:
        in_specs.append(row(D_MODEL))
    args = [x1, p2] + [y] * n_parts
    if n_parts > 1:
        in_specs.append(row(LANE))
        args.append(route)
    in_specs += [full((D_MODEL, D_MODEL)), full((1, D_MODEL)), full((PLE_DIM, D_MODEL)),
                 full((1, D_MODEL)), full((1, D_MODEL))]
    args += [wg.astype(BF16), bg[None], wp.astype(BF16), lg[None], lb[None]]
    return pl.pallas_call(
        functools.partial(_ple_ln2_kernel, n_parts=n_parts, alpha=alpha),
        grid=(steps,), in_specs=in_specs, out_specs=row(D_MODEL),
        out_shape=jax.ShapeDtypeStruct((t, D_MODEL), F32),
        compiler_params=_params(("parallel",)),
        name="ple_ln2",
    )(*args)


def _route_tables(route, t):
    flat_e = route[:, :TOP_K].astype(jnp.int32).reshape(-1)
    tk = flat_e.shape[0]
    onehot = (flat_e[:, None] == jnp.arange(N_EXPERTS, dtype=jnp.int32)[None, :]).astype(jnp.int32)
    before = jnp.cumsum(onehot, axis=0) - onehot
    rank = jnp.sum(before * onehot, axis=1)
    counts = jnp.sum(onehot, axis=0)
    padded = (counts + EXPERT_BLOCK - 1) // EXPERT_BLOCK * EXPERT_BLOCK
    pend = jnp.cumsum(padded)
    dest = (pend - padded)[flat_e] + rank
    n_blocks = -(-tk // EXPERT_BLOCK) + N_EXPERTS
    n_rows = n_blocks * EXPERT_BLOCK
    row_pair = jnp.full((n_rows,), -1, jnp.int32).at[dest].set(jnp.arange(tk, dtype=jnp.int32))
    is_pad = (row_pair < 0).astype(jnp.int32)
    row_tok = jnp.where(row_pair < 0, 0, row_pair // TOP_K)
    pad_slot = tk + jnp.cumsum(is_pad) - is_pad
    row_slot = jnp.where(row_pair < 0, pad_slot, (row_pair % TOP_K) * t + row_pair // TOP_K).astype(jnp.int32)
    block_start = jnp.arange(n_blocks, dtype=jnp.int32) * EXPERT_BLOCK
    block_e = jnp.sum((pend[None, :] <= block_start[:, None]).astype(jnp.int32), axis=1)
    block_e = jnp.minimum(block_e, N_EXPERTS - 1).astype(jnp.int32)
    return row_tok, row_slot, n_rows, block_e


def kernel(x, p, positions, w_in, pool_w, pool_scale, sg_ln_g, sg_ln_b, sg_w, sg_b, cmp_k_pos, cmp_k_w1, cmp_k_w2, cmp_v_pos, cmp_v_w1, cmp_v_w2, w_pool_out, w_sg_out, w_nsa_out, w_out, ln1_g, ln1_b, ffn_w_gate, ffn_w_up, ffn_w_down, moe_router, moe_router_b, moe_w_gate, moe_w_up, moe_w_down, ple_gate_w, ple_gate_b, ple_proj, ln2_g, ln2_b):
    b, seq, d = x.shape
    depth = w_in.shape[0]
    t = b * seq
    alpha = (2 * depth) ** 0.25
    tm = 512
    tabs = _rope_tables(positions)
    x2 = x.reshape(t, d)
    for i in range(depth):
        mix, br, qT, keys, cmp_src, valT, gates = _proj(x2, _prep_w_in(w_in[i]), tabs, b, seq, tm)
        pm, sgm = _mixers(mix, pool_w[i], pool_scale[i], sg_ln_g[i], sg_ln_b[i], sg_w[i], sg_b[i], seq, tm)
        cmp, cmpT = _compress(cmp_src, *_prep_compress(cmp_k_pos[i], cmp_k_w1[i], cmp_k_w2[i],
                                                       cmp_v_pos[i], cmp_v_w1[i], cmp_v_w2[i]))
        on = _nsa(qT, cmp, cmpT, keys, valT, gates, seq)
        j = i // 2
        moe = i % 2 == 1
        merged = _merge(pm, sgm, on, br, x2, w_pool_out[i], w_sg_out[i], w_nsa_out[i], w_out[i],
                        ln1_g[i], ln1_b[i], alpha, tm, router=(moe_router[j], moe_router_b[j]) if moe else None)
        if moe:
            x1, route, x1_tiles = merged
            row_tok, row_slot, n_slots, block_e = _route_tables(route, t)
            y = _moe(x1_tiles, row_tok, row_slot, n_slots, block_e, moe_w_gate[j].astype(BF16),
                     moe_w_up[j].astype(BF16), moe_w_down[j].astype(BF16))
        else:
            x1, route = merged, None
            y = _swiglu(x1, ffn_w_gate[j].astype(BF16), ffn_w_up[j].astype(BF16), ffn_w_down[j].astype(BF16), tm)
        x2 = _ple_ln2(x1, p[i].reshape(t, PLE_DIM), y, route, ple_gate_w[i], ple_gate_b[i], ple_proj[i],
                      ln2_g[i], ln2_b[i], alpha, tm)
    return x2.reshape(b, seq, d)
```

```python
import functools

import jax
import jax.numpy as jnp
import numpy as np
from jax import lax
from jax.experimental import pallas as pl
from jax.experimental.pallas import tpu as pltpu

F32 = jnp.float32
BF16 = jnp.bfloat16

D_MODEL = 1024
POOL_WIDTH = 512
POOL_WINDOWS = (2, 4, 8, 16)
POOL_HALO = 16
SG_WIDTH = 512
SG_HEADS = 4
SG_CHUNK = 128
NSA_Q_HEADS = 8
NSA_KV_HEADS = 2
GROUP = NSA_Q_HEADS // NSA_KV_HEADS
HEAD_DIM = 64
Q_WIDTH = NSA_Q_HEADS * HEAD_DIM
KV_WIDTH = NSA_KV_HEADS * HEAD_DIM
ROT_DIM = HEAD_DIM // 4
ROPE_THETA = 500000.0
CMP_LEN = 32
CMP_STRIDE = 16
CMP_HIDDEN = 256
SLC_LEN = 64
SLC_TOP = 16
WINDOW = 512
Q_BLOCK = 128
NEG_INF = -1e30
D_FF = 2816
N_EXPERTS = 8
TOP_K = 2
EXPERT_BLOCK = 512
PLE_DIM = 256
LN_EPS = 1e-5

LANE = 128
TOKEN_TILE = D_MODEL // LANE
PROJ_BLOCK = 512
NSA_STRIP = 64
SLC_BLOCK = 256
SLC_UNROLL = 4
SLC_AHEAD = 2
CMP_TILE = 256
SEL_GROUP = 8
ONES_PAD = 16
PSUM_PAD = 8
LOG2_E = 1.4426950408889634
FF_TILE = 1408
VMEM_LIMIT = 56 * 1024 * 1024


def _params(semantics, vmem=VMEM_LIMIT):
    return pltpu.CompilerParams(dimension_semantics=semantics, vmem_limit_bytes=vmem)


def _dot(a, b):
    return jnp.dot(a, b, preferred_element_type=F32)


def _dot_nt(a, b):
    return lax.dot_general(a, b, (((1,), (1,)), ((), ())), preferred_element_type=F32)


def _store_token_tiles(ref, v):
    n = v.shape[0]
    for s in range(TOKEN_TILE):
        ref[pl.ds(s, n, stride=TOKEN_TILE), :] = v[:, s * LANE:(s + 1) * LANE]


def _load_token_tiles(ref):
    n = ref.shape[0] // TOKEN_TILE
    return jnp.concatenate([ref[pl.ds(s, n, stride=TOKEN_TILE), :] for s in range(TOKEN_TILE)], axis=1)


def _layer_norm_rows(v, g, b):
    mu = jnp.mean(v, axis=-1, keepdims=True)
    c = v - mu
    var = jnp.mean(c * c, axis=-1, keepdims=True)
    return c * lax.rsqrt(var + LN_EPS) * g + b


def _rope_tables(positions):
    half = ROT_DIM // 2
    inv = ROPE_THETA ** (-jnp.arange(half, dtype=F32) / half)
    ang = positions.astype(F32)[:, :, None] * inv
    cos, sin = jnp.cos(ang), jnp.sin(ang)
    rest = HEAD_DIM - ROT_DIM
    one = jnp.ones(ang.shape[:2] + (rest,), F32)
    zero_r = jnp.zeros(ang.shape[:2] + (rest,), F32)
    zero_h = jnp.zeros_like(sin)
    c = jnp.concatenate([cos, cos, one], -1)
    s1 = jnp.concatenate([zero_h, sin, zero_r], -1)
    s2 = jnp.concatenate([-sin, zero_h, zero_r], -1)
    reps = LANE // HEAD_DIM
    lane_tabs = tuple(jnp.tile(t, (1, 1, reps)).reshape(-1, LANE) for t in (c, s1, s2))
    return lane_tabs + (cos.reshape(-1, half).T, sin.reshape(-1, half).T)


def _proj_kernel(x_ref, wrow_ref, wkey_ref, wq_ref, wval_ref, wgate_ref, c_ref, s1_ref, s2_ref, cos_ref, sin_ref,
                 mix_ref, br_ref, qT_ref, keys_ref, cmp_ref, valT_ref, gateT_ref, *, tiles_per_seq):
    tm = x_ref.shape[0]
    half = ROT_DIM // 2
    xb = x_ref[...].astype(BF16)

    for ref, col0 in ((mix_ref, 0), (br_ref, mix_ref.shape[1])):
        for col in range(0, ref.shape[1], PROJ_BLOCK):
            ref[:, col:col + PROJ_BLOCK] = _dot(xb, wrow_ref[:, col0 + col:col0 + col + PROJ_BLOCK]).astype(ref.dtype)

    def rope(v):
        return (v * c_ref[...] + pltpu.roll(v, half, 1) * s1_ref[...]
                + pltpu.roll(v, LANE - half, 1) * s2_ref[...])

    tpos = (pl.program_id(0) % tiles_per_seq) * tm + lax.broadcasted_iota(jnp.int32, (tm, LANE), 0)
    lane = lax.broadcasted_iota(jnp.int32, (tm, LANE), 1)
    onehot = jnp.where(lane == HEAD_DIM + (tpos // SLC_LEN) % SEL_GROUP, 1.0, 0.0)
    for k in range(4):
        acc = _dot(xb, wkey_ref[:, k * 2 * LANE:(k + 1) * 2 * LANE])
        for h in range(NSA_KV_HEADS):
            v = acc[:, h * LANE:(h + 1) * LANE]
            if k < 3:
                v = rope(v)
            if k == 0:
                cmp_ref[0, h] = v
            elif k == 1:
                keys_ref[0, h] = (v + onehot).astype(keys_ref.dtype)
            elif k == 2:
                keys_ref[1, h] = v.astype(keys_ref.dtype)
            else:
                cmp_ref[1, h] = v

    cos, sin = cos_ref[...], sin_ref[...]
    q_scale = HEAD_DIM ** -0.5 * LOG2_E
    for h in range(NSA_KV_HEADS):
        qt = _dot_nt(wq_ref[h], xb)
        for g in range(GROUP):
            blk = qt[g * HEAD_DIM:(g + 1) * HEAD_DIM]
            x1, x2 = blk[:half], blk[half:ROT_DIM]
            y = jnp.concatenate([x1 * cos - x2 * sin, x2 * cos + x1 * sin, blk[ROT_DIM:]], axis=0) * q_scale
            y = y.astype(qT_ref.dtype)
            for nn in range(tm // Q_BLOCK):
                dst = nn * GROUP * Q_BLOCK + g * Q_BLOCK
                qT_ref[h, :, dst:dst + Q_BLOCK] = y[:, nn * Q_BLOCK:(nn + 1) * Q_BLOCK]

    vt = _dot_nt(wval_ref[...], xb)
    ones_pad = jnp.where(lax.broadcasted_iota(jnp.int32, (ONES_PAD, tm), 0) == 0, 1.0, 0.0).astype(valT_ref.dtype)
    for k in range(2):
        for h in range(NSA_KV_HEADS):
            row = (k * NSA_KV_HEADS + h) * HEAD_DIM
            valT_ref[k, h, :HEAD_DIM, :] = vt[row:row + HEAD_DIM].astype(valT_ref.dtype)
            valT_ref[k, h, HEAD_DIM:, :] = ones_pad

    gt = _dot_nt(wgate_ref[...], xb)
    for h in range(NSA_KV_HEADS):
        gateT_ref[h] = gt[h * 16:(h + 1) * 16].astype(gateT_ref.dtype)


def _proj(x2, weights, tabs, b, seq, tm):
    t = x2.shape[0]
    tps = seq // tm
    wrow, wkey, wq, wval, wgate = weights
    h = NSA_KV_HEADS
    n_mix = POOL_WIDTH + 2 * SG_WIDTH
    row = lambda i: (i, 0)
    const = lambda shape: pl.BlockSpec(shape, lambda i: (0,) * len(shape), pipeline_mode=pl.Buffered(1))
    return pl.pallas_call(
        functools.partial(_proj_kernel, tiles_per_seq=tps),
        grid=(t // tm,),
        in_specs=[pl.BlockSpec((tm, D_MODEL), row), const(wrow.shape), const(wkey.shape), const(wq.shape),
                  const(wval.shape), const(wgate.shape),
                  pl.BlockSpec((tm, LANE), row), pl.BlockSpec((tm, LANE), row), pl.BlockSpec((tm, LANE), row),
                  pl.BlockSpec((ROT_DIM // 2, tm), lambda i: (0, i)), pl.BlockSpec((ROT_DIM // 2, tm), lambda i: (0, i))],
        out_specs=[pl.BlockSpec((tm, n_mix), row), pl.BlockSpec((tm, 3 * D_MODEL), row),
                   pl.BlockSpec((None, h, HEAD_DIM, tm * GROUP), lambda i: (i // tps, 0, 0, i % tps)),
                   pl.BlockSpec((2, None, h, tm, LANE), lambda i: (0, i // tps, 0, i % tps, 0)),
                   pl.BlockSpec((2, None, h, tm, LANE), lambda i: (0, i // tps, 0, i % tps, 0)),
                   pl.BlockSpec((2, None, h, HEAD_DIM + ONES_PAD, tm), lambda i: (0, i // tps, 0, 0, i % tps)),
                   pl.BlockSpec((None, h, 16, tm), lambda i: (i // tps, 0, 0, i % tps))],
        out_shape=[jax.ShapeDtypeStruct((t, n_mix), BF16), jax.ShapeDtypeStruct((t, 3 * D_MODEL), BF16),
                   jax.ShapeDtypeStruct((b, h, HEAD_DIM, seq * GROUP), BF16),
                   jax.ShapeDtypeStruct((2, b, h, seq, LANE), BF16),
                   jax.ShapeDtypeStruct((2, b, h, seq, LANE), F32),
                   jax.ShapeDtypeStruct((2, b, h, HEAD_DIM + ONES_PAD, seq), BF16),
                   jax.ShapeDtypeStruct((b, h, 16, seq), BF16)],
        compiler_params=_params(("parallel",)),
        name="proj",
    )(x2, wrow, wkey, wq, wval, wgate, *tabs)


def _prep_w_in(w):
    d = w.shape[0]
    h = NSA_KV_HEADS
    n_mix = POOL_WIDTH + 2 * SG_WIDTH
    kv0 = n_mix + Q_WIDTH
    n_gate = NSA_Q_HEADS * 3
    gate0 = kv0 + 6 * KV_WIDTH
    kv = lambda k: w[:, kv0 + k * KV_WIDTH:kv0 + (k + 1) * KV_WIDTH]
    wrow = jnp.concatenate([w[:, :n_mix], w[:, gate0 + n_gate:]], axis=1)
    pad_heads = lambda m: jnp.pad(m.reshape(d, h, HEAD_DIM), ((0, 0), (0, 0), (0, LANE - HEAD_DIM))).reshape(d, h * LANE)
    wkey = jnp.concatenate([pad_heads(kv(0)), pad_heads(kv(2)), pad_heads(kv(4)), pad_heads(kv(1))], axis=1)
    wq = w[:, n_mix:kv0].T.reshape(h, GROUP * HEAD_DIM, d)
    wval = jnp.concatenate([kv(3), kv(5)], axis=1).T
    wgate = jnp.pad(w[:, gate0:gate0 + n_gate].T.reshape(h, GROUP * 3, d), ((0, 0), (0, 16 - GROUP * 3), (0, 0)))
    return tuple(m.astype(BF16) for m in (wrow, wkey, wq, wval, wgate.reshape(h * 16, d)))


def _mixers_kernel(a_ref, ap_ref, u_ref, v_ref, pw_ref, ps_ref, lg_ref, lb_ref, sw_ref, sb_ref,
                   pm_ref, sg_ref, *, tiles_per_seq):
    i = pl.program_id(0)
    tm = a_ref.shape[0]
    it = i % tiles_per_seq
    gw = POOL_WIDTH // len(POOL_WINDOWS)

    a = a_ref[...].astype(F32)
    prev = jnp.where(it == 0, 0.0, ap_ref[...].astype(F32))
    ext = jnp.concatenate([prev, a], axis=0)
    tpos = (it * tm + lax.broadcasted_iota(jnp.int32, (tm, 1), 0) + 1).astype(F32)
    for g, w in enumerate(POOL_WINDOWS):
        s = ext[:, g * gw:(g + 1) * gw]
        k = 1
        while k < w:
            s = s + pltpu.roll(s, k, 0)
            k *= 2
        d = s[POOL_HALO:] / jnp.minimum(tpos, float(w)) - a[:, g * gw:(g + 1) * gw]
        y = _dot(d.astype(BF16), pw_ref[g]) * ps_ref[:, g * gw:(g + 1) * gw]
        pm_ref[:, g * gw:(g + 1) * gw] = y.astype(pm_ref.dtype)

    u = jax.nn.gelu(u_ref[...].astype(F32))
    v = _layer_norm_rows(jax.nn.gelu(v_ref[...].astype(F32)), lg_ref[...], lb_ref[...]).astype(BF16)
    hc = SG_WIDTH // SG_HEADS
    tri = (lax.broadcasted_iota(jnp.int32, (SG_CHUNK, SG_CHUNK), 0)
           >= lax.broadcasted_iota(jnp.int32, (SG_CHUNK, SG_CHUNK), 1))
    for g in range(SG_HEADS):
        ws = jnp.where(tri, sw_ref[g], 0.0).astype(BF16)
        bias = sb_ref[:, g:g + 1]
        for c in range(tm // SG_CHUNK):
            rows = slice(c * SG_CHUNK, (c + 1) * SG_CHUNK)
            cols = slice(g * hc, (g + 1) * hc)
            mixed = _dot(ws, v[rows, cols]) + bias
            sg_ref[rows, cols] = (u[rows, cols] * mixed).astype(sg_ref.dtype)


def _mixers(mix, pool_w, pool_scale, ln_g, ln_b, sg_w, sg_b, seq, tm):
    t = mix.shape[0]
    halo_blocks = tm // POOL_HALO
    full = lambda shape: pl.BlockSpec(shape, lambda i: (0,) * len(shape))
    return pl.pallas_call(
        functools.partial(_mixers_kernel, tiles_per_seq=seq // tm),
        grid=(t // tm,),
        in_specs=[pl.BlockSpec((tm, POOL_WIDTH), lambda i: (i, 0)),
                  pl.BlockSpec((POOL_HALO, POOL_WIDTH), lambda i: (jnp.maximum(i * halo_blocks - 1, 0), 0)),
                  pl.BlockSpec((tm, SG_WIDTH), lambda i: (i, 1)),
                  pl.BlockSpec((tm, SG_WIDTH), lambda i: (i, 2)),
                  full(pool_w.shape), full((1, POOL_WIDTH)), full((1, SG_WIDTH)), full((1, SG_WIDTH)),
                  full(sg_w.shape), full((SG_CHUNK, SG_HEADS))],
        out_specs=[pl.BlockSpec((tm, POOL_WIDTH), lambda i: (i, 0)),
                   pl.BlockSpec((tm, SG_WIDTH), lambda i: (i, 0))],
        out_shape=[jax.ShapeDtypeStruct((t, POOL_WIDTH), BF16), jax.ShapeDtypeStruct((t, SG_WIDTH), BF16)],
        compiler_params=_params(("parallel",)),
        name="mixers",
    )(mix, mix, mix, mix, pool_w.astype(BF16), pool_scale[None], ln_g[None], ln_b[None], sg_w, sg_b.T)


def _compress_kernel(src_ref, pos_ref, w1_ref, w2_ref, w2T_ref, o_ref, oT_ref):
    nc = o_ref.shape[0]
    first = jnp.zeros((nc, CMP_HIDDEN), F32)
    second = jnp.zeros((nc, CMP_HIDDEN), F32)
    for p in range(CMP_STRIDE):
        rows = src_ref[pl.ds(p, nc, stride=CMP_STRIDE), :]
        first = first + _dot((rows + pos_ref[p:p + 1, :]).astype(BF16), w1_ref[p])
        q = CMP_STRIDE + p
        second = second + _dot((rows + pos_ref[q:q + 1, :]).astype(BF16), w1_ref[q])
    pre = first + pltpu.roll(second, nc - 1, 0)
    hidden = jax.nn.gelu(pre).astype(BF16)
    o_ref[...] = _dot(hidden, w2_ref[...]).astype(o_ref.dtype)
    oT_ref[...] = _dot_nt(w2T_ref[...], hidden).astype(oT_ref.dtype)


def _compress(src, pos, w1, w2, w2T):
    _, b, h, seq, _ = src.shape
    nc = seq // CMP_STRIDE
    sel = lambda n: pl.BlockSpec((None,) + n, lambda s, bi, hi: (s,) + (0,) * len(n))
    return pl.pallas_call(
        _compress_kernel,
        grid=(2, b, h),
        in_specs=[pl.BlockSpec((None, None, None, seq, LANE), lambda s, bi, hi: (s, bi, hi, 0, 0)),
                  sel(pos.shape[1:]), sel(w1.shape[1:]), sel(w2.shape[1:]), sel(w2T.shape[1:])],
        out_specs=[pl.BlockSpec((None, None, None, nc, HEAD_DIM), lambda s, bi, hi: (s, bi, hi, 0, 0)),
                   pl.BlockSpec((None, None, None, HEAD_DIM, nc), lambda s, bi, hi: (s, bi, hi, 0, 0))],
        out_shape=[jax.ShapeDtypeStruct((2, b, h, nc, HEAD_DIM), BF16),
                   jax.ShapeDtypeStruct((2, b, h, HEAD_DIM, nc), BF16)],
        compiler_params=_params(("parallel", "parallel", "parallel")),
        name="compress",
    )(src, pos, w1, w2, w2T)


def _prep_compress(k_pos, k_w1, k_w2, v_pos, v_w1, v_w2):
    pad = LANE - HEAD_DIM
    pos = jnp.pad(jnp.stack([k_pos, v_pos]), ((0, 0), (0, 0), (0, pad)))
    w1 = jnp.stack([k_w1, v_w1]).reshape(2, CMP_LEN, HEAD_DIM, CMP_HIDDEN)
    w1 = jnp.pad(w1, ((0, 0), (0, 0), (0, pad), (0, 0))).astype(BF16)
    w2 = jnp.stack([k_w2, v_w2]).astype(BF16)
    return pos, w1, w2, w2.transpose(0, 2, 1)


def _score_strips(k_ref, k0, n_strips, q_aug, s_ref, mask):
    mx = None
    for r in range(n_strips):
        start = pl.multiple_of(k0 + r * NSA_STRIP, NSA_STRIP)
        s = _dot(k_ref[pl.ds(start, NSA_STRIP), :], q_aug).astype(s_ref.dtype)
        if mask is not None:
            s = mask(r, start, s)
        s_ref[r * NSA_STRIP:(r + 1) * NSA_STRIP, :] = s
        mx = s if mx is None else jnp.maximum(mx, s)
    return jnp.max(mx.astype(F32), axis=0, keepdims=True)


def _prob_strips(n_strips, m, s_ref, p_ref):
    m = m.astype(s_ref.dtype)
    for r in range(n_strips):
        rows = slice(r * NSA_STRIP, (r + 1) * NSA_STRIP)
        p_ref[rows, :] = jnp.exp2(s_ref[rows, :] - m).astype(BF16)


def _nsa_kernel(qT_ref, kc_ref, vcT_ref, ks_ref, vsT_ref, kw_ref, vwT_ref, g_ref, bound_ref, o_ref,
                s_ref, p_ref, pc_ref, s2_ref, p2_ref, psum_ref, selneg_ref, *, n_top):
    n = pl.program_id(2)
    qs = n * Q_BLOCK
    width = GROUP * Q_BLOCK
    qT = qT_ref[...]
    t_row = qs + lax.broadcasted_iota(jnp.int32, (1, width), 1) % Q_BLOCK
    nc = kc_ref.shape[0]
    ns = selneg_ref.shape[0]
    strip_iota = lax.broadcasted_iota(jnp.int32, (NSA_STRIP, 1), 0)

    per = SLC_LEN // CMP_STRIDE
    psum_ref[:PSUM_PAD, :] = jnp.zeros((PSUM_PAD, Q_BLOCK), F32)

    def compressed_and_selection(n_tiles):
        keys = n_tiles * CMP_TILE
        rows = keys // per
        strip_rows = [slice(r * NSA_STRIP, (r + 1) * NSA_STRIP) for r in range(keys // NSA_STRIP)]

        mx = None
        for sl in strip_rows:
            s = _dot(kc_ref[sl, :], qT)
            c_end = (sl.start + strip_iota) * CMP_STRIDE + (CMP_LEN - 1)
            s = jnp.where(c_end <= t_row, s, NEG_INF)
            s_ref[sl, :] = s
            mx = s if mx is None else jnp.maximum(mx, s)
        m_cmp = jnp.maximum(jnp.max(mx, axis=0, keepdims=True), 0.1 * NEG_INF)
        lsum = jnp.zeros((NSA_STRIP, width), F32)
        for sl in strip_rows:
            e = jnp.exp2(s_ref[sl, :] - m_cmp)
            s_ref[sl, :] = e
            pc_ref[sl, :] = e.astype(BF16)
            lsum = lsum + e
        l = jnp.sum(lsum, axis=0, keepdims=True)
        inv = jnp.where(l > 0.0, 1.0 / l, 0.0)
        out = _dot(vcT_ref[:, :keys], pc_ref[:keys, :]) * inv
        for sl in strip_rows:
            p = s_ref[sl, :] * inv
            psum = p[:, :Q_BLOCK]
            for g in range(1, GROUP):
                psum = psum + p[:, g * Q_BLOCK:(g + 1) * Q_BLOCK]
            psum_ref[PSUM_PAD + sl.start:PSUM_PAD + sl.stop, :] = psum

        part = [psum_ref[pl.ds(PSUM_PAD + k, rows, stride=per), :] for k in range(per)]
        before = psum_ref[pl.ds(PSUM_PAD - 1, rows, stride=per), :]
        imp = 2.0 * (part[0] + part[1] + part[2]) + part[3] + before

        blk = lax.broadcasted_iota(jnp.int32, (rows, Q_BLOCK), 0)
        cur = (qs + lax.broadcasted_iota(jnp.int32, (rows, Q_BLOCK), 1)) // SLC_LEN
        taken = -2.0
        score = jnp.where(blk <= cur, imp, -1.0)
        score = jnp.where(blk == 0, taken, jnp.where(blk == cur, taken, jnp.where(blk == cur - 1, taken, score)))
        blk_f = blk.astype(F32)
        for _ in range(n_top - 3):
            top = jnp.max(score, axis=0, keepdims=True)
            cand = jnp.where(score == top, blk_f, float(rows))
            first = jnp.min(cand, axis=0, keepdims=True)
            score = jnp.where(cand == first, taken, score)
        selneg_ref[:rows, :] = jnp.where(score == taken, 0.0, NEG_INF)
        if rows < ns:
            selneg_ref[rows:, :] = jnp.full((ns - rows, Q_BLOCK), NEG_INF, F32)
        return out

    reach = CMP_TILE * CMP_STRIDE // Q_BLOCK
    o_cmp = lax.switch(n // reach, [functools.partial(compressed_and_selection, t + 1) for t in range(nc // CMP_TILE)])

    strips = SLC_BLOCK // NSA_STRIP
    sel_rows = ks_ref.shape[1] - HEAD_DIM - SEL_GROUP
    last_block = ks_ref.shape[0] // SLC_BLOCK - 1
    n_diag = n // (SLC_BLOCK // Q_BLOCK)

    def past_only(r, start, s):
        return jnp.where(start + strip_iota <= t_row, s, NEG_INF)

    def slc_scores(kb, slot):
        per_block = SLC_BLOCK // Q_BLOCK
        variant = jnp.where(kb < n_diag, 0, jnp.where(kb > n_diag, 1 + per_block, 1 + n % per_block))

        def limit(r, start, s):
            return jnp.minimum(s, bound_ref[variant, r * NSA_STRIP:(r + 1) * NSA_STRIP, :])

        kb = jnp.minimum(kb, last_block)
        k0 = pl.multiple_of(kb * SLC_BLOCK, SLC_BLOCK)
        g0 = pl.multiple_of((kb * strips // SEL_GROUP) * SEL_GROUP, SEL_GROUP)
        bias = jnp.concatenate([selneg_ref[pl.ds(g0, SEL_GROUP), :]] * GROUP, axis=1)
        bias = jnp.concatenate([bias, jnp.zeros((sel_rows, width), F32)], axis=0).astype(BF16)
        q_aug = jnp.concatenate([qT, bias], axis=0)
        return _score_strips(ks_ref, k0, strips, q_aug, s2_ref.at[slot], limit)

    def slc_consume(kb, slot, m, acc, mx):
        k0 = pl.multiple_of(kb * SLC_BLOCK, SLC_BLOCK)
        m_new = jnp.maximum(m, mx)
        _prob_strips(strips, m_new, s2_ref.at[slot], p2_ref.at[slot])
        pv = _dot(vsT_ref[:, pl.ds(k0, SLC_BLOCK)], p2_ref[slot])
        return m_new, jnp.exp2(m - m_new) * acc + pv

    def slc_group(j, carry):
        m, acc, pending = carry[0], carry[1], list(carry[2:])
        for u in range(SLC_UNROLL):
            kb = SLC_UNROLL * j + u
            pending.append(slc_scores(kb + SLC_AHEAD, (u + SLC_AHEAD) % SLC_UNROLL))
            m, acc = slc_consume(kb, u, m, acc, pending.pop(0))
        return (m, acc, *pending)

    init = (jnp.full((1, width), NEG_INF, F32), jnp.zeros((vsT_ref.shape[0], width), F32),
            *[slc_scores(a, a) for a in range(SLC_AHEAD)])
    acc = lax.fori_loop(0, (n_diag + SLC_UNROLL) // SLC_UNROLL, slc_group, init)[1]
    o_slc = acc[:HEAD_DIM] / acc[HEAD_DIM:HEAD_DIM + 1]

    span = WINDOW + Q_BLOCK
    w_strips = span // NSA_STRIP
    edge = Q_BLOCK // NSA_STRIP
    w0 = pl.multiple_of(jnp.maximum(qs - WINDOW, 0), Q_BLOCK)

    def band_edges(r, start, s):
        if r < edge:
            return jnp.where(t_row - (start + strip_iota) < WINDOW, s, NEG_INF)
        if r >= w_strips - edge:
            return past_only(r, start, s)
        return s

    q_win = jnp.concatenate([qT, jnp.zeros((kw_ref.shape[1] - HEAD_DIM, width), BF16)], axis=0)
    m_win = lax.cond(n >= WINDOW // Q_BLOCK,
                     lambda: _score_strips(kw_ref, w0, w_strips, q_win, s_ref, band_edges),
                     lambda: _score_strips(kw_ref, w0, w_strips, q_win, s_ref, past_only))
    _prob_strips(w_strips, m_win, s_ref, p_ref)
    pv = _dot(vwT_ref[:, pl.ds(w0, span)], p_ref[:span, :])
    o_win = pv[:HEAD_DIM] / pv[HEAD_DIM:HEAD_DIM + 1]

    gates = jax.nn.sigmoid(g_ref[...].astype(F32))

    def gate(c):
        return jnp.concatenate([gates[g * 3 + c:g * 3 + c + 1, :] for g in range(GROUP)], axis=1)

    o = gate(0) * o_cmp + gate(1) * o_slc + gate(2) * o_win
    for g in range(GROUP):
        o_ref[:, g * HEAD_DIM:(g + 1) * HEAD_DIM] = o[:, g * Q_BLOCK:(g + 1) * Q_BLOCK].T.astype(o_ref.dtype)


def _causal_bounds():
    big = np.float32(3.0e38)
    key = np.arange(SLC_BLOCK)[:, None]
    query = np.tile(np.arange(Q_BLOCK), GROUP)[None, :]
    tables = [np.full((SLC_BLOCK, GROUP * Q_BLOCK), big, np.float32)]
    for v in range(SLC_BLOCK // Q_BLOCK):
        tables.append(np.where(key <= query + v * Q_BLOCK, big, np.float32(NEG_INF)).astype(np.float32))
    tables.append(np.full((SLC_BLOCK, GROUP * Q_BLOCK), NEG_INF, np.float32))
    return jnp.asarray(np.stack(tables))


def _nsa(qT, cmp, cmpT, keys, valT, gates, seq):
    b, h = qT.shape[:2]
    nq = seq // Q_BLOCK
    nc = cmp.shape[3]
    ns = seq // SLC_LEN
    width = GROUP * Q_BLOCK
    assert nc == ns * (SLC_LEN // CMP_STRIDE) and nc % CMP_TILE == 0 and seq % (SLC_BLOCK * SLC_UNROLL) == 0
    assert seq >= WINDOW + Q_BLOCK and ns >= SLC_TOP
    bounds = _causal_bounds()
    per_head = lambda arr, k: pl.BlockSpec((None, None, None) + arr.shape[3:], lambda bi, hi, n, k=k: (k, bi, hi, 0, 0))
    return pl.pallas_call(
        functools.partial(_nsa_kernel, n_top=SLC_TOP),
        grid=(b, h, nq),
        in_specs=[pl.BlockSpec((None, None, HEAD_DIM, width), lambda bi, hi, n: (bi, hi, 0, n)),
                  per_head(cmp, 0), per_head(cmpT, 1),
                  per_head(keys, 0), per_head(valT, 0), per_head(keys, 1), per_head(valT, 1),
                  pl.BlockSpec((None, None, 16, Q_BLOCK), lambda bi, hi, n: (bi, hi, 0, n)),
                  pl.BlockSpec(bounds.shape, lambda bi, hi, n: (0, 0, 0), pipeline_mode=pl.Buffered(1))],
        out_specs=pl.BlockSpec((Q_BLOCK, GROUP * HEAD_DIM), lambda bi, hi, n: (bi * nq + n, hi)),
        out_shape=jax.ShapeDtypeStruct((b * seq, Q_WIDTH), BF16),
        scratch_shapes=[pltpu.VMEM((max(nc, WINDOW + Q_BLOCK), width), F32),
                        pltpu.VMEM((WINDOW + Q_BLOCK, width), BF16),
                        pltpu.VMEM((nc, width), BF16),
                        pltpu.VMEM((SLC_UNROLL, SLC_BLOCK, width), F32),
                        pltpu.VMEM((SLC_UNROLL, SLC_BLOCK, width), BF16),
                        pltpu.VMEM((PSUM_PAD + nc, Q_BLOCK), F32),
                        pltpu.VMEM((ns, Q_BLOCK), F32)],
        compiler_params=_params(("parallel", "parallel", "arbitrary")),
        name="nsa",
    )(qT, cmp, cmpT, keys, valT, keys, valT, gates, bounds)


def _merge_kernel(*refs, with_router, alpha):
    (pm_ref, sg_ref, on_ref, gp_ref, gs_ref, gn_ref, x_ref, wp_ref, ws_ref, wn_ref, wo_ref, lg_ref, lb_ref) = refs[:13]
    if with_router:
        wr_ref, br_ref, x1_ref, route_ref, x1t_ref = refs[13:]
    else:
        (x1_ref,) = refs[13:]
    sig = lambda r: jax.nn.sigmoid(r[...].astype(F32))
    y = (sig(gp_ref) * _dot(pm_ref[...], wp_ref[...]) + sig(gs_ref) * _dot(sg_ref[...], ws_ref[...])
         + sig(gn_ref) * _dot(on_ref[...], wn_ref[...]))
    hmix = _dot(y.astype(BF16), wo_ref[...])
    x1 = _layer_norm_rows(alpha * x_ref[...] + hmix, lg_ref[...], lb_ref[...])
    x1_ref[...] = x1
    if with_router:
        _store_token_tiles(x1t_ref, x1)
        wr = wr_ref[...]
        w_hi = wr.astype(BF16)
        w_lo = (wr - w_hi.astype(F32)).astype(BF16)
        x_hi = x1.astype(BF16)
        x_lo = (x1 - x_hi.astype(F32)).astype(BF16)
        logits = _dot(x_hi, w_hi) + (_dot(x_lo, w_hi) + _dot(x_hi, w_lo)) + br_ref[...]
        lane = lax.broadcasted_iota(jnp.int32, logits.shape, 1)
        lane_f = lane.astype(F32)
        logits = jnp.where(lane < N_EXPERTS, logits, -jnp.inf)
        m1 = jnp.max(logits, axis=-1, keepdims=True)
        i1 = jnp.min(jnp.where(logits == m1, lane_f, float(LANE)), axis=-1, keepdims=True)
        rest = jnp.where(lane_f == i1, -jnp.inf, logits)
        m2 = jnp.max(rest, axis=-1, keepdims=True)
        i2 = jnp.min(jnp.where(rest == m2, lane_f, float(LANE)), axis=-1, keepdims=True)
        e2 = jnp.exp(m2 - m1)
        w1 = 1.0 / (1.0 + e2)
        route_ref[...] = jnp.where(lane == 0, i1, jnp.where(lane == 1, i2, jnp.where(lane == 2, w1,
                                   jnp.where(lane == 3, e2 * w1, 0.0))))


def _merge(pm, sgm, on, br, x2, wp, ws, wn, wo, lg, lb, alpha, tm, router=None):
    t = x2.shape[0]
    row = lambda width, cb=0: pl.BlockSpec((tm, width), lambda i, cb=cb: (i, cb))
    full = lambda shape: pl.BlockSpec(shape, lambda i: (0,) * len(shape))
    in_specs = [row(POOL_WIDTH), row(SG_WIDTH), row(Q_WIDTH),
                row(D_MODEL, 0), row(D_MODEL, 1), row(D_MODEL, 2),
                row(D_MODEL), full(wp.shape), full(ws.shape), full(wn.shape), full(wo.shape),
                full((1, D_MODEL)), full((1, D_MODEL))]
    args = [pm, sgm, on, br, br, br, x2, wp.astype(BF16), ws.astype(BF16), wn.astype(BF16), wo.astype(BF16),
            lg[None], lb[None]]
    out_specs = [row(D_MODEL)]
    out_shape = [jax.ShapeDtypeStruct((t, D_MODEL), F32)]
    if router is not None:
        w_router, b_router = router
        pad = LANE - N_EXPERTS
        in_specs += [full((D_MODEL, LANE)), full((1, LANE))]
        args += [jnp.pad(w_router, ((0, 0), (0, pad))), jnp.pad(b_router, (0, pad))[None]]
        out_specs += [row(LANE), pl.BlockSpec((tm * TOKEN_TILE, LANE), lambda i: (i, 0))]
        out_shape += [jax.ShapeDtypeStruct((t, LANE), F32), jax.ShapeDtypeStruct((t * TOKEN_TILE, LANE), F32)]
    out = pl.pallas_call(
        functools.partial(_merge_kernel, with_router=router is not None, alpha=alpha),
        grid=(t // tm,), in_specs=in_specs, out_specs=out_specs, out_shape=out_shape,
        compiler_params=_params(("parallel",)),
        name="merge",
    )(*args)
    return out if router is not None else out[0]


def _tile_rows(i):
    start = i * TOKEN_TILE if isinstance(i, int) else pl.multiple_of(i * TOKEN_TILE, TOKEN_TILE)
    return pl.ds(start, TOKEN_TILE)


def _for_rows(n, fn):
    def body(r, c):
        fn(r)
        return c
    lax.fori_loop(0, n, body, 0, unroll=8)


def _moe_kernel(be_ref, tok_ref, tok_next_ref, slot_ref, x_hbm, wg_ref, wu_ref, wd_ref, y_hbm,
                xg_ref, xb_ref, acc_ref, yt_ref, gather_sem, scatter_sem):
    i = pl.program_id(0)
    k = pl.program_id(1)
    n_blocks = pl.num_programs(0)
    rows = xb_ref.shape[0]

    def gather(idx_ref, slot):
        return lambda r: pltpu.make_async_copy(x_hbm.at[_tile_rows(idx_ref[0, r])], xg_ref.at[slot, _tile_rows(r)],
                                               gather_sem.at[slot])

    def scatter(idx_ref):
        return lambda r: pltpu.make_async_copy(yt_ref.at[_tile_rows(r)], y_hbm.at[_tile_rows(idx_ref[0, r])],
                                               scatter_sem)

    @pl.when(k == 0)
    def _():
        slot = i % 2

        @pl.when(i == 0)
        def _():
            _for_rows(rows, lambda r: gather(tok_ref, 0)(r).start())

        pltpu.make_async_copy(x_hbm.at[pl.ds(0, rows * TOKEN_TILE)], xg_ref.at[slot], gather_sem.at[slot]).wait()

        @pl.when(i + 1 < n_blocks)
        def _():
            _for_rows(rows, lambda r: gather(tok_next_ref, 1 - slot)(r).start())

        for s in range(TOKEN_TILE):
            xb_ref[:, s * LANE:(s + 1) * LANE] = xg_ref[slot, pl.ds(s, rows, stride=TOKEN_TILE), :].astype(BF16)

    xb = xb_ref[...]
    hidden = (jax.nn.silu(_dot(xb, wg_ref[...])) * _dot(xb, wu_ref[...])).astype(BF16)
    part = _dot(hidden, wd_ref[...])

    @pl.when(k == 0)
    def _():
        acc_ref[...] = part

    @pl.when(k == pl.num_programs(1) - 1)
    def _():
        def wait_scatter():
            pltpu.make_async_copy(yt_ref, y_hbm.at[pl.ds(0, rows * TOKEN_TILE)], scatter_sem).wait()

        pl.when(i > 0)(wait_scatter)
        _store_token_tiles(yt_ref, acc_ref[...] + part)
        _for_rows(rows, lambda r: scatter(slot_ref)(r).start())
        pl.when(i == n_blocks - 1)(wait_scatter)


def _moe(x_tiles, row_tok, row_slot, n_slots, block_e, wg, wu, wd):
    rows = EXPERT_BLOCK
    n_blocks = row_tok.shape[0] // rows
    kf = D_FF // FF_TILE
    assert kf == 2
    idx_spec = lambda fn: pl.BlockSpec((None, 1, rows), lambda i, k, be: (fn(i), 0, 0), memory_space=pltpu.SMEM)
    tok3 = row_tok.reshape(n_blocks, 1, rows)
    slot3 = row_slot.reshape(n_blocks, 1, rows)
    return pl.pallas_call(
        _moe_kernel,
        grid_spec=pltpu.PrefetchScalarGridSpec(
            num_scalar_prefetch=1,
            grid=(n_blocks, kf),
            in_specs=[idx_spec(lambda i: i), idx_spec(lambda i: jnp.minimum(i + 1, n_blocks - 1)),
                      idx_spec(lambda i: i),
                      pl.BlockSpec(memory_space=pl.ANY),
                      pl.BlockSpec((None, D_MODEL, FF_TILE), lambda i, k, be: (be[i], 0, k)),
                      pl.BlockSpec((None, D_MODEL, FF_TILE), lambda i, k, be: (be[i], 0, k)),
                      pl.BlockSpec((None, FF_TILE, D_MODEL), lambda i, k, be: (be[i], k, 0))],
            out_specs=pl.BlockSpec(memory_space=pl.ANY),
            scratch_shapes=[pltpu.VMEM((2, rows * TOKEN_TILE, LANE), F32), pltpu.VMEM((rows, D_MODEL), BF16),
                            pltpu.VMEM((rows, D_MODEL), F32), pltpu.VMEM((rows * TOKEN_TILE, LANE), F32),
                            pltpu.SemaphoreType.DMA((2,)), pltpu.SemaphoreType.DMA(())]),
        out_shape=jax.ShapeDtypeStruct((n_slots * TOKEN_TILE, LANE), F32),
        compiler_params=_params(("arbitrary", "arbitrary")),
        name="moe",
    )(block_e, tok3, tok3, slot3, x_tiles, wg, wu, wd)


def _swiglu_kernel(x_ref, wg_ref, wu_ref, wd_ref, o_ref, xb_ref):
    k = pl.program_id(1)

    @pl.when(k == 0)
    def _():
        xb_ref[...] = x_ref[...].astype(BF16)

    xb = xb_ref[...]
    hidden = (jax.nn.silu(_dot(xb, wg_ref[...])) * _dot(xb, wu_ref[...])).astype(BF16)
    part = _dot(hidden, wd_ref[...])

    @pl.when(k == 0)
    def _():
        o_ref[...] = part

    @pl.when(k != 0)
    def _():
        o_ref[...] += part


def _swiglu(rows, wg, wu, wd, tm):
    r = rows.shape[0]
    row = pl.BlockSpec((tm, D_MODEL), lambda i, k: (i, 0))
    return pl.pallas_call(
        _swiglu_kernel,
        grid=(r // tm, D_FF // FF_TILE),
        in_specs=[row,
                  pl.BlockSpec((D_MODEL, FF_TILE), lambda i, k: (0, k)),
                  pl.BlockSpec((D_MODEL, FF_TILE), lambda i, k: (0, k)),
                  pl.BlockSpec((FF_TILE, D_MODEL), lambda i, k: (k, 0))],
        out_specs=row,
        out_shape=jax.ShapeDtypeStruct(rows.shape, F32),
        scratch_shapes=[pltpu.VMEM((tm, D_MODEL), BF16)],
        compiler_params=_params(("parallel", "arbitrary")),
        name="swiglu",
    )(rows, wg, wu, wd)


def _ple_ln2_kernel(*refs, n_parts, alpha):
    x_ref, p_ref = refs[0], refs[1]
    parts = refs[2:2 + n_parts]
    rest = refs[2 + n_parts:]
    if n_parts > 1:
        route_ref, rest = rest[0], rest[1:]
    wg_ref, bg_ref, wp_ref, lg_ref, lb_ref, o_ref = rest
    x1 = x_ref[...]
    if n_parts > 1:
        route = route_ref[...]
        f = _load_token_tiles(parts[0]) * route[:, TOP_K:TOP_K + 1]
        for k in range(1, n_parts):
            f = f + _load_token_tiles(parts[k]) * route[:, TOP_K + k:TOP_K + k + 1]
    else:
        f = parts[0][...]
    gate = jax.nn.sigmoid(_dot(x1.astype(BF16), wg_ref[...]) + bg_ref[...])
    ple = gate * _dot(p_ref[...].astype(BF16), wp_ref[...])
    o_ref[...] = _layer_norm_rows(alpha * x1 + f + ple, lg_ref[...], lb_ref[...])


def _ple_ln2(x1, p2, y, route, wg, bg, wp, lg, lb, alpha, tm):
    t = x1.shape[0]
    n_parts = 1 if route is None else TOP_K
    steps = t // tm
    row = lambda width: pl.BlockSpec((tm, width), lambda i: (i, 0))
    full = lambda shape: pl.BlockSpec(shape, lambda i: (0,) * len(shape))
    in_specs = [row(D_MODEL), row(PLE_DIM)]
    if n_parts > 1:
        in_specs += [pl.BlockSpec((tm * TOKEN_TILE, LANE), lambda i, k=k: (k * steps + i, 0)) for k in range(n_parts)]
    else:
        in_specs.append(row(D_MODEL))
    args = [x1, p2] + [y] * n_parts
    if n_parts > 1:
        in_specs.append(row(LANE))
        args.append(route)
    in_specs += [full((D_MODEL, D_MODEL)), full((1, D_MODEL)), full((PLE_DIM, D_MODEL)),
                 full((1, D_MODEL)), full((1, D_MODEL))]
    args += [wg.astype(BF16), bg[None], wp.astype(BF16), lg[None], lb[None]]
    return pl.pallas_call(
        functools.partial(_ple_ln2_kernel, n_parts=n_parts, alpha=alpha),
        grid=(steps,), in_specs=in_specs, out_specs=row(D_MODEL),
        out_shape=jax.ShapeDtypeStruct((t, D_MODEL), F32),
        compiler_params=_params(("parallel",)),
        name="ple_ln2",
    )(*args)


def _route_tables(route, t):
    flat_e = route[:, :TOP_K].astype(jnp.int32).reshape(-1)
    tk = flat_e.shape[0]
    onehot = (flat_e[:, None] == jnp.arange(N_EXPERTS, dtype=jnp.int32)[None, :]).astype(jnp.int32)
    before = jnp.cumsum(onehot, axis=0) - onehot
    rank = jnp.sum(before * onehot, axis=1)
    counts = jnp.sum(onehot, axis=0)
    padded = (counts + EXPERT_BLOCK - 1) // EXPERT_BLOCK * EXPERT_BLOCK
    pend = jnp.cumsum(padded)
    dest = (pend - padded)[flat_e] + rank
    n_blocks = -(-tk // EXPERT_BLOCK) + N_EXPERTS
    n_rows = n_blocks * EXPERT_BLOCK
    row_pair = jnp.full((n_rows,), -1, jnp.int32).at[dest].set(jnp.arange(tk, dtype=jnp.int32))
    is_pad = (row_pair < 0).astype(jnp.int32)
    row_tok = jnp.where(row_pair < 0, 0, row_pair // TOP_K)
    pad_slot = tk + jnp.cumsum(is_pad) - is_pad
    row_slot = jnp.where(row_pair < 0, pad_slot, (row_pair % TOP_K) * t + row_pair // TOP_K).astype(jnp.int32)
    block_start = jnp.arange(n_blocks, dtype=jnp.int32) * EXPERT_BLOCK
    block_e = jnp.sum((pend[None, :] <= block_start[:, None]).astype(jnp.int32), axis=1)
    block_e = jnp.minimum(block_e, N_EXPERTS - 1).astype(jnp.int32)
    return row_tok, row_slot, n_rows, block_e


def kernel(x, p, positions, w_in, pool_w, pool_scale, sg_ln_g, sg_ln_b, sg_w, sg_b, cmp_k_pos, cmp_k_w1, cmp_k_w2, cmp_v_pos, cmp_v_w1, cmp_v_w2, w_pool_out, w_sg_out, w_nsa_out, w_out, ln1_g, ln1_b, ffn_w_gate, ffn_w_up, ffn_w_down, moe_router, moe_router_b, moe_w_gate, moe_w_up, moe_w_down, ple_gate_w, ple_gate_b, ple_proj, ln2_g, ln2_b):
    b, seq, d = x.shape
    depth = w_in.shape[0]
    t = b * seq
    alpha = (2 * depth) ** 0.25
    tm = 512
    tabs = _rope_tables(positions)
    x2 = x.reshape(t, d)
    for i in range(depth):
        mix, br, qT, keys, cmp_src, valT, gates = _proj(x2, _prep_w_in(w_in[i]), tabs, b, seq, tm)
        pm, sgm = _mixers(mix, pool_w[i], pool_scale[i], sg_ln_g[i], sg_ln_b[i], sg_w[i], sg_b[i], seq, tm)
        cmp, cmpT = _compress(cmp_src, *_prep_compress(cmp_k_pos[i], cmp_k_w1[i], cmp_k_w2[i],
                                                       cmp_v_pos[i], cmp_v_w1[i], cmp_v_w2[i]))
        on = _nsa(qT, cmp, cmpT, keys, valT, gates, seq)
        j = i // 2
        moe = i % 2 == 1
        merged = _merge(pm, sgm, on, br, x2, w_pool_out[i], w_sg_out[i], w_nsa_out[i], w_out[i],
                        ln1_g[i], ln1_b[i], alpha, tm, router=(moe_router[j], moe_router_b[j]) if moe else None)
        if moe:
            x1, route, x1_tiles = merged
            row_tok, row_slot, n_slots, block_e = _route_tables(route, t)
            y = _moe(x1_tiles, row_tok, row_slot, n_slots, block_e, moe_w_gate[j].astype(BF16),
                     moe_w_up[j].astype(BF16), moe_w_down[j].astype(BF16))
        else:
            x1, route = merged, None
            y = _swiglu(x1, ffn_w_gate[j].astype(BF16), ffn_w_up[j].astype(BF16), ffn_w_down[j].astype(BF16), tm)
        x2 = _ple_ln2(x1, p[i].reshape(t, PLE_DIM), y, route, ple_gate_w[i], ple_gate_b[i], ple_proj[i],
                      ln2_g[i], ln2_b[i], alpha, tm)
    return x2.reshape(b, seq, d)
```

```python
import functools

import jax
import jax.numpy as jnp
import numpy as np
from jax import lax
from jax.experimental import pallas as pl
from jax.experimental.pallas import tpu as pltpu

F32 = jnp.float32
BF16 = jnp.bfloat16

D_MODEL = 1024
POOL_WIDTH = 512
POOL_WINDOWS = (2, 4, 8, 16)
POOL_HALO = 16
SG_WIDTH = 512
SG_HEADS = 4
SG_CHUNK = 128
NSA_Q_HEADS = 8
NSA_KV_HEADS = 2
GROUP = NSA_Q_HEADS // NSA_KV_HEADS
HEAD_DIM = 64
Q_WIDTH = NSA_Q_HEADS * HEAD_DIM
KV_WIDTH = NSA_KV_HEADS * HEAD_DIM
ROT_DIM = HEAD_DIM // 4
ROPE_THETA = 500000.0
CMP_LEN = 32
CMP_STRIDE = 16
CMP_HIDDEN = 256
SLC_LEN = 64
SLC_TOP = 16
WINDOW = 512
Q_BLOCK = 128
NEG_INF = -1e30
D_FF = 2816
N_EXPERTS = 8
TOP_K = 2
EXPERT_BLOCK = 512
PLE_DIM = 256
LN_EPS = 1e-5

LANE = 128
TOKEN_TILE = D_MODEL // LANE
PROJ_BLOCK = 512
NSA_STRIP = 64
SLC_BLOCK = 256
SLC_UNROLL = 4
SLC_AHEAD = 2
CMP_TILE = 256
SEL_GROUP = 8
ONES_PAD = 16
PSUM_PAD = 8
LOG2_E = 1.4426950408889634
FF_TILE = 1408
VMEM_LIMIT = 56 * 1024 * 1024


def _params(semantics, vmem=VMEM_LIMIT):
    return pltpu.CompilerParams(dimension_semantics=semantics, vmem_limit_bytes=vmem)


def _dot(a, b):
    return jnp.dot(a, b, preferred_element_type=F32)


def _dot_nt(a, b):
    return lax.dot_general(a, b, (((1,), (1,)), ((), ())), preferred_element_type=F32)


def _store_token_tiles(ref, v):
    n = v.shape[0]
    for s in range(TOKEN_TILE):
        ref[pl.ds(s, n, stride=TOKEN_TILE), :] = v[:, s * LANE:(s + 1) * LANE]


def _load_token_tiles(ref):
    n = ref.shape[0] // TOKEN_TILE
    return jnp.concatenate([ref[pl.ds(s, n, stride=TOKEN_TILE), :] for s in range(TOKEN_TILE)], axis=1)


def _layer_norm_rows(v, g, b):
    mu = jnp.mean(v, axis=-1, keepdims=True)
    c = v - mu
    var = jnp.mean(c * c, axis=-1, keepdims=True)
    return c * lax.rsqrt(var + LN_EPS) * g + b


def _rope_tables(positions):
    half = ROT_DIM // 2
    inv = ROPE_THETA ** (-jnp.arange(half, dtype=F32) / half)
    ang = positions.astype(F32)[:, :, None] * inv
    cos, sin = jnp.cos(ang), jnp.sin(ang)
    rest = HEAD_DIM - ROT_DIM
    one = jnp.ones(ang.shape[:2] + (rest,), F32)
    zero_r = jnp.zeros(ang.shape[:2] + (rest,), F32)
    zero_h = jnp.zeros_like(sin)
    c = jnp.concatenate([cos, cos, one], -1)
    s1 = jnp.concatenate([zero_h, sin, zero_r], -1)
    s2 = jnp.concatenate([-sin, zero_h, zero_r], -1)
    reps = LANE // HEAD_DIM
    lane_tabs = tuple(jnp.tile(t, (1, 1, reps)).reshape(-1, LANE) for t in (c, s1, s2))
    return lane_tabs + (cos.reshape(-1, half).T, sin.reshape(-1, half).T)


def _proj_kernel(x_ref, wrow_ref, wkey_ref, wq_ref, wval_ref, wgate_ref, c_ref, s1_ref, s2_ref, cos_ref, sin_ref,
                 mix_ref, br_ref, qT_ref, keys_ref, cmp_ref, valT_ref, gateT_ref, *, tiles_per_seq):
    tm = x_ref.shape[0]
    half = ROT_DIM // 2
    xb = x_ref[...].astype(BF16)

    for ref, col0 in ((mix_ref, 0), (br_ref, mix_ref.shape[1])):
        for col in range(0, ref.shape[1], PROJ_BLOCK):
            ref[:, col:col + PROJ_BLOCK] = _dot(xb, wrow_ref[:, col0 + col:col0 + col + PROJ_BLOCK]).astype(ref.dtype)

    def rope(v):
        return (v * c_ref[...] + pltpu.roll(v, half, 1) * s1_ref[...]
                + pltpu.roll(v, LANE - half, 1) * s2_ref[...])

    tpos = (pl.program_id(0) % tiles_per_seq) * tm + lax.broadcasted_iota(jnp.int32, (tm, LANE), 0)
    lane = lax.broadcasted_iota(jnp.int32, (tm, LANE), 1)
    onehot = jnp.where(lane == HEAD_DIM + (tpos // SLC_LEN) % SEL_GROUP, 1.0, 0.0)
    for k in range(4):
        acc = _dot(xb, wkey_ref[:, k * 2 * LANE:(k + 1) * 2 * LANE])
        for h in range(NSA_KV_HEADS):
            v = acc[:, h * LANE:(h + 1) * LANE]
            if k < 3:
                v = rope(v)
            if k == 0:
                cmp_ref[0, h] = v
            elif k == 1:
                keys_ref[0, h] = (v + onehot).astype(keys_ref.dtype)
            elif k == 2:
                keys_ref[1, h] = v.astype(keys_ref.dtype)
            else:
                cmp_ref[1, h] = v

    cos, sin = cos_ref[...], sin_ref[...]
    q_scale = HEAD_DIM ** -0.5 * LOG2_E
    for h in range(NSA_KV_HEADS):
        qt = _dot_nt(wq_ref[h], xb)
        for g in range(GROUP):
            blk = qt[g * HEAD_DIM:(g + 1) * HEAD_DIM]
            x1, x2 = blk[:half], blk[half:ROT_DIM]
            y = jnp.concatenate([x1 * cos - x2 * sin, x2 * cos + x1 * sin, blk[ROT_DIM:]], axis=0) * q_scale
            y = y.astype(qT_ref.dtype)
            for nn in range(tm // Q_BLOCK):
                dst = nn * GROUP * Q_BLOCK + g * Q_BLOCK
                qT_ref[h, :, dst:dst + Q_BLOCK] = y[:, nn * Q_BLOCK:(nn + 1) * Q_BLOCK]

    vt = _dot_nt(wval_ref[...], xb)
    ones_pad = jnp.where(lax.broadcasted_iota(jnp.int32, (ONES_PAD, tm), 0) == 0, 1.0, 0.0).astype(valT_ref.dtype)
    for k in range(2):
        for h in range(NSA_KV_HEADS):
            row = (k * NSA_KV_HEADS + h) * HEAD_DIM
            valT_ref[k, h, :HEAD_DIM, :] = vt[row:row + HEAD_DIM].astype(valT_ref.dtype)
            valT_ref[k, h, HEAD_DIM:, :] = ones_pad

    gt = _dot_nt(wgate_ref[...], xb)
    for h in range(NSA_KV_HEADS):
        gateT_ref[h] = gt[h * 16:(h + 1) * 16].astype(gateT_ref.dtype)


def _proj(x2, weights, tabs, b, seq, tm):
    t = x2.shape[0]
    tps = seq // tm
    wrow, wkey, wq, wval, wgate = weights
    h = NSA_KV_HEADS
    n_mix = POOL_WIDTH + 2 * SG_WIDTH
    row = lambda i: (i, 0)
    const = lambda shape: pl.BlockSpec(shape, lambda i: (0,) * len(shape), pipeline_mode=pl.Buffered(1))
    return pl.pallas_call(
        functools.partial(_proj_kernel, tiles_per_seq=tps),
        grid=(t // tm,),
        in_specs=[pl.BlockSpec((tm, D_MODEL), row), const(wrow.shape), const(wkey.shape), const(wq.shape),
                  const(wval.shape), const(wgate.shape),
                  pl.BlockSpec((tm, LANE), row), pl.BlockSpec((tm, LANE), row), pl.BlockSpec((tm, LANE), row),
                  pl.BlockSpec((ROT_DIM // 2, tm), lambda i: (0, i)), pl.BlockSpec((ROT_DIM // 2, tm), lambda i: (0, i))],
        out_specs=[pl.BlockSpec((tm, n_mix), row), pl.BlockSpec((tm, 3 * D_MODEL), row),
                   pl.BlockSpec((None, h, HEAD_DIM, tm * GROUP), lambda i: (i // tps, 0, 0, i % tps)),
                   pl.BlockSpec((2, None, h, tm, LANE), lambda i: (0, i // tps, 0, i % tps, 0)),
                   pl.BlockSpec((2, None, h, tm, LANE), lambda i: (0, i // tps, 0, i % tps, 0)),
                   pl.BlockSpec((2, None, h, HEAD_DIM + ONES_PAD, tm), lambda i: (0, i // tps, 0, 0, i % tps)),
                   pl.BlockSpec((None, h, 16, tm), lambda i: (i // tps, 0, 0, i % tps))],
        out_shape=[jax.ShapeDtypeStruct((t, n_mix), BF16), jax.ShapeDtypeStruct((t, 3 * D_MODEL), BF16),
                   jax.ShapeDtypeStruct((b, h, HEAD_DIM, seq * GROUP), BF16),
                   jax.ShapeDtypeStruct((2, b, h, seq, LANE), BF16),
                   jax.ShapeDtypeStruct((2, b, h, seq, LANE), F32),
                   jax.ShapeDtypeStruct((2, b, h, HEAD_DIM + ONES_PAD, seq), BF16),
                   jax.ShapeDtypeStruct((b, h, 16, seq), BF16)],
        compiler_params=_params(("parallel",)),
        name="proj",
    )(x2, wrow, wkey, wq, wval, wgate, *tabs)


def _prep_w_in(w):
    d = w.shape[0]
    h = NSA_KV_HEADS
    n_mix = POOL_WIDTH + 2 * SG_WIDTH
    kv0 = n_mix + Q_WIDTH
    n_gate = NSA_Q_HEADS * 3
    gate0 = kv0 + 6 * KV_WIDTH
    kv = lambda k: w[:, kv0 + k * KV_WIDTH:kv0 + (k + 1) * KV_WIDTH]
    wrow = jnp.concatenate([w[:, :n_mix], w[:, gate0 + n_gate:]], axis=1)
    pad_heads = lambda m: jnp.pad(m.reshape(d, h, HEAD_DIM), ((0, 0), (0, 0), (0, LANE - HEAD_DIM))).reshape(d, h * LANE)
    wkey = jnp.concatenate([pad_heads(kv(0)), pad_heads(kv(2)), pad_heads(kv(4)), pad_heads(kv(1))], axis=1)
    wq = w[:, n_mix:kv0].T.reshape(h, GROUP * HEAD_DIM, d)
    wval = jnp.concatenate([kv(3), kv(5)], axis=1).T
    wgate = jnp.pad(w[:, gate0:gate0 + n_gate].T.reshape(h, GROUP * 3, d), ((0, 0), (0, 16 - GROUP * 3), (0, 0)))
    return tuple(m.astype(BF16) for m in (wrow, wkey, wq, wval, wgate.reshape(h * 16, d)))


def _mixers_kernel(a_ref, ap_ref, u_ref, v_ref, pw_ref, ps_ref, lg_ref, lb_ref, sw_ref, sb_ref,
                   pm_ref, sg_ref, *, tiles_per_seq):
    i = pl.program_id(0)
    tm = a_ref.shape[0]
    it = i % tiles_per_seq
    gw = POOL_WIDTH // len(POOL_WINDOWS)

    a = a_ref[...].astype(F32)
    prev = jnp.where(it == 0, 0.0, ap_ref[...].astype(F32))
    ext = jnp.concatenate([prev, a], axis=0)
    tpos = (it * tm + lax.broadcasted_iota(jnp.int32, (tm, 1), 0) + 1).astype(F32)
    for g, w in enumerate(POOL_WINDOWS):
        s = ext[:, g * gw:(g + 1) * gw]
        k = 1
        while k < w:
            s = s + pltpu.roll(s, k, 0)
            k *= 2
        d = s[POOL_HALO:] / jnp.minimum(tpos, float(w)) - a[:, g * gw:(g + 1) * gw]
        y = _dot(d.astype(BF16), pw_ref[g]) * ps_ref[:, g * gw:(g + 1) * gw]
        pm_ref[:, g * gw:(g + 1) * gw] = y.astype(pm_ref.dtype)

    u = jax.nn.gelu(u_ref[...].astype(F32))
    v = _layer_norm_rows(jax.nn.gelu(v_ref[...].astype(F32)), lg_ref[...], lb_ref[...]).astype(BF16)
    hc = SG_WIDTH // SG_HEADS
    tri = (lax.broadcasted_iota(jnp.int32, (SG_CHUNK, SG_CHUNK), 0)
           >= lax.broadcasted_iota(jnp.int32, (SG_CHUNK, SG_CHUNK), 1))
    for g in range(SG_HEADS):
        ws = jnp.where(tri, sw_ref[g], 0.0).astype(BF16)
        bias = sb_ref[:, g:g + 1]
        for c in range(tm // SG_CHUNK):
            rows = slice(c * SG_CHUNK, (c + 1) * SG_CHUNK)
            cols = slice(g * hc, (g + 1) * hc)
            mixed = _dot(ws, v[rows, cols]) + bias
            sg_ref[rows, cols] = (u[rows, cols] * mixed).astype(sg_ref.dtype)


def _mixers(mix, pool_w, pool_scale, ln_g, ln_b, sg_w, sg_b, seq, tm):
    t = mix.shape[0]
    halo_blocks = tm // POOL_HALO
    full = lambda shape: pl.BlockSpec(shape, lambda i: (0,) * len(shape))
    return pl.pallas_call(
        functools.partial(_mixers_kernel, tiles_per_seq=seq // tm),
        grid=(t // tm,),
        in_specs=[pl.BlockSpec((tm, POOL_WIDTH), lambda i: (i, 0)),
                  pl.BlockSpec((POOL_HALO, POOL_WIDTH), lambda i: (jnp.maximum(i * halo_blocks - 1, 0), 0)),
                  pl.BlockSpec((tm, SG_WIDTH), lambda i: (i, 1)),
                  pl.BlockSpec((tm, SG_WIDTH), lambda i: (i, 2)),
                  full(pool_w.shape), full((1, POOL_WIDTH)), full((1, SG_WIDTH)), full((1, SG_WIDTH)),
                  full(sg_w.shape), full((SG_CHUNK, SG_HEADS))],
        out_specs=[pl.BlockSpec((tm, POOL_WIDTH), lambda i: (i, 0)),
                   pl.BlockSpec((tm, SG_WIDTH), lambda i: (i, 0))],
        out_shape=[jax.ShapeDtypeStruct((t, POOL_WIDTH), BF16), jax.ShapeDtypeStruct((t, SG_WIDTH), BF16)],
        compiler_params=_params(("parallel",)),
        name="mixers",
    )(mix, mix, mix, mix, pool_w.astype(BF16), pool_scale[None], ln_g[None], ln_b[None], sg_w, sg_b.T)


def _compress_kernel(src_ref, pos_ref, w1_ref, w2_ref, w2T_ref, o_ref, oT_ref):
    nc = o_ref.shape[0]
    first = jnp.zeros((nc, CMP_HIDDEN), F32)
    second = jnp.zeros((nc, CMP_HIDDEN), F32)
    for p in range(CMP_STRIDE):
        rows = src_ref[pl.ds(p, nc, stride=CMP_STRIDE), :]
        first = first + _dot((rows + pos_ref[p:p + 1, :]).astype(BF16), w1_ref[p])
        q = CMP_STRIDE + p
        second = second + _dot((rows + pos_ref[q:q + 1, :]).astype(BF16), w1_ref[q])
    pre = first + pltpu.roll(second, nc - 1, 0)
    hidden = jax.nn.gelu(pre).astype(BF16)
    o_ref[...] = _dot(hidden, w2_ref[...]).astype(o_ref.dtype)
    oT_ref[...] = _dot_nt(w2T_ref[...], hidden).astype(oT_ref.dtype)


def _compress(src, pos, w1, w2, w2T):
    _, b, h, seq, _ = src.shape
    nc = seq // CMP_STRIDE
    sel = lambda n: pl.BlockSpec((None,) + n, lambda s, bi, hi: (s,) + (0,) * len(n))
    return pl.pallas_call(
        _compress_kernel,
        grid=(2, b, h),
        in_specs=[pl.BlockSpec((None, None, None, seq, LANE), lambda s, bi, hi: (s, bi, hi, 0, 0)),
                  sel(pos.shape[1:]), sel(w1.shape[1:]), sel(w2.shape[1:]), sel(w2T.shape[1:])],
        out_specs=[pl.BlockSpec((None, None, None, nc, HEAD_DIM), lambda s, bi, hi: (s, bi, hi, 0, 0)),
                   pl.BlockSpec((None, None, None, HEAD_DIM, nc), lambda s, bi, hi: (s, bi, hi, 0, 0))],
        out_shape=[jax.ShapeDtypeStruct((2, b, h, nc, HEAD_DIM), BF16),
                   jax.ShapeDtypeStruct((2, b, h, HEAD_DIM, nc), BF16)],
        compiler_params=_params(("parallel", "parallel", "parallel")),
        name="compress",
    )(src, pos, w1, w2, w2T)


def _prep_compress(k_pos, k_w1, k_w2, v_pos, v_w1, v_w2):
    pad = LANE - HEAD_DIM
    pos = jnp.pad(jnp.stack([k_pos, v_pos]), ((0, 0), (0, 0), (0, pad)))
    w1 = jnp.stack([k_w1, v_w1]).reshape(2, CMP_LEN, HEAD_DIM, CMP_HIDDEN)
    w1 = jnp.pad(w1, ((0, 0), (0, 0), (0, pad), (0, 0))).astype(BF16)
    w2 = jnp.stack([k_w2, v_w2]).astype(BF16)
    return pos, w1, w2, w2.transpose(0, 2, 1)


def _score_strips(k_ref, k0, n_strips, q_aug, s_ref, mask):
    mx = None
    for r in range(n_strips):
        start = pl.multiple_of(k0 + r * NSA_STRIP, NSA_STRIP)
        s = _dot(k_ref[pl.ds(start, NSA_STRIP), :], q_aug).astype(s_ref.dtype)
        if mask is not None:
            s = mask(r, start, s)
        s_ref[r * NSA_STRIP:(r + 1) * NSA_STRIP, :] = s
        mx = s if mx is None else jnp.maximum(mx, s)
    return jnp.max(mx.astype(F32), axis=0, keepdims=True)


def _prob_strips(n_strips, m, s_ref, p_ref):
    m = m.astype(s_ref.dtype)
    for r in range(n_strips):
        rows = slice(r * NSA_STRIP, (r + 1) * NSA_STRIP)
        p_ref[rows, :] = jnp.exp2(s_ref[rows, :] - m).astype(BF16)


def _nsa_kernel(qT_ref, kc_ref, vcT_ref, ks_ref, vsT_ref, kw_ref, vwT_ref, g_ref, bound_ref, o_ref,
                s_ref, p_ref, pc_ref, s2_ref, p2_ref, psum_ref, selneg_ref, *, n_top):
    n = pl.program_id(2)
    qs = n * Q_BLOCK
    width = GROUP * Q_BLOCK
    qT = qT_ref[...]
    t_row = qs + lax.broadcasted_iota(jnp.int32, (1, width), 1) % Q_BLOCK
    nc = kc_ref.shape[0]
    ns = selneg_ref.shape[0]
    strip_iota = lax.broadcasted_iota(jnp.int32, (NSA_STRIP, 1), 0)

    per = SLC_LEN // CMP_STRIDE
    psum_ref[:PSUM_PAD, :] = jnp.zeros((PSUM_PAD, Q_BLOCK), F32)

    def compressed_and_selection(n_tiles):
        keys = n_tiles * CMP_TILE
        rows = keys // per
        strip_rows = [slice(r * NSA_STRIP, (r + 1) * NSA_STRIP) for r in range(keys // NSA_STRIP)]

        mx = None
        for sl in strip_rows:
            s = _dot(kc_ref[sl, :], qT)
            c_end = (sl.start + strip_iota) * CMP_STRIDE + (CMP_LEN - 1)
            s = jnp.where(c_end <= t_row, s, NEG_INF)
            s_ref[sl, :] = s
            mx = s if mx is None else jnp.maximum(mx, s)
        m_cmp = jnp.maximum(jnp.max(mx, axis=0, keepdims=True), 0.1 * NEG_INF)
        lsum = jnp.zeros((NSA_STRIP, width), F32)
        for sl in strip_rows:
            e = jnp.exp2(s_ref[sl, :] - m_cmp)
            s_ref[sl, :] = e
            pc_ref[sl, :] = e.astype(BF16)
            lsum = lsum + e
        l = jnp.sum(lsum, axis=0, keepdims=True)
        inv = jnp.where(l > 0.0, 1.0 / l, 0.0)
        out = _dot(vcT_ref[:, :keys], pc_ref[:keys, :]) * inv
        for sl in strip_rows:
            p = s_ref[sl, :] * inv
            psum = p[:, :Q_BLOCK]
            for g in range(1, GROUP):
                psum = psum + p[:, g * Q_BLOCK:(g + 1) * Q_BLOCK]
            psum_ref[PSUM_PAD + sl.start:PSUM_PAD + sl.stop, :] = psum

        part = [psum_ref[pl.ds(PSUM_PAD + k, rows, stride=per), :] for k in range(per)]
        before = psum_ref[pl.ds(PSUM_PAD - 1, rows, stride=per), :]
        imp = 2.0 * (part[0] + part[1] + part[2]) + part[3] + before

        blk = lax.broadcasted_iota(jnp.int32, (rows, Q_BLOCK), 0)
        cur = (qs + lax.broadcasted_iota(jnp.int32, (rows, Q_BLOCK), 1)) // SLC_LEN
        taken = -2.0
        score = jnp.where(blk <= cur, imp, -1.0)
        score = jnp.where(blk == 0, taken, jnp.where(blk == cur, taken, jnp.where(blk == cur - 1, taken, score)))
        blk_f = blk.astype(F32)
        for _ in range(n_top - 3):
            top = jnp.max(score, axis=0, keepdims=True)
            cand = jnp.where(score == top, blk_f, float(rows))
            first = jnp.min(cand, axis=0, keepdims=True)
            score = jnp.where(cand == first, taken, score)
        selneg_ref[:rows, :] = jnp.where(score == taken, 0.0, NEG_INF)
        if rows < ns:
            selneg_ref[rows:, :] = jnp.full((ns - rows, Q_BLOCK), NEG_INF, F32)
        return out

    reach = CMP_TILE * CMP_STRIDE // Q_BLOCK
    o_cmp = lax.switch(n // reach, [functools.partial(compressed_and_selection, t + 1) for t in range(nc // CMP_TILE)])

    strips = SLC_BLOCK // NSA_STRIP
    sel_rows = ks_ref.shape[1] - HEAD_DIM - SEL_GROUP
    last_block = ks_ref.shape[0] // SLC_BLOCK - 1
    n_diag = n // (SLC_BLOCK // Q_BLOCK)

    def past_only(r, start, s):
        return jnp.where(start + strip_iota <= t_row, s, NEG_INF)

    def slc_scores(kb, slot):
        per_block = SLC_BLOCK // Q_BLOCK
        variant = jnp.where(kb < n_diag, 0, jnp.where(kb > n_diag, 1 + per_block, 1 + n % per_block))

        def limit(r, start, s):
            return jnp.minimum(s, bound_ref[variant, r * NSA_STRIP:(r + 1) * NSA_STRIP, :])

        kb = jnp.minimum(kb, last_block)
        k0 = pl.multiple_of(kb * SLC_BLOCK, SLC_BLOCK)
        g0 = pl.multiple_of((kb * strips // SEL_GROUP) * SEL_GROUP, SEL_GROUP)
        bias = jnp.concatenate([selneg_ref[pl.ds(g0, SEL_GROUP), :]] * GROUP, axis=1)
        bias = jnp.concatenate([bias, jnp.zeros((sel_rows, width), F32)], axis=0).astype(BF16)
        q_aug = jnp.concatenate([qT, bias], axis=0)
        return _score_strips(ks_ref, k0, strips, q_aug, s2_ref.at[slot], limit)

    def slc_consume(kb, slot, m, acc, mx):
        k0 = pl.multiple_of(kb * SLC_BLOCK, SLC_BLOCK)
        m_new = jnp.maximum(m, mx)
        _prob_strips(strips, m_new, s2_ref.at[slot], p2_ref.at[slot])
        pv = _dot(vsT_ref[:, pl.ds(k0, SLC_BLOCK)], p2_ref[slot])
        return m_new, jnp.exp2(m - m_new) * acc + pv

    def slc_group(j, carry):
        m, acc, pending = carry[0], carry[1], list(carry[2:])
        for u in range(SLC_UNROLL):
            kb = SLC_UNROLL * j + u
            pending.append(slc_scores(kb + SLC_AHEAD, (u + SLC_AHEAD) % SLC_UNROLL))
            m, acc = slc_consume(kb, u, m, acc, pending.pop(0))
        return (m, acc, *pending)

    init = (jnp.full((1, width), NEG_INF, F32), jnp.zeros((vsT_ref.shape[0], width), F32),
            *[slc_scores(a, a) for a in range(SLC_AHEAD)])
    acc = lax.fori_loop(0, (n_diag + SLC_UNROLL) // SLC_UNROLL, slc_group, init)[1]
    o_slc = acc[:HEAD_DIM] / acc[HEAD_DIM:HEAD_DIM + 1]

    span = WINDOW + Q_BLOCK
    w_strips = span // NSA_STRIP
    edge = Q_BLOCK // NSA_STRIP
    w0 = pl.multiple_of(jnp.maximum(qs - WINDOW, 0), Q_BLOCK)

    def band_edges(r, start, s):
        if r < edge:
            return jnp.where(t_row - (start + strip_iota) < WINDOW, s, NEG_INF)
        if r >= w_strips - edge:
            return past_only(r, start, s)
        return s

    q_win = jnp.concatenate([qT, jnp.zeros((kw_ref.shape[1] - HEAD_DIM, width), BF16)], axis=0)
    m_win = lax.cond(n >= WINDOW // Q_BLOCK,
                     lambda: _score_strips(kw_ref, w0, w_strips, q_win, s_ref, band_edges),
                     lambda: _score_strips(kw_ref, w0, w_strips, q_win, s_ref, past_only))
    _prob_strips(w_strips, m_win, s_ref, p_ref)
    pv = _dot(vwT_ref[:, pl.ds(w0, span)], p_ref[:span, :])
    o_win = pv[:HEAD_DIM] / pv[HEAD_DIM:HEAD_DIM + 1]

    gates = jax.nn.sigmoid(g_ref[...].astype(F32))

    def gate(c):
        return jnp.concatenate([gates[g * 3 + c:g * 3 + c + 1, :] for g in range(GROUP)], axis=1)

    o = gate(0) * o_cmp + gate(1) * o_slc + gate(2) * o_win
    for g in range(GROUP):
        o_ref[:, g * HEAD_DIM:(g + 1) * HEAD_DIM] = o[:, g * Q_BLOCK:(g + 1) * Q_BLOCK].T.astype(o_ref.dtype)


def _causal_bounds():
    big = np.float32(3.0e38)
    key = np.arange(SLC_BLOCK)[:, None]
    query = np.tile(np.arange(Q_BLOCK), GROUP)[None, :]
    tables = [np.full((SLC_BLOCK, GROUP * Q_BLOCK), big, np.float32)]
    for v in range(SLC_BLOCK // Q_BLOCK):
        tables.append(np.where(key <= query + v * Q_BLOCK, big, np.float32(NEG_INF)).astype(np.float32))
    tables.append(np.full((SLC_BLOCK, GROUP * Q_BLOCK), NEG_INF, np.float32))
    return jnp.asarray(np.stack(tables), BF16)


def _nsa(qT, cmp, cmpT, keys, valT, gates, seq):
    b, h = qT.shape[:2]
    nq = seq // Q_BLOCK
    nc = cmp.shape[3]
    ns = seq // SLC_LEN
    width = GROUP * Q_BLOCK
    assert nc == ns * (SLC_LEN // CMP_STRIDE) and nc % CMP_TILE == 0 and seq % (SLC_BLOCK * SLC_UNROLL) == 0
    assert seq >= WINDOW + Q_BLOCK and ns >= SLC_TOP
    bounds = _causal_bounds()
    per_head = lambda arr, k: pl.BlockSpec((None, None, None) + arr.shape[3:], lambda bi, hi, n, k=k: (k, bi, hi, 0, 0))
    return pl.pallas_call(
        functools.partial(_nsa_kernel, n_top=SLC_TOP),
        grid=(b, h, nq),
        in_specs=[pl.BlockSpec((None, None, HEAD_DIM, width), lambda bi, hi, n: (bi, hi, 0, n)),
                  per_head(cmp, 0), per_head(cmpT, 1),
                  per_head(keys, 0), per_head(valT, 0), per_head(keys, 1), per_head(valT, 1),
                  pl.BlockSpec((None, None, 16, Q_BLOCK), lambda bi, hi, n: (bi, hi, 0, n)),
                  pl.BlockSpec(bounds.shape, lambda bi, hi, n: (0, 0, 0), pipeline_mode=pl.Buffered(1))],
        out_specs=pl.BlockSpec((Q_BLOCK, GROUP * HEAD_DIM), lambda bi, hi, n: (bi * nq + n, hi)),
        out_shape=jax.ShapeDtypeStruct((b * seq, Q_WIDTH), BF16),
        scratch_shapes=[pltpu.VMEM((max(nc, WINDOW + Q_BLOCK), width), F32),
                        pltpu.VMEM((WINDOW + Q_BLOCK, width), BF16),
                        pltpu.VMEM((nc, width), BF16),
                        pltpu.VMEM((SLC_UNROLL, SLC_BLOCK, width), BF16),
                        pltpu.VMEM((SLC_UNROLL, SLC_BLOCK, width), BF16),
                        pltpu.VMEM((PSUM_PAD + nc, Q_BLOCK), F32),
                        pltpu.VMEM((ns, Q_BLOCK), F32)],
        compiler_params=_params(("parallel", "parallel", "arbitrary")),
        name="nsa",
    )(qT, cmp, cmpT, keys, valT, keys, valT, gates, bounds)


def _merge_kernel(*refs, with_router, alpha):
    (pm_ref, sg_ref, on_ref, gp_ref, gs_ref, gn_ref, x_ref, wp_ref, ws_ref, wn_ref, wo_ref, lg_ref, lb_ref) = refs[:13]
    if with_router:
        wr_ref, br_ref, x1_ref, route_ref, x1t_ref = refs[13:]
    else:
        (x1_ref,) = refs[13:]
    sig = lambda r: jax.nn.sigmoid(r[...].astype(F32))
    y = (sig(gp_ref) * _dot(pm_ref[...], wp_ref[...]) + sig(gs_ref) * _dot(sg_ref[...], ws_ref[...])
         + sig(gn_ref) * _dot(on_ref[...], wn_ref[...]))
    hmix = _dot(y.astype(BF16), wo_ref[...])
    x1 = _layer_norm_rows(alpha * x_ref[...] + hmix, lg_ref[...], lb_ref[...])
    x1_ref[...] = x1
    if with_router:
        _store_token_tiles(x1t_ref, x1)
        wr = wr_ref[...]
        w_hi = wr.astype(BF16)
        w_lo = (wr - w_hi.astype(F32)).astype(BF16)
        x_hi = x1.astype(BF16)
        x_lo = (x1 - x_hi.astype(F32)).astype(BF16)
        logits = _dot(x_hi, w_hi) + (_dot(x_lo, w_hi) + _dot(x_hi, w_lo)) + br_ref[...]
        lane = lax.broadcasted_iota(jnp.int32, logits.shape, 1)
        lane_f = lane.astype(F32)
        logits = jnp.where(lane < N_EXPERTS, logits, -jnp.inf)
        m1 = jnp.max(logits, axis=-1, keepdims=True)
        i1 = jnp.min(jnp.where(logits == m1, lane_f, float(LANE)), axis=-1, keepdims=True)
        rest = jnp.where(lane_f == i1, -jnp.inf, logits)
        m2 = jnp.max(rest, axis=-1, keepdims=True)
        i2 = jnp.min(jnp.where(rest == m2, lane_f, float(LANE)), axis=-1, keepdims=True)
        e2 = jnp.exp(m2 - m1)
        w1 = 1.0 / (1.0 + e2)
        route_ref[...] = jnp.where(lane == 0, i1, jnp.where(lane == 1, i2, jnp.where(lane == 2, w1,
                                   jnp.where(lane == 3, e2 * w1, 0.0))))


def _merge(pm, sgm, on, br, x2, wp, ws, wn, wo, lg, lb, alpha, tm, router=None):
    t = x2.shape[0]
    row = lambda width, cb=0: pl.BlockSpec((tm, width), lambda i, cb=cb: (i, cb))
    full = lambda shape: pl.BlockSpec(shape, lambda i: (0,) * len(shape))
    in_specs = [row(POOL_WIDTH), row(SG_WIDTH), row(Q_WIDTH),
                row(D_MODEL, 0), row(D_MODEL, 1), row(D_MODEL, 2),
                row(D_MODEL), full(wp.shape), full(ws.shape), full(wn.shape), full(wo.shape),
                full((1, D_MODEL)), full((1, D_MODEL))]
    args = [pm, sgm, on, br, br, br, x2, wp.astype(BF16), ws.astype(BF16), wn.astype(BF16), wo.astype(BF16),
            lg[None], lb[None]]
    out_specs = [row(D_MODEL)]
    out_shape = [jax.ShapeDtypeStruct((t, D_MODEL), F32)]
    if router is not None:
        w_router, b_router = router
        pad = LANE - N_EXPERTS
        in_specs += [full((D_MODEL, LANE)), full((1, LANE))]
        args += [jnp.pad(w_router, ((0, 0), (0, pad))), jnp.pad(b_router, (0, pad))[None]]
        out_specs += [row(LANE), pl.BlockSpec((tm * TOKEN_TILE, LANE), lambda i: (i, 0))]
        out_shape += [jax.ShapeDtypeStruct((t, LANE), F32), jax.ShapeDtypeStruct((t * TOKEN_TILE, LANE), F32)]
    out = pl.pallas_call(
        functools.partial(_merge_kernel, with_router=router is not None, alpha=alpha),
        grid=(t // tm,), in_specs=in_specs, out_specs=out_specs, out_shape=out_shape,
        compiler_params=_params(("parallel",)),
        name="merge",
    )(*args)
    return out if router is not None else out[0]


def _tile_rows(i):
    start = i * TOKEN_TILE if isinstance(i, int) else pl.multiple_of(i * TOKEN_TILE, TOKEN_TILE)
    return pl.ds(start, TOKEN_TILE)


def _for_rows(n, fn):
    def body(r, c):
        fn(r)
        return c
    lax.fori_loop(0, n, body, 0, unroll=8)


def _moe_kernel(be_ref, tok_ref, tok_next_ref, slot_ref, x_hbm, wg_ref, wu_ref, wd_ref, y_hbm,
                xg_ref, xb_ref, acc_ref, yt_ref, gather_sem, scatter_sem):
    i = pl.program_id(0)
    k = pl.program_id(1)
    n_blocks = pl.num_programs(0)
    rows = xb_ref.shape[0]

    def gather(idx_ref, slot):
        return lambda r: pltpu.make_async_copy(x_hbm.at[_tile_rows(idx_ref[0, r])], xg_ref.at[slot, _tile_rows(r)],
                                               gather_sem.at[slot])

    def scatter(idx_ref):
        return lambda r: pltpu.make_async_copy(yt_ref.at[_tile_rows(r)], y_hbm.at[_tile_rows(idx_ref[0, r])],
                                               scatter_sem)

    def wait_gather(slot):
        pltpu.make_async_copy(x_hbm.at[pl.ds(0, rows * TOKEN_TILE)], xg_ref.at[slot], gather_sem.at[slot]).wait()

    @pl.when(k == 0)
    def _():
        slot = i % 2

        @pl.when(i == 0)
        def _():
            _for_rows(rows, lambda r: gather(tok_ref, 0)(r).start())

        wait_gather(slot)
        for s in range(TOKEN_TILE):
            xb_ref[:, s * LANE:(s + 1) * LANE] = xg_ref[slot, pl.ds(s, rows, stride=TOKEN_TILE), :].astype(BF16)

    share = rows // (D_FF // FF_TILE)
    for r in range(share):
        gather(tok_next_ref, 1 - i % 2)(k * share + r).start()

    xb = xb_ref[...]
    hidden = (jax.nn.silu(_dot(xb, wg_ref[...])) * _dot(xb, wu_ref[...])).astype(BF16)
    part = _dot(hidden, wd_ref[...])

    @pl.when(k == 0)
    def _():
        acc_ref[...] = part

    @pl.when(k == pl.num_programs(1) - 1)
    def _():
        def wait_scatter():
            pltpu.make_async_copy(yt_ref, y_hbm.at[pl.ds(0, rows * TOKEN_TILE)], scatter_sem).wait()

        pl.when(i > 0)(wait_scatter)
        _store_token_tiles(yt_ref, acc_ref[...] + part)
        _for_rows(rows, lambda r: scatter(slot_ref)(r).start())

        @pl.when(i == n_blocks - 1)
        def _():
            wait_scatter()
            wait_gather(1 - i % 2)


def _moe(x_tiles, row_tok, row_slot, n_slots, block_e, wg, wu, wd):
    rows = EXPERT_BLOCK
    n_blocks = row_tok.shape[0] // rows
    kf = D_FF // FF_TILE
    assert kf == 2
    idx_spec = lambda fn: pl.BlockSpec((None, 1, rows), lambda i, k, be: (fn(i), 0, 0), memory_space=pltpu.SMEM)
    tok3 = row_tok.reshape(n_blocks, 1, rows)
    slot3 = row_slot.reshape(n_blocks, 1, rows)
    return pl.pallas_call(
        _moe_kernel,
        grid_spec=pltpu.PrefetchScalarGridSpec(
            num_scalar_prefetch=1,
            grid=(n_blocks, kf),
            in_specs=[idx_spec(lambda i: i), idx_spec(lambda i: jnp.minimum(i + 1, n_blocks - 1)),
                      idx_spec(lambda i: i),
                      pl.BlockSpec(memory_space=pl.ANY),
                      pl.BlockSpec((None, D_MODEL, FF_TILE), lambda i, k, be: (be[i], 0, k)),
                      pl.BlockSpec((None, D_MODEL, FF_TILE), lambda i, k, be: (be[i], 0, k)),
                      pl.BlockSpec((None, FF_TILE, D_MODEL), lambda i, k, be: (be[i], k, 0))],
            out_specs=pl.BlockSpec(memory_space=pl.ANY),
            scratch_shapes=[pltpu.VMEM((2, rows * TOKEN_TILE, LANE), F32), pltpu.VMEM((rows, D_MODEL), BF16),
                            pltpu.VMEM((rows, D_MODEL), F32), pltpu.VMEM((rows * TOKEN_TILE, LANE), F32),
                            pltpu.SemaphoreType.DMA((2,)), pltpu.SemaphoreType.DMA(())]),
        out_shape=jax.ShapeDtypeStruct((n_slots * TOKEN_TILE, LANE), F32),
        compiler_params=_params(("arbitrary", "arbitrary")),
        name="moe",
    )(block_e, tok3, tok3, slot3, x_tiles, wg, wu, wd)


def _swiglu_kernel(x_ref, wg_ref, wu_ref, wd_ref, o_ref, xb_ref):
    k = pl.program_id(1)

    @pl.when(k == 0)
    def _():
        xb_ref[...] = x_ref[...].astype(BF16)

    xb = xb_ref[...]
    hidden = (jax.nn.silu(_dot(xb, wg_ref[...])) * _dot(xb, wu_ref[...])).astype(BF16)
    part = _dot(hidden, wd_ref[...])

    @pl.when(k == 0)
    def _():
        o_ref[...] = part

    @pl.when(k != 0)
    def _():
        o_ref[...] += part


def _swiglu(rows, wg, wu, wd, tm):
    r = rows.shape[0]
    row = pl.BlockSpec((tm, D_MODEL), lambda i, k: (i, 0))
    return pl.pallas_call(
        _swiglu_kernel,
        grid=(r // tm, D_FF // FF_TILE),
        in_specs=[row,
                  pl.BlockSpec((D_MODEL, FF_TILE), lambda i, k: (0, k)),
                  pl.BlockSpec((D_MODEL, FF_TILE), lambda i, k: (0, k)),
                  pl.BlockSpec((FF_TILE, D_MODEL), lambda i, k: (k, 0))],
        out_specs=row,
        out_shape=jax.ShapeDtypeStruct(rows.shape, F32),
        scratch_shapes=[pltpu.VMEM((tm, D_MODEL), BF16)],
        compiler_params=_params(("parallel", "arbitrary")),
        name="swiglu",
    )(rows, wg, wu, wd)


def _ple_ln2_kernel(*refs, n_parts, alpha):
    x_ref, p_ref = refs[0], refs[1]
    parts = refs[2:2 + n_parts]
    rest = refs[2 + n_parts:]
    if n_parts > 1:
        route_ref, rest = rest[0], rest[1:]
    wg_ref, bg_ref, wp_ref, lg_ref, lb_ref, o_ref = rest
    x1 = x_ref[...]
    if n_parts > 1:
        route = route_ref[...]
        f = _load_token_tiles(parts[0]) * route[:, TOP_K:TOP_K + 1]
        for k in range(1, n_parts):
            f = f + _load_token_tiles(parts[k]) * route[:, TOP_K + k:TOP_K + k + 1]
    else:
        f = parts[0][...]
    gate = jax.nn.sigmoid(_dot(x1.astype(BF16), wg_ref[...]) + bg_ref[...])
    ple = gate * _dot(p_ref[...].astype(BF16), wp_ref[...])
    o_ref[...] = _layer_norm_rows(alpha * x1 + f + ple, lg_ref[...], lb_ref[...])


def _ple_ln2(x1, p2, y, route, wg, bg, wp, lg, lb, alpha, tm):
    t = x1.shape[0]
    n_parts = 1 if route is None else TOP_K
    steps = t // tm
    row = lambda width: pl.BlockSpec((tm, width), lambda i: (i, 0))
    full = lambda shape: pl.BlockSpec(shape, lambda i: (0,) * len(shape))
    in_specs = [row(D_MODEL), row(PLE_DIM)]
    if n_parts > 1:
        in_specs += [pl.BlockSpec((tm * TOKEN_TILE, LANE), lambda i, k=k: (k * steps + i, 0)) for k in range(n_parts)]
    else:
        in_specs.append(row(D_MODEL))
    args = [x1, p2] + [y] * n_parts
    if n_parts > 1:
        in_specs.append(row(LANE))
        args.append(route)
    in_specs += [full((D_MODEL, D_MODEL)), full((1, D_MODEL)), full((PLE_DIM, D_MODEL)),
                 full((1, D_MODEL)), full((1, D_MODEL))]
    args += [wg.astype(BF16), bg[None], wp.astype(BF16), lg[None], lb[None]]
    return pl.pallas_call(
        functools.partial(_ple_ln2_kernel, n_parts=n_parts, alpha=alpha),
        grid=(steps,), in_specs=in_specs, out_specs=row(D_MODEL),
        out_shape=jax.ShapeDtypeStruct((t, D_MODEL), F32),
        compiler_params=_params(("parallel",)),
        name="ple_ln2",
    )(*args)


def _route_tables(route, t):
    flat_e = route[:, :TOP_K].astype(jnp.int32).reshape(-1)
    tk = flat_e.shape[0]
    onehot = (flat_e[:, None] == jnp.arange(N_EXPERTS, dtype=jnp.int32)[None, :]).astype(jnp.int32)
    before = jnp.cumsum(onehot, axis=0) - onehot
    rank = jnp.sum(before * onehot, axis=1)
    counts = jnp.sum(onehot, axis=0)
    padded = (counts + EXPERT_BLOCK - 1) // EXPERT_BLOCK * EXPERT_BLOCK
    pend = jnp.cumsum(padded)
    dest = (pend - padded)[flat_e] + rank
    n_blocks = -(-tk // EXPERT_BLOCK) + N_EXPERTS
    n_rows = n_blocks * EXPERT_BLOCK
    row_pair = jnp.full((n_rows,), -1, jnp.int32).at[dest].set(jnp.arange(tk, dtype=jnp.int32))
    is_pad = (row_pair < 0).astype(jnp.int32)
    row_tok = jnp.where(row_pair < 0, 0, row_pair // TOP_K)
    pad_slot = tk + jnp.cumsum(is_pad) - is_pad
    row_slot = jnp.where(row_pair < 0, pad_slot, (row_pair % TOP_K) * t + row_pair // TOP_K).astype(jnp.int32)
    block_start = jnp.arange(n_blocks, dtype=jnp.int32) * EXPERT_BLOCK
    block_e = jnp.sum((pend[None, :] <= block_start[:, None]).astype(jnp.int32), axis=1)
    block_e = jnp.minimum(block_e, N_EXPERTS - 1).astype(jnp.int32)
    return row_tok, row_slot, n_rows, block_e


def kernel(x, p, positions, w_in, pool_w, pool_scale, sg_ln_g, sg_ln_b, sg_w, sg_b, cmp_k_pos, cmp_k_w1, cmp_k_w2, cmp_v_pos, cmp_v_w1, cmp_v_w2, w_pool_out, w_sg_out, w_nsa_out, w_out, ln1_g, ln1_b, ffn_w_gate, ffn_w_up, ffn_w_down, moe_router, moe_router_b, moe_w_gate, moe_w_up, moe_w_down, ple_gate_w, ple_gate_b, ple_proj, ln2_g, ln2_b):
    b, seq, d = x.shape
    depth = w_in.shape[0]
    t = b * seq
    alpha = (2 * depth) ** 0.25
    tm = 512
    tabs = _rope_tables(positions)
    x2 = x.reshape(t, d)
    for i in range(depth):
        mix, br, qT, keys, cmp_src, valT, gates = _proj(x2, _prep_w_in(w_in[i]), tabs, b, seq, tm)
        pm, sgm = _mixers(mix, pool_w[i], pool_scale[i], sg_ln_g[i], sg_ln_b[i], sg_w[i], sg_b[i], seq, tm)
        cmp, cmpT = _compress(cmp_src, *_prep_compress(cmp_k_pos[i], cmp_k_w1[i], cmp_k_w2[i],
                                                       cmp_v_pos[i], cmp_v_w1[i], cmp_v_w2[i]))
        on = _nsa(qT, cmp, cmpT, keys, valT, gates, seq)
        j = i // 2
        moe = i % 2 == 1
        merged = _merge(pm, sgm, on, br, x2, w_pool_out[i], w_sg_out[i], w_nsa_out[i], w_out[i],
                        ln1_g[i], ln1_b[i], alpha, tm, router=(moe_router[j], moe_router_b[j]) if moe else None)
        if moe:
            x1, route, x1_tiles = merged
            row_tok, row_slot, n_slots, block_e = _route_tables(route, t)
            y = _moe(x1_tiles, row_tok, row_slot, n_slots, block_e, moe_w_gate[j].astype(BF16),
                     moe_w_up[j].astype(BF16), moe_w_down[j].astype(BF16))
        else:
            x1, route = merged, None
            y = _swiglu(x1, ffn_w_gate[j].astype(BF16), ffn_w_up[j].astype(BF16), ffn_w_down[j].astype(BF16), tm)
        x2 = _ple_ln2(x1, p[i].reshape(t, PLE_DIM), y, route, ple_gate_w[i], ple_gate_b[i], ple_proj[i],
                      ln2_g[i], ln2_b[i], alpha, tm)
    return x2.reshape(b, seq, d)
```

```python
import functools

import jax
import jax.numpy as jnp
import numpy as np
from jax import lax
from jax.experimental import pallas as pl
from jax.experimental.pallas import tpu as pltpu

F32 = jnp.float32
BF16 = jnp.bfloat16

D_MODEL = 1024
POOL_WIDTH = 512
POOL_WINDOWS = (2, 4, 8, 16)
POOL_HALO = 16
SG_WIDTH = 512
SG_HEADS = 4
SG_CHUNK = 128
NSA_Q_HEADS = 8
NSA_KV_HEADS = 2
GROUP = NSA_Q_HEADS // NSA_KV_HEADS
HEAD_DIM = 64
Q_WIDTH = NSA_Q_HEADS * HEAD_DIM
KV_WIDTH = NSA_KV_HEADS * HEAD_DIM
ROT_DIM = HEAD_DIM // 4
ROPE_THETA = 500000.0
CMP_LEN = 32
CMP_STRIDE = 16
CMP_HIDDEN = 256
SLC_LEN = 64
SLC_TOP = 16
WINDOW = 512
Q_BLOCK = 128
NEG_INF = -1e30
D_FF = 2816
N_EXPERTS = 8
TOP_K = 2
EXPERT_BLOCK = 512
PLE_DIM = 256
LN_EPS = 1e-5

LANE = 128
TOKEN_TILE = D_MODEL // LANE
PROJ_BLOCK = 512
NSA_STRIP = 64
SLC_BLOCK = 256
SLC_UNROLL = 4
SLC_AHEAD = 2
CMP_TILE = 128
SEL_GROUP = 8
ONES_PAD = 16
PSUM_PAD = 8
LOG2_E = 1.4426950408889634
FF_TILE = 1408
VMEM_LIMIT = 56 * 1024 * 1024


def _params(semantics, vmem=VMEM_LIMIT):
    return pltpu.CompilerParams(dimension_semantics=semantics, vmem_limit_bytes=vmem)


def _dot(a, b):
    return jnp.dot(a, b, preferred_element_type=F32)


def _dot_nt(a, b):
    return lax.dot_general(a, b, (((1,), (1,)), ((), ())), preferred_element_type=F32)


def _store_token_tiles(ref, v):
    n = v.shape[0]
    for s in range(TOKEN_TILE):
        ref[pl.ds(s, n, stride=TOKEN_TILE), :] = v[:, s * LANE:(s + 1) * LANE]


def _load_token_tiles(ref):
    n = ref.shape[0] // TOKEN_TILE
    return jnp.concatenate([ref[pl.ds(s, n, stride=TOKEN_TILE), :] for s in range(TOKEN_TILE)], axis=1)


def _layer_norm_rows(v, g, b):
    mu = jnp.mean(v, axis=-1, keepdims=True)
    c = v - mu
    var = jnp.mean(c * c, axis=-1, keepdims=True)
    return c * lax.rsqrt(var + LN_EPS) * g + b


def _rope_tables(positions):
    half = ROT_DIM // 2
    inv = ROPE_THETA ** (-jnp.arange(half, dtype=F32) / half)
    ang = positions.astype(F32)[:, :, None] * inv
    cos, sin = jnp.cos(ang), jnp.sin(ang)
    rest = HEAD_DIM - ROT_DIM
    one = jnp.ones(ang.shape[:2] + (rest,), F32)
    zero_r = jnp.zeros(ang.shape[:2] + (rest,), F32)
    zero_h = jnp.zeros_like(sin)
    c = jnp.concatenate([cos, cos, one], -1)
    s1 = jnp.concatenate([zero_h, sin, zero_r], -1)
    s2 = jnp.concatenate([-sin, zero_h, zero_r], -1)
    reps = LANE // HEAD_DIM
    lane_tabs = tuple(jnp.tile(t, (1, 1, reps)).reshape(-1, LANE) for t in (c, s1, s2))
    return lane_tabs + (cos.reshape(-1, half).T, sin.reshape(-1, half).T)


def _proj_kernel(x_ref, wrow_ref, wkey_ref, wq_ref, wval_ref, wgate_ref, c_ref, s1_ref, s2_ref, cos_ref, sin_ref,
                 mix_ref, br_ref, qT_ref, keys_ref, cmp_ref, valT_ref, gateT_ref, *, tiles_per_seq):
    tm = x_ref.shape[0]
    half = ROT_DIM // 2
    xb = x_ref[...].astype(BF16)

    for ref, col0 in ((mix_ref, 0), (br_ref, mix_ref.shape[1])):
        for col in range(0, ref.shape[1], PROJ_BLOCK):
            ref[:, col:col + PROJ_BLOCK] = _dot(xb, wrow_ref[:, col0 + col:col0 + col + PROJ_BLOCK]).astype(ref.dtype)

    def rope(v):
        return (v * c_ref[...] + pltpu.roll(v, half, 1) * s1_ref[...]
                + pltpu.roll(v, LANE - half, 1) * s2_ref[...])

    tpos = (pl.program_id(0) % tiles_per_seq) * tm + lax.broadcasted_iota(jnp.int32, (tm, LANE), 0)
    lane = lax.broadcasted_iota(jnp.int32, (tm, LANE), 1)
    onehot = jnp.where(lane == HEAD_DIM + (tpos // SLC_LEN) % SEL_GROUP, 1.0, 0.0)
    for k in range(4):
        acc = _dot(xb, wkey_ref[:, k * 2 * LANE:(k + 1) * 2 * LANE])
        for h in range(NSA_KV_HEADS):
            v = acc[:, h * LANE:(h + 1) * LANE]
            if k < 3:
                v = rope(v)
            if k == 0:
                cmp_ref[0, h] = v
            elif k == 1:
                keys_ref[0, h] = (v + onehot).astype(keys_ref.dtype)
            elif k == 2:
                keys_ref[1, h] = v.astype(keys_ref.dtype)
            else:
                cmp_ref[1, h] = v

    cos, sin = cos_ref[...], sin_ref[...]
    q_scale = HEAD_DIM ** -0.5 * LOG2_E
    for h in range(NSA_KV_HEADS):
        qt = _dot_nt(wq_ref[h], xb)
        for g in range(GROUP):
            blk = qt[g * HEAD_DIM:(g + 1) * HEAD_DIM]
            x1, x2 = blk[:half], blk[half:ROT_DIM]
            y = jnp.concatenate([x1 * cos - x2 * sin, x2 * cos + x1 * sin, blk[ROT_DIM:]], axis=0) * q_scale
            y = y.astype(qT_ref.dtype)
            for nn in range(tm // Q_BLOCK):
                dst = nn * GROUP * Q_BLOCK + g * Q_BLOCK
                qT_ref[h, :, dst:dst + Q_BLOCK] = y[:, nn * Q_BLOCK:(nn + 1) * Q_BLOCK]

    vt = _dot_nt(wval_ref[...], xb)
    ones_pad = jnp.where(lax.broadcasted_iota(jnp.int32, (ONES_PAD, tm), 0) == 0, 1.0, 0.0).astype(valT_ref.dtype)
    for k in range(2):
        for h in range(NSA_KV_HEADS):
            row = (k * NSA_KV_HEADS + h) * HEAD_DIM
            valT_ref[k, h, :HEAD_DIM, :] = vt[row:row + HEAD_DIM].astype(valT_ref.dtype)
            valT_ref[k, h, HEAD_DIM:, :] = ones_pad

    gt = _dot_nt(wgate_ref[...], xb)
    for h in range(NSA_KV_HEADS):
        gateT_ref[h] = gt[h * 16:(h + 1) * 16].astype(gateT_ref.dtype)


def _proj(x2, weights, tabs, b, seq, tm):
    t = x2.shape[0]
    tps = seq // tm
    wrow, wkey, wq, wval, wgate = weights
    h = NSA_KV_HEADS
    n_mix = POOL_WIDTH + 2 * SG_WIDTH
    row = lambda i: (i, 0)
    const = lambda shape: pl.BlockSpec(shape, lambda i: (0,) * len(shape), pipeline_mode=pl.Buffered(1))
    return pl.pallas_call(
        functools.partial(_proj_kernel, tiles_per_seq=tps),
        grid=(t // tm,),
        in_specs=[pl.BlockSpec((tm, D_MODEL), row), const(wrow.shape), const(wkey.shape), const(wq.shape),
                  const(wval.shape), const(wgate.shape),
                  pl.BlockSpec((tm, LANE), row), pl.BlockSpec((tm, LANE), row), pl.BlockSpec((tm, LANE), row),
                  pl.BlockSpec((ROT_DIM // 2, tm), lambda i: (0, i)), pl.BlockSpec((ROT_DIM // 2, tm), lambda i: (0, i))],
        out_specs=[pl.BlockSpec((tm, n_mix), row), pl.BlockSpec((tm, 3 * D_MODEL), row),
                   pl.BlockSpec((None, h, HEAD_DIM, tm * GROUP), lambda i: (i // tps, 0, 0, i % tps)),
                   pl.BlockSpec((2, None, h, tm, LANE), lambda i: (0, i // tps, 0, i % tps, 0)),
                   pl.BlockSpec((2, None, h, tm, LANE), lambda i: (0, i // tps, 0, i % tps, 0)),
                   pl.BlockSpec((2, None, h, HEAD_DIM + ONES_PAD, tm), lambda i: (0, i // tps, 0, 0, i % tps)),
                   pl.BlockSpec((None, h, 16, tm), lambda i: (i // tps, 0, 0, i % tps))],
        out_shape=[jax.ShapeDtypeStruct((t, n_mix), BF16), jax.ShapeDtypeStruct((t, 3 * D_MODEL), BF16),
                   jax.ShapeDtypeStruct((b, h, HEAD_DIM, seq * GROUP), BF16),
                   jax.ShapeDtypeStruct((2, b, h, seq, LANE), BF16),
                   jax.ShapeDtypeStruct((2, b, h, seq, LANE), F32),
                   jax.ShapeDtypeStruct((2, b, h, HEAD_DIM + ONES_PAD, seq), BF16),
                   jax.ShapeDtypeStruct((b, h, 16, seq), BF16)],
        compiler_params=_params(("parallel",)),
        name="proj",
    )(x2, wrow, wkey, wq, wval, wgate, *tabs)


def _prep_w_in(w):
    d = w.shape[0]
    h = NSA_KV_HEADS
    n_mix = POOL_WIDTH + 2 * SG_WIDTH
    kv0 = n_mix + Q_WIDTH
    n_gate = NSA_Q_HEADS * 3
    gate0 = kv0 + 6 * KV_WIDTH
    kv = lambda k: w[:, kv0 + k * KV_WIDTH:kv0 + (k + 1) * KV_WIDTH]
    wrow = jnp.concatenate([w[:, :n_mix], w[:, gate0 + n_gate:]], axis=1)
    pad_heads = lambda m: jnp.pad(m.reshape(d, h, HEAD_DIM), ((0, 0), (0, 0), (0, LANE - HEAD_DIM))).reshape(d, h * LANE)
    wkey = jnp.concatenate([pad_heads(kv(0)), pad_heads(kv(2)), pad_heads(kv(4)), pad_heads(kv(1))], axis=1)
    wq = w[:, n_mix:kv0].T.reshape(h, GROUP * HEAD_DIM, d)
    wval = jnp.concatenate([kv(3), kv(5)], axis=1).T
    wgate = jnp.pad(w[:, gate0:gate0 + n_gate].T.reshape(h, GROUP * 3, d), ((0, 0), (0, 16 - GROUP * 3), (0, 0)))
    return tuple(m.astype(BF16) for m in (wrow, wkey, wq, wval, wgate.reshape(h * 16, d)))


def _mixers_kernel(a_ref, ap_ref, u_ref, v_ref, pw_ref, ps_ref, lg_ref, lb_ref, sw_ref, sb_ref,
                   pm_ref, sg_ref, *, tiles_per_seq):
    i = pl.program_id(0)
    tm = a_ref.shape[0]
    it = i % tiles_per_seq
    gw = POOL_WIDTH // len(POOL_WINDOWS)

    a = a_ref[...].astype(F32)
    prev = jnp.where(it == 0, 0.0, ap_ref[...].astype(F32))
    ext = jnp.concatenate([prev, a], axis=0)
    tpos = (it * tm + lax.broadcasted_iota(jnp.int32, (tm, 1), 0) + 1).astype(F32)
    for g, w in enumerate(POOL_WINDOWS):
        s = ext[:, g * gw:(g + 1) * gw]
        k = 1
        while k < w:
            s = s + pltpu.roll(s, k, 0)
            k *= 2
        d = s[POOL_HALO:] / jnp.minimum(tpos, float(w)) - a[:, g * gw:(g + 1) * gw]
        y = _dot(d.astype(BF16), pw_ref[g]) * ps_ref[:, g * gw:(g + 1) * gw]
        pm_ref[:, g * gw:(g + 1) * gw] = y.astype(pm_ref.dtype)

    u = jax.nn.gelu(u_ref[...].astype(F32))
    v = _layer_norm_rows(jax.nn.gelu(v_ref[...].astype(F32)), lg_ref[...], lb_ref[...]).astype(BF16)
    hc = SG_WIDTH // SG_HEADS
    tri = (lax.broadcasted_iota(jnp.int32, (SG_CHUNK, SG_CHUNK), 0)
           >= lax.broadcasted_iota(jnp.int32, (SG_CHUNK, SG_CHUNK), 1))
    for g in range(SG_HEADS):
        ws = jnp.where(tri, sw_ref[g], 0.0).astype(BF16)
        bias = sb_ref[:, g:g + 1]
        for c in range(tm // SG_CHUNK):
            rows = slice(c * SG_CHUNK, (c + 1) * SG_CHUNK)
            cols = slice(g * hc, (g + 1) * hc)
            mixed = _dot(ws, v[rows, cols]) + bias
            sg_ref[rows, cols] = (u[rows, cols] * mixed).astype(sg_ref.dtype)


def _mixers(mix, pool_w, pool_scale, ln_g, ln_b, sg_w, sg_b, seq, tm):
    t = mix.shape[0]
    halo_blocks = tm // POOL_HALO
    full = lambda shape: pl.BlockSpec(shape, lambda i: (0,) * len(shape))
    return pl.pallas_call(
        functools.partial(_mixers_kernel, tiles_per_seq=seq // tm),
        grid=(t // tm,),
        in_specs=[pl.BlockSpec((tm, POOL_WIDTH), lambda i: (i, 0)),
                  pl.BlockSpec((POOL_HALO, POOL_WIDTH), lambda i: (jnp.maximum(i * halo_blocks - 1, 0), 0)),
                  pl.BlockSpec((tm, SG_WIDTH), lambda i: (i, 1)),
                  pl.BlockSpec((tm, SG_WIDTH), lambda i: (i, 2)),
                  full(pool_w.shape), full((1, POOL_WIDTH)), full((1, SG_WIDTH)), full((1, SG_WIDTH)),
                  full(sg_w.shape), full((SG_CHUNK, SG_HEADS))],
        out_specs=[pl.BlockSpec((tm, POOL_WIDTH), lambda i: (i, 0)),
                   pl.BlockSpec((tm, SG_WIDTH), lambda i: (i, 0))],
        out_shape=[jax.ShapeDtypeStruct((t, POOL_WIDTH), BF16), jax.ShapeDtypeStruct((t, SG_WIDTH), BF16)],
        compiler_params=_params(("parallel",)),
        name="mixers",
    )(mix, mix, mix, mix, pool_w.astype(BF16), pool_scale[None], ln_g[None], ln_b[None], sg_w, sg_b.T)


def _compress_kernel(src_ref, pos_ref, w1_ref, w2_ref, w2T_ref, o_ref, oT_ref):
    nc = o_ref.shape[0]
    first = jnp.zeros((nc, CMP_HIDDEN), F32)
    second = jnp.zeros((nc, CMP_HIDDEN), F32)
    for p in range(CMP_STRIDE):
        rows = src_ref[pl.ds(p, nc, stride=CMP_STRIDE), :]
        first = first + _dot((rows + pos_ref[p:p + 1, :]).astype(BF16), w1_ref[p])
        q = CMP_STRIDE + p
        second = second + _dot((rows + pos_ref[q:q + 1, :]).astype(BF16), w1_ref[q])
    pre = first + pltpu.roll(second, nc - 1, 0)
    hidden = jax.nn.gelu(pre).astype(BF16)
    o_ref[...] = _dot(hidden, w2_ref[...]).astype(o_ref.dtype)
    oT_ref[...] = _dot_nt(w2T_ref[...], hidden).astype(oT_ref.dtype)


def _compress(src, pos, w1, w2, w2T):
    _, b, h, seq, _ = src.shape
    nc = seq // CMP_STRIDE
    sel = lambda n: pl.BlockSpec((None,) + n, lambda s, bi, hi: (s,) + (0,) * len(n))
    return pl.pallas_call(
        _compress_kernel,
        grid=(2, b, h),
        in_specs=[pl.BlockSpec((None, None, None, seq, LANE), lambda s, bi, hi: (s, bi, hi, 0, 0)),
                  sel(pos.shape[1:]), sel(w1.shape[1:]), sel(w2.shape[1:]), sel(w2T.shape[1:])],
        out_specs=[pl.BlockSpec((None, None, None, nc, HEAD_DIM), lambda s, bi, hi: (s, bi, hi, 0, 0)),
                   pl.BlockSpec((None, None, None, HEAD_DIM, nc), lambda s, bi, hi: (s, bi, hi, 0, 0))],
        out_shape=[jax.ShapeDtypeStruct((2, b, h, nc, HEAD_DIM), BF16),
                   jax.ShapeDtypeStruct((2, b, h, HEAD_DIM, nc), BF16)],
        compiler_params=_params(("parallel", "parallel", "parallel")),
        name="compress",
    )(src, pos, w1, w2, w2T)


def _prep_compress(k_pos, k_w1, k_w2, v_pos, v_w1, v_w2):
    pad = LANE - HEAD_DIM
    pos = jnp.pad(jnp.stack([k_pos, v_pos]), ((0, 0), (0, 0), (0, pad)))
    w1 = jnp.stack([k_w1, v_w1]).reshape(2, CMP_LEN, HEAD_DIM, CMP_HIDDEN)
    w1 = jnp.pad(w1, ((0, 0), (0, 0), (0, pad), (0, 0))).astype(BF16)
    w2 = jnp.stack([k_w2, v_w2]).astype(BF16)
    return pos, w1, w2, w2.transpose(0, 2, 1)


def _score_strips(k_ref, k0, n_strips, q_aug, s_ref, mask):
    mx = None
    for r in range(n_strips):
        start = pl.multiple_of(k0 + r * NSA_STRIP, NSA_STRIP)
        s = _dot(k_ref[pl.ds(start, NSA_STRIP), :], q_aug).astype(s_ref.dtype)
        if mask is not None:
            s = mask(r, start, s)
        s_ref[r * NSA_STRIP:(r + 1) * NSA_STRIP, :] = s
        mx = s if mx is None else jnp.maximum(mx, s)
    return jnp.max(mx.astype(F32), axis=0, keepdims=True)


def _prob_strips(n_strips, m, s_ref, p_ref):
    m = m.astype(s_ref.dtype)
    for r in range(n_strips):
        rows = slice(r * NSA_STRIP, (r + 1) * NSA_STRIP)
        p_ref[rows, :] = jnp.exp2(s_ref[rows, :] - m).astype(BF16)


def _nsa_kernel(qT_ref, kc_ref, vcT_ref, ks_ref, vsT_ref, kw_ref, vwT_ref, g_ref, bound_ref, o_ref,
                s_ref, p_ref, pc_ref, s2_ref, p2_ref, psum_ref, selneg_ref, *, n_top):
    n = pl.program_id(2)
    qs = n * Q_BLOCK
    width = GROUP * Q_BLOCK
    qT = qT_ref[...]
    t_row = qs + lax.broadcasted_iota(jnp.int32, (1, width), 1) % Q_BLOCK
    nc = kc_ref.shape[0]
    ns = selneg_ref.shape[0]
    strip_iota = lax.broadcasted_iota(jnp.int32, (NSA_STRIP, 1), 0)

    per = SLC_LEN // CMP_STRIDE
    psum_ref[:PSUM_PAD, :] = jnp.zeros((PSUM_PAD, Q_BLOCK), F32)

    def compressed_and_selection(n_tiles):
        keys = n_tiles * CMP_TILE
        rows = keys // per
        strip_rows = [slice(r * NSA_STRIP, (r + 1) * NSA_STRIP) for r in range(keys // NSA_STRIP)]

        mx = None
        for sl in strip_rows:
            s = _dot(kc_ref[sl, :], qT)
            c_end = (sl.start + strip_iota) * CMP_STRIDE + (CMP_LEN - 1)
            s = jnp.where(c_end <= t_row, s, NEG_INF)
            s_ref[sl, :] = s
            mx = s if mx is None else jnp.maximum(mx, s)
        m_cmp = jnp.maximum(jnp.max(mx, axis=0, keepdims=True), 0.1 * NEG_INF)
        lsum = jnp.zeros((NSA_STRIP, width), F32)
        for sl in strip_rows:
            e = jnp.exp2(s_ref[sl, :] - m_cmp)
            s_ref[sl, :] = e
            pc_ref[sl, :] = e.astype(BF16)
            lsum = lsum + e
        l = jnp.sum(lsum, axis=0, keepdims=True)
        inv = jnp.where(l > 0.0, 1.0 / l, 0.0)
        out = _dot(vcT_ref[:, :keys], pc_ref[:keys, :]) * inv
        for sl in strip_rows:
            p = s_ref[sl, :] * inv
            psum = p[:, :Q_BLOCK]
            for g in range(1, GROUP):
                psum = psum + p[:, g * Q_BLOCK:(g + 1) * Q_BLOCK]
            psum_ref[PSUM_PAD + sl.start:PSUM_PAD + sl.stop, :] = psum

        part = [psum_ref[pl.ds(PSUM_PAD + k, rows, stride=per), :] for k in range(per)]
        before = psum_ref[pl.ds(PSUM_PAD - 1, rows, stride=per), :]
        imp = 2.0 * (part[0] + part[1] + part[2]) + part[3] + before

        blk = lax.broadcasted_iota(jnp.int32, (rows, Q_BLOCK), 0)
        cur = (qs + lax.broadcasted_iota(jnp.int32, (rows, Q_BLOCK), 1)) // SLC_LEN
        taken = -2.0
        score = jnp.where(blk <= cur, imp, -1.0)
        score = jnp.where(blk == 0, taken, jnp.where(blk == cur, taken, jnp.where(blk == cur - 1, taken, score)))
        blk_f = blk.astype(F32)
        for _ in range(n_top - 3):
            top = jnp.max(score, axis=0, keepdims=True)
            cand = jnp.where(score == top, blk_f, float(rows))
            first = jnp.min(cand, axis=0, keepdims=True)
            score = jnp.where(cand == first, taken, score)
        selneg_ref[:rows, :] = jnp.where(score == taken, 0.0, NEG_INF)
        if rows < ns:
            selneg_ref[rows:, :] = jnp.full((ns - rows, Q_BLOCK), NEG_INF, F32)
        return out

    reach = CMP_TILE * CMP_STRIDE // Q_BLOCK
    o_cmp = lax.switch(n // reach, [functools.partial(compressed_and_selection, t + 1) for t in range(nc // CMP_TILE)])

    strips = SLC_BLOCK // NSA_STRIP
    sel_rows = ks_ref.shape[1] - HEAD_DIM - SEL_GROUP
    last_block = ks_ref.shape[0] // SLC_BLOCK - 1
    n_diag = n // (SLC_BLOCK // Q_BLOCK)

    def past_only(r, start, s):
        return jnp.where(start + strip_iota <= t_row, s, NEG_INF)

    def slc_scores(kb, slot):
        per_block = SLC_BLOCK // Q_BLOCK
        variant = jnp.where(kb < n_diag, 0, jnp.where(kb > n_diag, 1 + per_block, 1 + n % per_block))

        def limit(r, start, s):
            return jnp.minimum(s, bound_ref[variant, r * NSA_STRIP:(r + 1) * NSA_STRIP, :])

        kb = jnp.minimum(kb, last_block)
        k0 = pl.multiple_of(kb * SLC_BLOCK, SLC_BLOCK)
        g0 = pl.multiple_of((kb * strips // SEL_GROUP) * SEL_GROUP, SEL_GROUP)
        bias = jnp.concatenate([selneg_ref[pl.ds(g0, SEL_GROUP), :]] * GROUP, axis=1)
        bias = jnp.concatenate([bias, jnp.zeros((sel_rows, width), F32)], axis=0).astype(BF16)
        q_aug = jnp.concatenate([qT, bias], axis=0)
        return _score_strips(ks_ref, k0, strips, q_aug, s2_ref.at[slot], limit)

    def slc_consume(kb, slot, m, acc, mx):
        k0 = pl.multiple_of(kb * SLC_BLOCK, SLC_BLOCK)
        m_new = jnp.maximum(m, mx)
        _prob_strips(strips, m_new, s2_ref.at[slot], p2_ref.at[slot])
        pv = _dot(vsT_ref[:, pl.ds(k0, SLC_BLOCK)], p2_ref[slot])
        return m_new, jnp.exp2(m - m_new) * acc + pv

    def slc_group(j, carry):
        m, acc, pending = carry[0], carry[1], list(carry[2:])
        for u in range(SLC_UNROLL):
            kb = SLC_UNROLL * j + u
            pending.append(slc_scores(kb + SLC_AHEAD, (u + SLC_AHEAD) % SLC_UNROLL))
            m, acc = slc_consume(kb, u, m, acc, pending.pop(0))
        return (m, acc, *pending)

    init = (jnp.full((1, width), NEG_INF, F32), jnp.zeros((vsT_ref.shape[0], width), F32),
            *[slc_scores(a, a) for a in range(SLC_AHEAD)])
    acc = lax.fori_loop(0, (n_diag + SLC_UNROLL) // SLC_UNROLL, slc_group, init)[1]
    o_slc = acc[:HEAD_DIM] / acc[HEAD_DIM:HEAD_DIM + 1]

    span = WINDOW + Q_BLOCK
    w_strips = span // NSA_STRIP
    edge = Q_BLOCK // NSA_STRIP
    w0 = pl.multiple_of(jnp.maximum(qs - WINDOW, 0), Q_BLOCK)

    def band_edges(r, start, s):
        if r < edge:
            return jnp.where(t_row - (start + strip_iota) < WINDOW, s, NEG_INF)
        if r >= w_strips - edge:
            return past_only(r, start, s)
        return s

    q_win = jnp.concatenate([qT, jnp.zeros((kw_ref.shape[1] - HEAD_DIM, width), BF16)], axis=0)
    m_win = lax.cond(n >= WINDOW // Q_BLOCK,
                     lambda: _score_strips(kw_ref, w0, w_strips, q_win, s_ref, band_edges),
                     lambda: _score_strips(kw_ref, w0, w_strips, q_win, s_ref, past_only))
    _prob_strips(w_strips, m_win, s_ref, p_ref)
    pv = _dot(vwT_ref[:, pl.ds(w0, span)], p_ref[:span, :])
    o_win = pv[:HEAD_DIM] / pv[HEAD_DIM:HEAD_DIM + 1]

    gates = jax.nn.sigmoid(g_ref[...].astype(F32))

    def gate(c):
        return jnp.concatenate([gates[g * 3 + c:g * 3 + c + 1, :] for g in range(GROUP)], axis=1)

    o = gate(0) * o_cmp + gate(1) * o_slc + gate(2) * o_win
    for g in range(GROUP):
        o_ref[:, g * HEAD_DIM:(g + 1) * HEAD_DIM] = o[:, g * Q_BLOCK:(g + 1) * Q_BLOCK].T.astype(o_ref.dtype)


def _causal_bounds():
    big = np.float32(3.0e38)
    key = np.arange(SLC_BLOCK)[:, None]
    query = np.tile(np.arange(Q_BLOCK), GROUP)[None, :]
    tables = [np.full((SLC_BLOCK, GROUP * Q_BLOCK), big, np.float32)]
    for v in range(SLC_BLOCK // Q_BLOCK):
        tables.append(np.where(key <= query + v * Q_BLOCK, big, np.float32(NEG_INF)).astype(np.float32))
    tables.append(np.full((SLC_BLOCK, GROUP * Q_BLOCK), NEG_INF, np.float32))
    return jnp.asarray(np.stack(tables))


def _nsa(qT, cmp, cmpT, keys, valT, gates, seq):
    b, h = qT.shape[:2]
    nq = seq // Q_BLOCK
    nc = cmp.shape[3]
    ns = seq // SLC_LEN
    width = GROUP * Q_BLOCK
    assert nc == ns * (SLC_LEN // CMP_STRIDE) and nc % CMP_TILE == 0 and seq % (SLC_BLOCK * SLC_UNROLL) == 0
    assert seq >= WINDOW + Q_BLOCK and ns >= SLC_TOP
    bounds = _causal_bounds()
    per_head = lambda arr, k: pl.BlockSpec((None, None, None) + arr.shape[3:], lambda bi, hi, n, k=k: (k, bi, hi, 0, 0))
    return pl.pallas_call(
        functools.partial(_nsa_kernel, n_top=SLC_TOP),
        grid=(b, h, nq),
        in_specs=[pl.BlockSpec((None, None, HEAD_DIM, width), lambda bi, hi, n: (bi, hi, 0, n)),
                  per_head(cmp, 0), per_head(cmpT, 1),
                  per_head(keys, 0), per_head(valT, 0), per_head(keys, 1), per_head(valT, 1),
                  pl.BlockSpec((None, None, 16, Q_BLOCK), lambda bi, hi, n: (bi, hi, 0, n)),
                  pl.BlockSpec(bounds.shape, lambda bi, hi, n: (0, 0, 0), pipeline_mode=pl.Buffered(1))],
        out_specs=pl.BlockSpec((Q_BLOCK, GROUP * HEAD_DIM), lambda bi, hi, n: (bi * nq + n, hi)),
        out_shape=jax.ShapeDtypeStruct((b * seq, Q_WIDTH), BF16),
        scratch_shapes=[pltpu.VMEM((max(nc, WINDOW + Q_BLOCK), width), F32),
                        pltpu.VMEM((WINDOW + Q_BLOCK, width), BF16),
                        pltpu.VMEM((nc, width), BF16),
                        pltpu.VMEM((SLC_UNROLL, SLC_BLOCK, width), F32),
                        pltpu.VMEM((SLC_UNROLL, SLC_BLOCK, width), BF16),
                        pltpu.VMEM((PSUM_PAD + nc, Q_BLOCK), F32),
                        pltpu.VMEM((ns, Q_BLOCK), F32)],
        compiler_params=_params(("parallel", "parallel", "arbitrary")),
        name="nsa",
    )(qT, cmp, cmpT, keys, valT, keys, valT, gates, bounds)


def _merge_kernel(*refs, with_router, alpha):
    (pm_ref, sg_ref, on_ref, gp_ref, gs_ref, gn_ref, x_ref, wp_ref, ws_ref, wn_ref, wo_ref, lg_ref, lb_ref) = refs[:13]
    if with_router:
        wr_ref, br_ref, x1_ref, route_ref, x1t_ref = refs[13:]
    else:
        (x1_ref,) = refs[13:]
    sig = lambda r: jax.nn.sigmoid(r[...].astype(F32))
    y = (sig(gp_ref) * _dot(pm_ref[...], wp_ref[...]) + sig(gs_ref) * _dot(sg_ref[...], ws_ref[...])
         + sig(gn_ref) * _dot(on_ref[...], wn_ref[...]))
    hmix = _dot(y.astype(BF16), wo_ref[...])
    x1 = _layer_norm_rows(alpha * x_ref[...] + hmix, lg_ref[...], lb_ref[...])
    x1_ref[...] = x1
    if with_router:
        _store_token_tiles(x1t_ref, x1)
        wr = wr_ref[...]
        w_hi = wr.astype(BF16)
        w_lo = (wr - w_hi.astype(F32)).astype(BF16)
        x_hi = x1.astype(BF16)
        x_lo = (x1 - x_hi.astype(F32)).astype(BF16)
        logits = _dot(x_hi, w_hi) + (_dot(x_lo, w_hi) + _dot(x_hi, w_lo)) + br_ref[...]
        lane = lax.broadcasted_iota(jnp.int32, logits.shape, 1)
        lane_f = lane.astype(F32)
        logits = jnp.where(lane < N_EXPERTS, logits, -jnp.inf)
        m1 = jnp.max(logits, axis=-1, keepdims=True)
        i1 = jnp.min(jnp.where(logits == m1, lane_f, float(LANE)), axis=-1, keepdims=True)
        rest = jnp.where(lane_f == i1, -jnp.inf, logits)
        m2 = jnp.max(rest, axis=-1, keepdims=True)
        i2 = jnp.min(jnp.where(rest == m2, lane_f, float(LANE)), axis=-1, keepdims=True)
        e2 = jnp.exp(m2 - m1)
        w1 = 1.0 / (1.0 + e2)
        route_ref[...] = jnp.where(lane == 0, i1, jnp.where(lane == 1, i2, jnp.where(lane == 2, w1,
                                   jnp.where(lane == 3, e2 * w1, 0.0))))


def _merge(pm, sgm, on, br, x2, wp, ws, wn, wo, lg, lb, alpha, tm, router=None):
    t = x2.shape[0]
    row = lambda width, cb=0: pl.BlockSpec((tm, width), lambda i, cb=cb: (i, cb))
    full = lambda shape: pl.BlockSpec(shape, lambda i: (0,) * len(shape))
    in_specs = [row(POOL_WIDTH), row(SG_WIDTH), row(Q_WIDTH),
                row(D_MODEL, 0), row(D_MODEL, 1), row(D_MODEL, 2),
                row(D_MODEL), full(wp.shape), full(ws.shape), full(wn.shape), full(wo.shape),
                full((1, D_MODEL)), full((1, D_MODEL))]
    args = [pm, sgm, on, br, br, br, x2, wp.astype(BF16), ws.astype(BF16), wn.astype(BF16), wo.astype(BF16),
            lg[None], lb[None]]
    out_specs = [row(D_MODEL)]
    out_shape = [jax.ShapeDtypeStruct((t, D_MODEL), F32)]
    if router is not None:
        w_router, b_router = router
        pad = LANE - N_EXPERTS
        in_specs += [full((D_MODEL, LANE)), full((1, LANE))]
        args += [jnp.pad(w_router, ((0, 0), (0, pad))), jnp.pad(b_router, (0, pad))[None]]
        out_specs += [row(LANE), pl.BlockSpec((tm * TOKEN_TILE, LANE), lambda i: (i, 0))]
        out_shape += [jax.ShapeDtypeStruct((t, LANE), F32), jax.ShapeDtypeStruct((t * TOKEN_TILE, LANE), F32)]
    out = pl.pallas_call(
        functools.partial(_merge_kernel, with_router=router is not None, alpha=alpha),
        grid=(t // tm,), in_specs=in_specs, out_specs=out_specs, out_shape=out_shape,
        compiler_params=_params(("parallel",)),
        name="merge",
    )(*args)
    return out if router is not None else out[0]


def _tile_rows(i):
    start = i * TOKEN_TILE if isinstance(i, int) else pl.multiple_of(i * TOKEN_TILE, TOKEN_TILE)
    return pl.ds(start, TOKEN_TILE)


def _for_rows(n, fn):
    def body(r, c):
        fn(r)
        return c
    lax.fori_loop(0, n, body, 0, unroll=8)


def _moe_kernel(be_ref, tok_ref, tok_next_ref, slot_ref, x_hbm, wg_ref, wu_ref, wd_ref, y_hbm,
                xg_ref, xb_ref, acc_ref, yt_ref, gather_sem, scatter_sem):
    i = pl.program_id(0)
    k = pl.program_id(1)
    n_blocks = pl.num_programs(0)
    rows = xb_ref.shape[0]

    def gather(idx_ref, slot):
        return lambda r: pltpu.make_async_copy(x_hbm.at[_tile_rows(idx_ref[0, r])], xg_ref.at[slot, _tile_rows(r)],
                                               gather_sem.at[slot])

    def scatter(idx_ref):
        return lambda r: pltpu.make_async_copy(yt_ref.at[_tile_rows(r)], y_hbm.at[_tile_rows(idx_ref[0, r])],
                                               scatter_sem)

    def wait_gather(slot):
        pltpu.make_async_copy(x_hbm.at[pl.ds(0, rows * TOKEN_TILE)], xg_ref.at[slot], gather_sem.at[slot]).wait()

    @pl.when(k == 0)
    def _():
        slot = i % 2

        @pl.when(i == 0)
        def _():
            _for_rows(rows, lambda r: gather(tok_ref, 0)(r).start())

        wait_gather(slot)
        for s in range(TOKEN_TILE):
            xb_ref[:, s * LANE:(s + 1) * LANE] = xg_ref[slot, pl.ds(s, rows, stride=TOKEN_TILE), :].astype(BF16)

    share = rows // (D_FF // FF_TILE)
    for r in range(share):
        gather(tok_next_ref, 1 - i % 2)(k * share + r).start()

    xb = xb_ref[...]
    hidden = (jax.nn.silu(_dot(xb, wg_ref[...])) * _dot(xb, wu_ref[...])).astype(BF16)
    part = _dot(hidden, wd_ref[...])

    @pl.when(k == 0)
    def _():
        acc_ref[...] = part

    @pl.when(k == pl.num_programs(1) - 1)
    def _():
        def wait_scatter():
            pltpu.make_async_copy(yt_ref, y_hbm.at[pl.ds(0, rows * TOKEN_TILE)], scatter_sem).wait()

        pl.when(i > 0)(wait_scatter)
        _store_token_tiles(yt_ref, acc_ref[...] + part)
        _for_rows(rows, lambda r: scatter(slot_ref)(r).start())

        @pl.when(i == n_blocks - 1)
        def _():
            wait_scatter()
            wait_gather(1 - i % 2)


def _moe(x_tiles, row_tok, row_slot, n_slots, block_e, wg, wu, wd):
    rows = EXPERT_BLOCK
    n_blocks = row_tok.shape[0] // rows
    kf = D_FF // FF_TILE
    assert kf == 2
    idx_spec = lambda fn: pl.BlockSpec((None, 1, rows), lambda i, k, be: (fn(i), 0, 0), memory_space=pltpu.SMEM)
    tok3 = row_tok.reshape(n_blocks, 1, rows)
    slot3 = row_slot.reshape(n_blocks, 1, rows)
    return pl.pallas_call(
        _moe_kernel,
        grid_spec=pltpu.PrefetchScalarGridSpec(
            num_scalar_prefetch=1,
            grid=(n_blocks, kf),
            in_specs=[idx_spec(lambda i: i), idx_spec(lambda i: jnp.minimum(i + 1, n_blocks - 1)),
                      idx_spec(lambda i: i),
                      pl.BlockSpec(memory_space=pl.ANY),
                      pl.BlockSpec((None, D_MODEL, FF_TILE), lambda i, k, be: (be[i], 0, k)),
                      pl.BlockSpec((None, D_MODEL, FF_TILE), lambda i, k, be: (be[i], 0, k)),
                      pl.BlockSpec((None, FF_TILE, D_MODEL), lambda i, k, be: (be[i], k, 0))],
            out_specs=pl.BlockSpec(memory_space=pl.ANY),
            scratch_shapes=[pltpu.VMEM((2, rows * TOKEN_TILE, LANE), F32), pltpu.VMEM((rows, D_MODEL), BF16),
                            pltpu.VMEM((rows, D_MODEL), F32), pltpu.VMEM((rows * TOKEN_TILE, LANE), F32),
                            pltpu.SemaphoreType.DMA((2,)), pltpu.SemaphoreType.DMA(())]),
        out_shape=jax.ShapeDtypeStruct((n_slots * TOKEN_TILE, LANE), F32),
        compiler_params=_params(("arbitrary", "arbitrary")),
        name="moe",
    )(block_e, tok3, tok3, slot3, x_tiles, wg, wu, wd)


def _swiglu_kernel(x_ref, wg_ref, wu_ref, wd_ref, o_ref, xb_ref):
    k = pl.program_id(1)

    @pl.when(k == 0)
    def _():
        xb_ref[...] = x_ref[...].astype(BF16)

    xb = xb_ref[...]
    hidden = (jax.nn.silu(_dot(xb, wg_ref[...])) * _dot(xb, wu_ref[...])).astype(BF16)
    part = _dot(hidden, wd_ref[...])

    @pl.when(k == 0)
    def _():
        o_ref[...] = part

    @pl.when(k != 0)
    def _():
        o_ref[...] += part


def _swiglu(rows, wg, wu, wd, tm):
    r = rows.shape[0]
    row = pl.BlockSpec((tm, D_MODEL), lambda i, k: (i, 0))
    return pl.pallas_call(
        _swiglu_kernel,
        grid=(r // tm, D_FF // FF_TILE),
        in_specs=[row,
                  pl.BlockSpec((D_MODEL, FF_TILE), lambda i, k: (0, k)),
                  pl.BlockSpec((D_MODEL, FF_TILE), lambda i, k: (0, k)),
                  pl.BlockSpec((FF_TILE, D_MODEL), lambda i, k: (k, 0))],
        out_specs=row,
        out_shape=jax.ShapeDtypeStruct(rows.shape, F32),
        scratch_shapes=[pltpu.VMEM((tm, D_MODEL), BF16)],
        compiler_params=_params(("parallel", "arbitrary")),
        name="swiglu",
    )(rows, wg, wu, wd)


def _ple_ln2_kernel(*refs, n_parts, alpha):
    x_ref, p_ref = refs[0], refs[1]
    parts = refs[2:2 + n_parts]
    rest = refs[2 + n_parts:]
    if n_parts > 1:
        route_ref, rest = rest[0], rest[1:]
    wg_ref, bg_ref, wp_ref, lg_ref, lb_ref, o_ref = rest
    x1 = x_ref[...]
    if n_parts > 1:
        route = route_ref[...]
        f = _load_token_tiles(parts[0]) * route[:, TOP_K:TOP_K + 1]
        for k in range(1, n_parts):
            f = f + _load_token_tiles(parts[k]) * route[:, TOP_K + k:TOP_K + k + 1]
    else:
        f = parts[0][...]
    gate = jax.nn.sigmoid(_dot(x1.astype(BF16), wg_ref[...]) + bg_ref[...])
    ple = gate * _dot(p_ref[...].astype(BF16), wp_ref[...])
    o_ref[...] = _layer_norm_rows(alpha * x1 + f + ple, lg_ref[...], lb_ref[...])


def _ple_ln2(x1, p2, y, route, wg, bg, wp, lg, lb, alpha, tm):
    t = x1.shape[0]
    n_parts = 1 if route is None else TOP_K
    steps = t // tm
    row = lambda width: pl.BlockSpec((tm, width), lambda i: (i, 0))
    full = lambda shape: pl.BlockSpec(shape, lambda i: (0,) * len(shape))
    in_specs = [row(D_MODEL), row(PLE_DIM)]
    if n_parts > 1:
        in_specs += [pl.BlockSpec((tm * TOKEN_TILE, LANE), lambda i, k=k: (k * steps + i, 0)) for k in range(n_parts)]
    else:
        in_specs.append(row(D_MODEL))
    args = [x1, p2] + [y] * n_parts
    if n_parts > 1:
        in_specs.append(row(LANE))
        args.append(route)
    in_specs += [full((D_MODEL, D_MODEL)), full((1, D_MODEL)), full((PLE_DIM, D_MODEL)),
                 full((1, D_MODEL)), full((1, D_MODEL))]
    args += [wg.astype(BF16), bg[None], wp.astype(BF16), lg[None], lb[None]]
    return pl.pallas_call(
        functools.partial(_ple_ln2_kernel, n_parts=n_parts, alpha=alpha),
        grid=(steps,), in_specs=in_specs, out_specs=row(D_MODEL),
        out_shape=jax.ShapeDtypeStruct((t, D_MODEL), F32),
        compiler_params=_params(("parallel",)),
        name="ple_ln2",
    )(*args)


def _route_tables(route, t):
    flat_e = route[:, :TOP_K].astype(jnp.int32).reshape(-1)
    tk = flat_e.shape[0]
    onehot = (flat_e[:, None] == jnp.arange(N_EXPERTS, dtype=jnp.int32)[None, :]).astype(jnp.int32)
    before = jnp.cumsum(onehot, axis=0) - onehot
    rank = jnp.sum(before * onehot, axis=1)
    counts = jnp.sum(onehot, axis=0)
    padded = (counts + EXPERT_BLOCK - 1) // EXPERT_BLOCK * EXPERT_BLOCK
    pend = jnp.cumsum(padded)
    dest = (pend - padded)[flat_e] + rank
    n_blocks = -(-tk // EXPERT_BLOCK) + N_EXPERTS
    n_rows = n_blocks * EXPERT_BLOCK
    row_pair = jnp.full((n_rows,), -1, jnp.int32).at[dest].set(jnp.arange(tk, dtype=jnp.int32))
    is_pad = (row_pair < 0).astype(jnp.int32)
    row_tok = jnp.where(row_pair < 0, 0, row_pair // TOP_K)
    pad_slot = tk + jnp.cumsum(is_pad) - is_pad
    row_slot = jnp.where(row_pair < 0, pad_slot, (row_pair % TOP_K) * t + row_pair // TOP_K).astype(jnp.int32)
    block_start = jnp.arange(n_blocks, dtype=jnp.int32) * EXPERT_BLOCK
    block_e = jnp.sum((pend[None, :] <= block_start[:, None]).astype(jnp.int32), axis=1)
    block_e = jnp.minimum(block_e, N_EXPERTS - 1).astype(jnp.int32)
    return row_tok, row_slot, n_rows, block_e


def kernel(x, p, positions, w_in, pool_w, pool_scale, sg_ln_g, sg_ln_b, sg_w, sg_b, cmp_k_pos, cmp_k_w1, cmp_k_w2, cmp_v_pos, cmp_v_w1, cmp_v_w2, w_pool_out, w_sg_out, w_nsa_out, w_out, ln1_g, ln1_b, ffn_w_gate, ffn_w_up, ffn_w_down, moe_router, moe_router_b, moe_w_gate, moe_w_up, moe_w_down, ple_gate_w, ple_gate_b, ple_proj, ln2_g, ln2_b):
    b, seq, d = x.shape
    depth = w_in.shape[0]
    t = b * seq
    alpha = (2 * depth) ** 0.25
    tm = 512
    tabs = _rope_tables(positions)
    x2 = x.reshape(t, d)
    for i in range(depth):
        mix, br, qT, keys, cmp_src, valT, gates = _proj(x2, _prep_w_in(w_in[i]), tabs, b, seq, tm)
        pm, sgm = _mixers(mix, pool_w[i], pool_scale[i], sg_ln_g[i], sg_ln_b[i], sg_w[i], sg_b[i], seq, tm)
        cmp, cmpT = _compress(cmp_src, *_prep_compress(cmp_k_pos[i], cmp_k_w1[i], cmp_k_w2[i],
                                                       cmp_v_pos[i], cmp_v_w1[i], cmp_v_w2[i]))
        on = _nsa(qT, cmp, cmpT, keys, valT, gates, seq)
        j = i // 2
        moe = i % 2 == 1
        merged = _merge(pm, sgm, on, br, x2, w_pool_out[i], w_sg_out[i], w_nsa_out[i], w_out[i],
                        ln1_g[i], ln1_b[i], alpha, tm, router=(moe_router[j], moe_router_b[j]) if moe else None)
        if moe:
            x1, route, x1_tiles = merged
            row_tok, row_slot, n_slots, block_e = _route_tables(route, t)
            y = _moe(x1_tiles, row_tok, row_slot, n_slots, block_e, moe_w_gate[j].astype(BF16),
                     moe_w_up[j].astype(BF16), moe_w_down[j].astype(BF16))
        else:
            x1, route = merged, None
            y = _swiglu(x1, ffn_w_gate[j].astype(BF16), ffn_w_up[j].astype(BF16), ffn_w_down[j].astype(BF16), tm)
        x2 = _ple_ln2(x1, p[i].reshape(t, PLE_DIM), y, route, ple_gate_w[i], ple_gate_b[i], ple_proj[i],
                      ln2_g[i], ln2_b[i], alpha, tm)
    return x2.reshape(b, seq, d)
```

```python
import functools

import jax
import jax.numpy as jnp
import numpy as np
from jax import lax
from jax.experimental import pallas as pl
from jax.experimental.pallas import tpu as pltpu

F32 = jnp.float32
BF16 = jnp.bfloat16

D_MODEL = 1024
POOL_WIDTH = 512
POOL_WINDOWS = (2, 4, 8, 16)
POOL_HALO = 16
SG_WIDTH = 512
SG_HEADS = 4
SG_CHUNK = 128
NSA_Q_HEADS = 8
NSA_KV_HEADS = 2
GROUP = NSA_Q_HEADS // NSA_KV_HEADS
HEAD_DIM = 64
Q_WIDTH = NSA_Q_HEADS * HEAD_DIM
KV_WIDTH = NSA_KV_HEADS * HEAD_DIM
ROT_DIM = HEAD_DIM // 4
ROPE_THETA = 500000.0
CMP_LEN = 32
CMP_STRIDE = 16
CMP_HIDDEN = 256
SLC_LEN = 64
SLC_TOP = 16
WINDOW = 512
Q_BLOCK = 128
NEG_INF = -1e30
D_FF = 2816
N_EXPERTS = 8
TOP_K = 2
EXPERT_BLOCK = 512
PLE_DIM = 256
LN_EPS = 1e-5

LANE = 128
TOKEN_TILE = D_MODEL // LANE
PROJ_BLOCK = 512
NSA_STRIP = 64
SLC_BLOCK = 256
SLC_UNROLL = 4
SLC_AHEAD = 2
CMP_TILE = 128
SEL_GROUP = 8
ONES_PAD = 16
PSUM_PAD = 8
LOG2_E = 1.4426950408889634
FF_TILE = 1408
VMEM_LIMIT = 56 * 1024 * 1024


def _params(semantics, vmem=VMEM_LIMIT):
    return pltpu.CompilerParams(dimension_semantics=semantics, vmem_limit_bytes=vmem)


def _dot(a, b):
    return jnp.dot(a, b, preferred_element_type=F32)


def _dot_nt(a, b):
    return lax.dot_general(a, b, (((1,), (1,)), ((), ())), preferred_element_type=F32)


def _store_token_tiles(ref, v):
    n = v.shape[0]
    for s in range(TOKEN_TILE):
        ref[pl.ds(s, n, stride=TOKEN_TILE), :] = v[:, s * LANE:(s + 1) * LANE]


def _load_token_tiles(ref):
    n = ref.shape[0] // TOKEN_TILE
    return jnp.concatenate([ref[pl.ds(s, n, stride=TOKEN_TILE), :] for s in range(TOKEN_TILE)], axis=1)


def _layer_norm_rows(v, g, b):
    mu = jnp.mean(v, axis=-1, keepdims=True)
    c = v - mu
    var = jnp.mean(c * c, axis=-1, keepdims=True)
    return c * lax.rsqrt(var + LN_EPS) * g + b


def _rope_tables(positions):
    half = ROT_DIM // 2
    inv = ROPE_THETA ** (-jnp.arange(half, dtype=F32) / half)
    ang = positions.astype(F32)[:, :, None] * inv
    cos, sin = jnp.cos(ang), jnp.sin(ang)
    rest = HEAD_DIM - ROT_DIM
    one = jnp.ones(ang.shape[:2] + (rest,), F32)
    zero_r = jnp.zeros(ang.shape[:2] + (rest,), F32)
    zero_h = jnp.zeros_like(sin)
    c = jnp.concatenate([cos, cos, one], -1)
    s1 = jnp.concatenate([zero_h, sin, zero_r], -1)
    s2 = jnp.concatenate([-sin, zero_h, zero_r], -1)
    reps = LANE // HEAD_DIM
    lane_tabs = tuple(jnp.tile(t, (1, 1, reps)).reshape(-1, LANE) for t in (c, s1, s2))
    return lane_tabs + (cos.reshape(-1, half).T, sin.reshape(-1, half).T)


def _proj_kernel(x_ref, wrow_ref, wkey_ref, wq_ref, wval_ref, wgate_ref, c_ref, s1_ref, s2_ref, cos_ref, sin_ref,
                 mix_ref, br_ref, qT_ref, keys_ref, cmp_ref, valT_ref, gateT_ref, *, tiles_per_seq):
    tm = x_ref.shape[0]
    half = ROT_DIM // 2
    xb = x_ref[...].astype(BF16)

    for ref, col0 in ((mix_ref, 0), (br_ref, mix_ref.shape[1])):
        for col in range(0, ref.shape[1], PROJ_BLOCK):
            ref[:, col:col + PROJ_BLOCK] = _dot(xb, wrow_ref[:, col0 + col:col0 + col + PROJ_BLOCK]).astype(ref.dtype)

    def rope(v):
        return (v * c_ref[...] + pltpu.roll(v, half, 1) * s1_ref[...]
                + pltpu.roll(v, LANE - half, 1) * s2_ref[...])

    tpos = (pl.program_id(0) % tiles_per_seq) * tm + lax.broadcasted_iota(jnp.int32, (tm, LANE), 0)
    lane = lax.broadcasted_iota(jnp.int32, (tm, LANE), 1)
    onehot = jnp.where(lane == HEAD_DIM + (tpos // SLC_LEN) % SEL_GROUP, 1.0, 0.0)
    for k in range(4):
        acc = _dot(xb, wkey_ref[:, k * 2 * LANE:(k + 1) * 2 * LANE])
        for h in range(NSA_KV_HEADS):
            v = acc[:, h * LANE:(h + 1) * LANE]
            if k < 3:
                v = rope(v)
            if k == 0:
                cmp_ref[0, h] = v
            elif k == 1:
                keys_ref[0, h] = (v + onehot).astype(keys_ref.dtype)
            elif k == 2:
                keys_ref[1, h] = v.astype(keys_ref.dtype)
            else:
                cmp_ref[1, h] = v

    cos, sin = cos_ref[...], sin_ref[...]
    q_scale = HEAD_DIM ** -0.5 * LOG2_E
    for h in range(NSA_KV_HEADS):
        qt = _dot_nt(wq_ref[h], xb)
        for g in range(GROUP):
            blk = qt[g * HEAD_DIM:(g + 1) * HEAD_DIM]
            x1, x2 = blk[:half], blk[half:ROT_DIM]
            y = jnp.concatenate([x1 * cos - x2 * sin, x2 * cos + x1 * sin, blk[ROT_DIM:]], axis=0) * q_scale
            y = y.astype(qT_ref.dtype)
            for nn in range(tm // Q_BLOCK):
                dst = nn * GROUP * Q_BLOCK + g * Q_BLOCK
                qT_ref[h, :, dst:dst + Q_BLOCK] = y[:, nn * Q_BLOCK:(nn + 1) * Q_BLOCK]

    vt = _dot_nt(wval_ref[...], xb)
    ones_pad = jnp.where(lax.broadcasted_iota(jnp.int32, (ONES_PAD, tm), 0) == 0, 1.0, 0.0).astype(valT_ref.dtype)
    for k in range(2):
        for h in range(NSA_KV_HEADS):
            row = (k * NSA_KV_HEADS + h) * HEAD_DIM
            valT_ref[k, h, :HEAD_DIM, :] = vt[row:row + HEAD_DIM].astype(valT_ref.dtype)
            valT_ref[k, h, HEAD_DIM:, :] = ones_pad

    gt = _dot_nt(wgate_ref[...], xb)
    for h in range(NSA_KV_HEADS):
        gateT_ref[h] = gt[h * 16:(h + 1) * 16].astype(gateT_ref.dtype)


def _proj(x2, weights, tabs, b, seq, tm):
    t = x2.shape[0]
    tps = seq // tm
    wrow, wkey, wq, wval, wgate = weights
    h = NSA_KV_HEADS
    n_mix = POOL_WIDTH + 2 * SG_WIDTH
    row = lambda i: (i, 0)
    const = lambda shape: pl.BlockSpec(shape, lambda i: (0,) * len(shape), pipeline_mode=pl.Buffered(1))
    return pl.pallas_call(
        functools.partial(_proj_kernel, tiles_per_seq=tps),
        grid=(t // tm,),
        in_specs=[pl.BlockSpec((tm, D_MODEL), row), const(wrow.shape), const(wkey.shape), const(wq.shape),
                  const(wval.shape), const(wgate.shape),
                  pl.BlockSpec((tm, LANE), row), pl.BlockSpec((tm, LANE), row), pl.BlockSpec((tm, LANE), row),
                  pl.BlockSpec((ROT_DIM // 2, tm), lambda i: (0, i)), pl.BlockSpec((ROT_DIM // 2, tm), lambda i: (0, i))],
        out_specs=[pl.BlockSpec((tm, n_mix), row), pl.BlockSpec((tm, 3 * D_MODEL), row),
                   pl.BlockSpec((None, h, HEAD_DIM, tm * GROUP), lambda i: (i // tps, 0, 0, i % tps)),
                   pl.BlockSpec((2, None, h, tm, LANE), lambda i: (0, i // tps, 0, i % tps, 0)),
                   pl.BlockSpec((2, None, h, tm, LANE), lambda i: (0, i // tps, 0, i % tps, 0)),
                   pl.BlockSpec((2, None, h, HEAD_DIM + ONES_PAD, tm), lambda i: (0, i // tps, 0, 0, i % tps)),
                   pl.BlockSpec((None, h, 16, tm), lambda i: (i // tps, 0, 0, i % tps))],
        out_shape=[jax.ShapeDtypeStruct((t, n_mix), BF16), jax.ShapeDtypeStruct((t, 3 * D_MODEL), BF16),
                   jax.ShapeDtypeStruct((b, h, HEAD_DIM, seq * GROUP), BF16),
                   jax.ShapeDtypeStruct((2, b, h, seq, LANE), BF16),
                   jax.ShapeDtypeStruct((2, b, h, seq, LANE), F32),
                   jax.ShapeDtypeStruct((2, b, h, HEAD_DIM + ONES_PAD, seq), BF16),
                   jax.ShapeDtypeStruct((b, h, 16, seq), BF16)],
        compiler_params=_params(("parallel",)),
        name="proj",
    )(x2, wrow, wkey, wq, wval, wgate, *tabs)


def _prep_w_in(w):
    d = w.shape[0]
    h = NSA_KV_HEADS
    n_mix = POOL_WIDTH + 2 * SG_WIDTH
    kv0 = n_mix + Q_WIDTH
    n_gate = NSA_Q_HEADS * 3
    gate0 = kv0 + 6 * KV_WIDTH
    kv = lambda k: w[:, kv0 + k * KV_WIDTH:kv0 + (k + 1) * KV_WIDTH]
    wrow = jnp.concatenate([w[:, :n_mix], w[:, gate0 + n_gate:]], axis=1)
    pad_heads = lambda m: jnp.pad(m.reshape(d, h, HEAD_DIM), ((0, 0), (0, 0), (0, LANE - HEAD_DIM))).reshape(d, h * LANE)
    wkey = jnp.concatenate([pad_heads(kv(0)), pad_heads(kv(2)), pad_heads(kv(4)), pad_heads(kv(1))], axis=1)
    wq = w[:, n_mix:kv0].T.reshape(h, GROUP * HEAD_DIM, d)
    wval = jnp.concatenate([kv(3), kv(5)], axis=1).T
    wgate = jnp.pad(w[:, gate0:gate0 + n_gate].T.reshape(h, GROUP * 3, d), ((0, 0), (0, 16 - GROUP * 3), (0, 0)))
    return tuple(m.astype(BF16) for m in (wrow, wkey, wq, wval, wgate.reshape(h * 16, d)))


def _mixers_kernel(a_ref, ap_ref, u_ref, v_ref, pw_ref, ps_ref, lg_ref, lb_ref, sw_ref, sb_ref,
                   pm_ref, sg_ref, *, tiles_per_seq):
    i = pl.program_id(0)
    tm = a_ref.shape[0]
    it = i % tiles_per_seq
    gw = POOL_WIDTH // len(POOL_WINDOWS)

    a = a_ref[...].astype(F32)
    prev = jnp.where(it == 0, 0.0, ap_ref[...].astype(F32))
    ext = jnp.concatenate([prev, a], axis=0)
    tpos = (it * tm + lax.broadcasted_iota(jnp.int32, (tm, 1), 0) + 1).astype(F32)
    for g, w in enumerate(POOL_WINDOWS):
        s = ext[:, g * gw:(g + 1) * gw]
        k = 1
        while k < w:
            s = s + pltpu.roll(s, k, 0)
            k *= 2
        d = s[POOL_HALO:] / jnp.minimum(tpos, float(w)) - a[:, g * gw:(g + 1) * gw]
        y = _dot(d.astype(BF16), pw_ref[g]) * ps_ref[:, g * gw:(g + 1) * gw]
        pm_ref[:, g * gw:(g + 1) * gw] = y.astype(pm_ref.dtype)

    u = jax.nn.gelu(u_ref[...].astype(F32))
    v = _layer_norm_rows(jax.nn.gelu(v_ref[...].astype(F32)), lg_ref[...], lb_ref[...]).astype(BF16)
    hc = SG_WIDTH // SG_HEADS
    tri = (lax.broadcasted_iota(jnp.int32, (SG_CHUNK, SG_CHUNK), 0)
           >= lax.broadcasted_iota(jnp.int32, (SG_CHUNK, SG_CHUNK), 1))
    for g in range(SG_HEADS):
        ws = jnp.where(tri, sw_ref[g], 0.0).astype(BF16)
        bias = sb_ref[:, g:g + 1]
        for c in range(tm // SG_CHUNK):
            rows = slice(c * SG_CHUNK, (c + 1) * SG_CHUNK)
            cols = slice(g * hc, (g + 1) * hc)
            mixed = _dot(ws, v[rows, cols]) + bias
            sg_ref[rows, cols] = (u[rows, cols] * mixed).astype(sg_ref.dtype)


def _mixers(mix, pool_w, pool_scale, ln_g, ln_b, sg_w, sg_b, seq, tm):
    t = mix.shape[0]
    halo_blocks = tm // POOL_HALO
    full = lambda shape: pl.BlockSpec(shape, lambda i: (0,) * len(shape))
    return pl.pallas_call(
        functools.partial(_mixers_kernel, tiles_per_seq=seq // tm),
        grid=(t // tm,),
        in_specs=[pl.BlockSpec((tm, POOL_WIDTH), lambda i: (i, 0)),
                  pl.BlockSpec((POOL_HALO, POOL_WIDTH), lambda i: (jnp.maximum(i * halo_blocks - 1, 0), 0)),
                  pl.BlockSpec((tm, SG_WIDTH), lambda i: (i, 1)),
                  pl.BlockSpec((tm, SG_WIDTH), lambda i: (i, 2)),
                  full(pool_w.shape), full((1, POOL_WIDTH)), full((1, SG_WIDTH)), full((1, SG_WIDTH)),
                  full(sg_w.shape), full((SG_CHUNK, SG_HEADS))],
        out_specs=[pl.BlockSpec((tm, POOL_WIDTH), lambda i: (i, 0)),
                   pl.BlockSpec((tm, SG_WIDTH), lambda i: (i, 0))],
        out_shape=[jax.ShapeDtypeStruct((t, POOL_WIDTH), BF16), jax.ShapeDtypeStruct((t, SG_WIDTH), BF16)],
        compiler_params=_params(("parallel",)),
        name="mixers",
    )(mix, mix, mix, mix, pool_w.astype(BF16), pool_scale[None], ln_g[None], ln_b[None], sg_w, sg_b.T)


def _compress_kernel(src_ref, pos_ref, w1_ref, w2_ref, w2T_ref, o_ref, oT_ref):
    nc = o_ref.shape[0]
    first = jnp.zeros((nc, CMP_HIDDEN), F32)
    second = jnp.zeros((nc, CMP_HIDDEN), F32)
    for p in range(CMP_STRIDE):
        rows = src_ref[pl.ds(p, nc, stride=CMP_STRIDE), :]
        first = first + _dot((rows + pos_ref[p:p + 1, :]).astype(BF16), w1_ref[p])
        q = CMP_STRIDE + p
        second = second + _dot((rows + pos_ref[q:q + 1, :]).astype(BF16), w1_ref[q])
    pre = first + pltpu.roll(second, nc - 1, 0)
    hidden = jax.nn.gelu(pre).astype(BF16)
    o_ref[...] = _dot(hidden, w2_ref[...]).astype(o_ref.dtype)
    oT_ref[...] = _dot_nt(w2T_ref[...], hidden).astype(oT_ref.dtype)


def _compress(src, pos, w1, w2, w2T):
    _, b, h, seq, _ = src.shape
    nc = seq // CMP_STRIDE
    sel = lambda n: pl.BlockSpec((None,) + n, lambda s, bi, hi: (s,) + (0,) * len(n))
    return pl.pallas_call(
        _compress_kernel,
        grid=(2, b, h),
        in_specs=[pl.BlockSpec((None, None, None, seq, LANE), lambda s, bi, hi: (s, bi, hi, 0, 0)),
                  sel(pos.shape[1:]), sel(w1.shape[1:]), sel(w2.shape[1:]), sel(w2T.shape[1:])],
        out_specs=[pl.BlockSpec((None, None, None, nc, HEAD_DIM), lambda s, bi, hi: (s, bi, hi, 0, 0)),
                   pl.BlockSpec((None, None, None, HEAD_DIM, nc), lambda s, bi, hi: (s, bi, hi, 0, 0))],
        out_shape=[jax.ShapeDtypeStruct((2, b, h, nc, HEAD_DIM), BF16),
                   jax.ShapeDtypeStruct((2, b, h, HEAD_DIM, nc), BF16)],
        compiler_params=_params(("parallel", "parallel", "parallel")),
        name="compress",
    )(src, pos, w1, w2, w2T)


def _prep_compress(k_pos, k_w1, k_w2, v_pos, v_w1, v_w2):
    pad = LANE - HEAD_DIM
    pos = jnp.pad(jnp.stack([k_pos, v_pos]), ((0, 0), (0, 0), (0, pad)))
    w1 = jnp.stack([k_w1, v_w1]).reshape(2, CMP_LEN, HEAD_DIM, CMP_HIDDEN)
    w1 = jnp.pad(w1, ((0, 0), (0, 0), (0, pad), (0, 0))).astype(BF16)
    w2 = jnp.stack([k_w2, v_w2]).astype(BF16)
    return pos, w1, w2, w2.transpose(0, 2, 1)


def _score_strips(k_ref, k0, n_strips, q_aug, s_ref, mask):
    mx = None
    for r in range(n_strips):
        start = pl.multiple_of(k0 + r * NSA_STRIP, NSA_STRIP)
        s = _dot(k_ref[pl.ds(start, NSA_STRIP), :], q_aug).astype(s_ref.dtype)
        if mask is not None:
            s = mask(r, start, s)
        s_ref[r * NSA_STRIP:(r + 1) * NSA_STRIP, :] = s
        mx = s if mx is None else jnp.maximum(mx, s)
    return jnp.max(mx.astype(F32), axis=0, keepdims=True)


def _prob_strips(n_strips, m, s_ref, p_ref):
    m = m.astype(s_ref.dtype)
    for r in range(n_strips):
        rows = slice(r * NSA_STRIP, (r + 1) * NSA_STRIP)
        p_ref[rows, :] = jnp.exp2(s_ref[rows, :] - m).astype(BF16)


def _nsa_kernel(qT_ref, kc_ref, vcT_ref, ks_ref, vsT_ref, kw_ref, vwT_ref, g_ref, bound_ref, o_ref,
                s_ref, p_ref, pc_ref, s2_ref, p2_ref, psum_ref, selneg_ref, *, n_top):
    n = pl.program_id(2)
    qs = n * Q_BLOCK
    width = GROUP * Q_BLOCK
    qT = qT_ref[...]
    t_row = qs + lax.broadcasted_iota(jnp.int32, (1, width), 1) % Q_BLOCK
    nc = kc_ref.shape[0]
    ns = selneg_ref.shape[0]
    strip_iota = lax.broadcasted_iota(jnp.int32, (NSA_STRIP, 1), 0)

    per = SLC_LEN // CMP_STRIDE
    psum_ref[:PSUM_PAD, :] = jnp.zeros((PSUM_PAD, Q_BLOCK), F32)

    def compressed_and_selection(n_tiles):
        keys = n_tiles * CMP_TILE
        rows = keys // per
        strip_rows = [slice(r * NSA_STRIP, (r + 1) * NSA_STRIP) for r in range(keys // NSA_STRIP)]

        mx = None
        for sl in strip_rows:
            s = _dot(kc_ref[sl, :], qT)
            c_end = (sl.start + strip_iota) * CMP_STRIDE + (CMP_LEN - 1)
            s = jnp.where(c_end <= t_row, s, NEG_INF)
            s_ref[sl, :] = s
            mx = s if mx is None else jnp.maximum(mx, s)
        m_cmp = jnp.maximum(jnp.max(mx, axis=0, keepdims=True), 0.1 * NEG_INF)
        lsum = jnp.zeros((NSA_STRIP, width), F32)
        for sl in strip_rows:
            e = jnp.exp2(s_ref[sl, :] - m_cmp)
            s_ref[sl, :] = e
            pc_ref[sl, :] = e.astype(BF16)
            lsum = lsum + e
        l = jnp.sum(lsum, axis=0, keepdims=True)
        inv = jnp.where(l > 0.0, 1.0 / l, 0.0)
        out = _dot(vcT_ref[:, :keys], pc_ref[:keys, :]) * inv
        for sl in strip_rows:
            p = s_ref[sl, :] * inv
            psum = p[:, :Q_BLOCK]
            for g in range(1, GROUP):
                psum = psum + p[:, g * Q_BLOCK:(g + 1) * Q_BLOCK]
            psum_ref[PSUM_PAD + sl.start:PSUM_PAD + sl.stop, :] = psum

        part = [psum_ref[pl.ds(PSUM_PAD + k, rows, stride=per), :] for k in range(per)]
        before = psum_ref[pl.ds(PSUM_PAD - 1, rows, stride=per), :]
        imp = 2.0 * (part[0] + part[1] + part[2]) + part[3] + before

        blk = lax.broadcasted_iota(jnp.int32, (rows, Q_BLOCK), 0)
        cur = (qs + lax.broadcasted_iota(jnp.int32, (rows, Q_BLOCK), 1)) // SLC_LEN
        taken = -2.0
        score = jnp.where(blk <= cur, imp, -1.0)
        score = jnp.where(blk == 0, taken, jnp.where(blk == cur, taken, jnp.where(blk == cur - 1, taken, score)))
        blk_f = blk.astype(F32)
        for _ in range(n_top - 3):
            top = jnp.max(score, axis=0, keepdims=True)
            cand = jnp.where(score == top, blk_f, float(rows))
            first = jnp.min(cand, axis=0, keepdims=True)
            score = jnp.where(cand == first, taken, score)
        selneg_ref[:rows, :] = jnp.where(score == taken, 0.0, NEG_INF)
        if rows < ns:
            selneg_ref[rows:, :] = jnp.full((ns - rows, Q_BLOCK), NEG_INF, F32)
        return out

    reach = CMP_TILE * CMP_STRIDE // Q_BLOCK
    o_cmp = lax.switch(n // reach, [functools.partial(compressed_and_selection, t + 1) for t in range(nc // CMP_TILE)])

    strips = SLC_BLOCK // NSA_STRIP
    sel_rows = ks_ref.shape[1] - HEAD_DIM - SEL_GROUP
    last_block = ks_ref.shape[0] // SLC_BLOCK - 1
    n_diag = n // (SLC_BLOCK // Q_BLOCK)

    def past_only(r, start, s):
        return jnp.where(start + strip_iota <= t_row, s, NEG_INF)

    def slc_scores(kb, slot):
        per_block = SLC_BLOCK // Q_BLOCK
        variant = jnp.where(kb < n_diag, 0, jnp.where(kb > n_diag, 1 + per_block, 1 + n % per_block))

        def limit(r, start, s):
            return jnp.minimum(s, bound_ref[variant, r * NSA_STRIP:(r + 1) * NSA_STRIP, :])

        kb = jnp.minimum(kb, last_block)
        k0 = pl.multiple_of(kb * SLC_BLOCK, SLC_BLOCK)
        g0 = pl.multiple_of((kb * strips // SEL_GROUP) * SEL_GROUP, SEL_GROUP)
        bias = jnp.concatenate([selneg_ref[pl.ds(g0, SEL_GROUP), :]] * GROUP, axis=1)
        bias = jnp.concatenate([bias, jnp.zeros((sel_rows, width), F32)], axis=0).astype(BF16)
        q_aug = jnp.concatenate([qT, bias], axis=0)
        return _score_strips(ks_ref, k0, strips, q_aug, s2_ref.at[slot], limit)

    def slc_consume(kb, slot, m, acc, mx):
        k0 = pl.multiple_of(kb * SLC_BLOCK, SLC_BLOCK)
        m_new = jnp.maximum(m, mx)
        _prob_strips(strips, m_new, s2_ref.at[slot], p2_ref.at[slot])
        pv = _dot(vsT_ref[:, pl.ds(k0, SLC_BLOCK)], p2_ref[slot])
        return m_new, jnp.exp2(m - m_new) * acc + pv

    def slc_group(j, carry):
        m, acc, pending = carry[0], carry[1], list(carry[2:])
        for u in range(SLC_UNROLL):
            kb = SLC_UNROLL * j + u
            pending.append(slc_scores(kb + SLC_AHEAD, (u + SLC_AHEAD) % SLC_UNROLL))
            m, acc = slc_consume(kb, u, m, acc, pending.pop(0))
        return (m, acc, *pending)

    init = (jnp.full((1, width), NEG_INF, F32), jnp.zeros((vsT_ref.shape[0], width), F32),
            *[slc_scores(a, a) for a in range(SLC_AHEAD)])

    span = WINDOW + Q_BLOCK
    w_strips = span // NSA_STRIP
    edge = Q_BLOCK // NSA_STRIP
    w0 = pl.multiple_of(jnp.maximum(qs - WINDOW, 0), Q_BLOCK)

    def band(r, start, s):
        if r < edge:
            diff = t_row - (start + strip_iota)
            return jnp.where(diff >= 0, jnp.where(diff < WINDOW, s, NEG_INF), NEG_INF)
        return past_only(r, start, s)

    q_win = jnp.concatenate([qT, jnp.zeros((kw_ref.shape[1] - HEAD_DIM, width), BF16)], axis=0)
    m_win = _score_strips(kw_ref, w0, w_strips, q_win, s_ref, band)
    _prob_strips(w_strips, m_win, s_ref, p_ref)
    pv = _dot(vwT_ref[:, pl.ds(w0, span)], p_ref[:span, :])
    o_win = pv[:HEAD_DIM] / pv[HEAD_DIM:HEAD_DIM + 1]

    acc = lax.fori_loop(0, (n_diag + SLC_UNROLL) // SLC_UNROLL, slc_group, init)[1]
    o_slc = acc[:HEAD_DIM] / acc[HEAD_DIM:HEAD_DIM + 1]

    gates = jax.nn.sigmoid(g_ref[...].astype(F32))

    def gate(c):
        return jnp.concatenate([gates[g * 3 + c:g * 3 + c + 1, :] for g in range(GROUP)], axis=1)

    o = gate(0) * o_cmp + gate(1) * o_slc + gate(2) * o_win
    for g in range(GROUP):
        o_ref[:, g * HEAD_DIM:(g + 1) * HEAD_DIM] = o[:, g * Q_BLOCK:(g + 1) * Q_BLOCK].T.astype(o_ref.dtype)


def _causal_bounds():
    big = np.float32(3.0e38)
    key = np.arange(SLC_BLOCK)[:, None]
    query = np.tile(np.arange(Q_BLOCK), GROUP)[None, :]
    tables = [np.full((SLC_BLOCK, GROUP * Q_BLOCK), big, np.float32)]
    for v in range(SLC_BLOCK // Q_BLOCK):
        tables.append(np.where(key <= query + v * Q_BLOCK, big, np.float32(NEG_INF)).astype(np.float32))
    tables.append(np.full((SLC_BLOCK, GROUP * Q_BLOCK), NEG_INF, np.float32))
    return jnp.asarray(np.stack(tables))


def _nsa(qT, cmp, cmpT, keys, valT, gates, seq):
    b, h = qT.shape[:2]
    nq = seq // Q_BLOCK
    nc = cmp.shape[3]
    ns = seq // SLC_LEN
    width = GROUP * Q_BLOCK
    assert nc == ns * (SLC_LEN // CMP_STRIDE) and nc % CMP_TILE == 0 and seq % (SLC_BLOCK * SLC_UNROLL) == 0
    assert seq >= WINDOW + Q_BLOCK and ns >= SLC_TOP
    bounds = _causal_bounds()
    per_head = lambda arr, k: pl.BlockSpec((None, None, None) + arr.shape[3:], lambda bi, hi, n, k=k: (k, bi, hi, 0, 0))
    return pl.pallas_call(
        functools.partial(_nsa_kernel, n_top=SLC_TOP),
        grid=(b, h, nq),
        in_specs=[pl.BlockSpec((None, None, HEAD_DIM, width), lambda bi, hi, n: (bi, hi, 0, n)),
                  per_head(cmp, 0), per_head(cmpT, 1),
                  per_head(keys, 0), per_head(valT, 0), per_head(keys, 1), per_head(valT, 1),
                  pl.BlockSpec((None, None, 16, Q_BLOCK), lambda bi, hi, n: (bi, hi, 0, n)),
                  pl.BlockSpec(bounds.shape, lambda bi, hi, n: (0, 0, 0), pipeline_mode=pl.Buffered(1))],
        out_specs=pl.BlockSpec((Q_BLOCK, GROUP * HEAD_DIM), lambda bi, hi, n: (bi * nq + n, hi)),
        out_shape=jax.ShapeDtypeStruct((b * seq, Q_WIDTH), BF16),
        scratch_shapes=[pltpu.VMEM((max(nc, WINDOW + Q_BLOCK), width), F32),
                        pltpu.VMEM((WINDOW + Q_BLOCK, width), BF16),
                        pltpu.VMEM((nc, width), BF16),
                        pltpu.VMEM((SLC_UNROLL, SLC_BLOCK, width), F32),
                        pltpu.VMEM((SLC_UNROLL, SLC_BLOCK, width), BF16),
                        pltpu.VMEM((PSUM_PAD + nc, Q_BLOCK), F32),
                        pltpu.VMEM((ns, Q_BLOCK), F32)],
        compiler_params=_params(("parallel", "parallel", "arbitrary")),
        name="nsa",
    )(qT, cmp, cmpT, keys, valT, keys, valT, gates, bounds)


def _merge_kernel(*refs, with_router, alpha):
    (pm_ref, sg_ref, on_ref, gp_ref, gs_ref, gn_ref, x_ref, wp_ref, ws_ref, wn_ref, wo_ref, lg_ref, lb_ref) = refs[:13]
    if with_router:
        wr_ref, br_ref, x1_ref, route_ref, x1t_ref = refs[13:]
    else:
        (x1_ref,) = refs[13:]
    sig = lambda r: jax.nn.sigmoid(r[...].astype(F32))
    y = (sig(gp_ref) * _dot(pm_ref[...], wp_ref[...]) + sig(gs_ref) * _dot(sg_ref[...], ws_ref[...])
         + sig(gn_ref) * _dot(on_ref[...], wn_ref[...]))
    hmix = _dot(y.astype(BF16), wo_ref[...])
    x1 = _layer_norm_rows(alpha * x_ref[...] + hmix, lg_ref[...], lb_ref[...])
    x1_ref[...] = x1
    if with_router:
        _store_token_tiles(x1t_ref, x1)
        wr = wr_ref[...]
        w_hi = wr.astype(BF16)
        w_lo = (wr - w_hi.astype(F32)).astype(BF16)
        x_hi = x1.astype(BF16)
        x_lo = (x1 - x_hi.astype(F32)).astype(BF16)
        logits = _dot(x_hi, w_hi) + (_dot(x_lo, w_hi) + _dot(x_hi, w_lo)) + br_ref[...]
        lane = lax.broadcasted_iota(jnp.int32, logits.shape, 1)
        lane_f = lane.astype(F32)
        logits = jnp.where(lane < N_EXPERTS, logits, -jnp.inf)
        m1 = jnp.max(logits, axis=-1, keepdims=True)
        i1 = jnp.min(jnp.where(logits == m1, lane_f, float(LANE)), axis=-1, keepdims=True)
        rest = jnp.where(lane_f == i1, -jnp.inf, logits)
        m2 = jnp.max(rest, axis=-1, keepdims=True)
        i2 = jnp.min(jnp.where(rest == m2, lane_f, float(LANE)), axis=-1, keepdims=True)
        e2 = jnp.exp(m2 - m1)
        w1 = 1.0 / (1.0 + e2)
        route_ref[...] = jnp.where(lane == 0, i1, jnp.where(lane == 1, i2, jnp.where(lane == 2, w1,
                                   jnp.where(lane == 3, e2 * w1, 0.0))))


def _merge(pm, sgm, on, br, x2, wp, ws, wn, wo, lg, lb, alpha, tm, router=None):
    t = x2.shape[0]
    row = lambda width, cb=0: pl.BlockSpec((tm, width), lambda i, cb=cb: (i, cb))
    full = lambda shape: pl.BlockSpec(shape, lambda i: (0,) * len(shape))
    in_specs = [row(POOL_WIDTH), row(SG_WIDTH), row(Q_WIDTH),
                row(D_MODEL, 0), row(D_MODEL, 1), row(D_MODEL, 2),
                row(D_MODEL), full(wp.shape), full(ws.shape), full(wn.shape), full(wo.shape),
                full((1, D_MODEL)), full((1, D_MODEL))]
    args = [pm, sgm, on, br, br, br, x2, wp.astype(BF16), ws.astype(BF16), wn.astype(BF16), wo.astype(BF16),
            lg[None], lb[None]]
    out_specs = [row(D_MODEL)]
    out_shape = [jax.ShapeDtypeStruct((t, D_MODEL), F32)]
    if router is not None:
        w_router, b_router = router
        pad = LANE - N_EXPERTS
        in_specs += [full((D_MODEL, LANE)), full((1, LANE))]
        args += [jnp.pad(w_router, ((0, 0), (0, pad))), jnp.pad(b_router, (0, pad))[None]]
        out_specs += [row(LANE), pl.BlockSpec((tm * TOKEN_TILE, LANE), lambda i: (i, 0))]
        out_shape += [jax.ShapeDtypeStruct((t, LANE), F32), jax.ShapeDtypeStruct((t * TOKEN_TILE, LANE), F32)]
    out = pl.pallas_call(
        functools.partial(_merge_kernel, with_router=router is not None, alpha=alpha),
        grid=(t // tm,), in_specs=in_specs, out_specs=out_specs, out_shape=out_shape,
        compiler_params=_params(("parallel",)),
        name="merge",
    )(*args)
    return out if router is not None else out[0]


def _tile_rows(i):
    start = i * TOKEN_TILE if isinstance(i, int) else pl.multiple_of(i * TOKEN_TILE, TOKEN_TILE)
    return pl.ds(start, TOKEN_TILE)


def _for_rows(n, fn):
    def body(r, c):
        fn(r)
        return c
    lax.fori_loop(0, n, body, 0, unroll=8)


def _moe_kernel(be_ref, tok_ref, tok_next_ref, slot_ref, x_hbm, wg_ref, wu_ref, wd_ref, y_hbm,
                xg_ref, xb_ref, acc_ref, yt_ref, gather_sem, scatter_sem):
    i = pl.program_id(0)
    k = pl.program_id(1)
    n_blocks = pl.num_programs(0)
    rows = xb_ref.shape[0]

    def gather(idx_ref, slot):
        return lambda r: pltpu.make_async_copy(x_hbm.at[_tile_rows(idx_ref[0, r])], xg_ref.at[slot, _tile_rows(r)],
                                               gather_sem.at[slot])

    def scatter(idx_ref):
        return lambda r: pltpu.make_async_copy(yt_ref.at[_tile_rows(r)], y_hbm.at[_tile_rows(idx_ref[0, r])],
                                               scatter_sem)

    def wait_gather(slot):
        pltpu.make_async_copy(x_hbm.at[pl.ds(0, rows * TOKEN_TILE)], xg_ref.at[slot], gather_sem.at[slot]).wait()

    @pl.when(k == 0)
    def _():
        slot = i % 2

        @pl.when(i == 0)
        def _():
            _for_rows(rows, lambda r: gather(tok_ref, 0)(r).start())

        wait_gather(slot)
        for s in range(TOKEN_TILE):
            xb_ref[:, s * LANE:(s + 1) * LANE] = xg_ref[slot, pl.ds(s, rows, stride=TOKEN_TILE), :].astype(BF16)

    share = rows // (D_FF // FF_TILE)
    for r in range(share):
        gather(tok_next_ref, 1 - i % 2)(k * share + r).start()

    xb = xb_ref[...]
    hidden = (jax.nn.silu(_dot(xb, wg_ref[...])) * _dot(xb, wu_ref[...])).astype(BF16)
    part = _dot(hidden, wd_ref[...])

    @pl.when(k == 0)
    def _():
        acc_ref[...] = part

    @pl.when(k == pl.num_programs(1) - 1)
    def _():
        def wait_scatter():
            pltpu.make_async_copy(yt_ref, y_hbm.at[pl.ds(0, rows * TOKEN_TILE)], scatter_sem).wait()

        pl.when(i > 0)(wait_scatter)
        _store_token_tiles(yt_ref, acc_ref[...] + part)
        _for_rows(rows, lambda r: scatter(slot_ref)(r).start())

        @pl.when(i == n_blocks - 1)
        def _():
            wait_scatter()
            wait_gather(1 - i % 2)


def _moe(x_tiles, row_tok, row_slot, n_slots, block_e, wg, wu, wd):
    rows = EXPERT_BLOCK
    n_blocks = row_tok.shape[0] // rows
    kf = D_FF // FF_TILE
    assert kf == 2
    idx_spec = lambda fn: pl.BlockSpec((None, 1, rows), lambda i, k, be: (fn(i), 0, 0), memory_space=pltpu.SMEM)
    tok3 = row_tok.reshape(n_blocks, 1, rows)
    slot3 = row_slot.reshape(n_blocks, 1, rows)
    return pl.pallas_call(
        _moe_kernel,
        grid_spec=pltpu.PrefetchScalarGridSpec(
            num_scalar_prefetch=1,
            grid=(n_blocks, kf),
            in_specs=[idx_spec(lambda i: i), idx_spec(lambda i: jnp.minimum(i + 1, n_blocks - 1)),
                      idx_spec(lambda i: i),
                      pl.BlockSpec(memory_space=pl.ANY),
                      pl.BlockSpec((None, D_MODEL, FF_TILE), lambda i, k, be: (be[i], 0, k)),
                      pl.BlockSpec((None, D_MODEL, FF_TILE), lambda i, k, be: (be[i], 0, k)),
                      pl.BlockSpec((None, FF_TILE, D_MODEL), lambda i, k, be: (be[i], k, 0))],
            out_specs=pl.BlockSpec(memory_space=pl.ANY),
            scratch_shapes=[pltpu.VMEM((2, rows * TOKEN_TILE, LANE), F32), pltpu.VMEM((rows, D_MODEL), BF16),
                            pltpu.VMEM((rows, D_MODEL), F32), pltpu.VMEM((rows * TOKEN_TILE, LANE), F32),
                            pltpu.SemaphoreType.DMA((2,)), pltpu.SemaphoreType.DMA(())]),
        out_shape=jax.ShapeDtypeStruct((n_slots * TOKEN_TILE, LANE), F32),
        compiler_params=_params(("arbitrary", "arbitrary")),
        name="moe",
    )(block_e, tok3, tok3, slot3, x_tiles, wg, wu, wd)


def _swiglu_kernel(x_ref, wg_ref, wu_ref, wd_ref, o_ref, xb_ref):
    k = pl.program_id(1)

    @pl.when(k == 0)
    def _():
        xb_ref[...] = x_ref[...].astype(BF16)

    xb = xb_ref[...]
    hidden = (jax.nn.silu(_dot(xb, wg_ref[...])) * _dot(xb, wu_ref[...])).astype(BF16)
    part = _dot(hidden, wd_ref[...])

    @pl.when(k == 0)
    def _():
        o_ref[...] = part

    @pl.when(k != 0)
    def _():
        o_ref[...] += part


def _swiglu(rows, wg, wu, wd, tm):
    r = rows.shape[0]
    row = pl.BlockSpec((tm, D_MODEL), lambda i, k: (i, 0))
    return pl.pallas_call(
        _swiglu_kernel,
        grid=(r // tm, D_FF // FF_TILE),
        in_specs=[row,
                  pl.BlockSpec((D_MODEL, FF_TILE), lambda i, k: (0, k)),
                  pl.BlockSpec((D_MODEL, FF_TILE), lambda i, k: (0, k)),
                  pl.BlockSpec((FF_TILE, D_MODEL), lambda i, k: (k, 0))],
        out_specs=row,
        out_shape=jax.ShapeDtypeStruct(rows.shape, F32),
        scratch_shapes=[pltpu.VMEM((tm, D_MODEL), BF16)],
        compiler_params=_params(("parallel", "arbitrary")),
        name="swiglu",
    )(rows, wg, wu, wd)


def _ple_ln2_kernel(*refs, n_parts, alpha):
    x_ref, p_ref = refs[0], refs[1]
    parts = refs[2:2 + n_parts]
    rest = refs[2 + n_parts:]
    if n_parts > 1:
        route_ref, rest = rest[0], rest[1:]
    wg_ref, bg_ref, wp_ref, lg_ref, lb_ref, o_ref = rest
    x1 = x_ref[...]
    if n_parts > 1:
        route = route_ref[...]
        f = _load_token_tiles(parts[0]) * route[:, TOP_K:TOP_K + 1]
        for k in range(1, n_parts):
            f = f + _load_token_tiles(parts[k]) * route[:, TOP_K + k:TOP_K + k + 1]
    else:
        f = parts[0][...]
    gate = jax.nn.sigmoid(_dot(x1.astype(BF16), wg_ref[...]) + bg_ref[...])
    ple = gate * _dot(p_ref[...].astype(BF16), wp_ref[...])
    o_ref[...] = _layer_norm_rows(alpha * x1 + f + ple, lg_ref[...], lb_ref[...])


def _ple_ln2(x1, p2, y, route, wg, bg, wp, lg, lb, alpha, tm):
    t = x1.shape[0]
    n_parts = 1 if route is None else TOP_K
    steps = t // tm
    row = lambda width: pl.BlockSpec((tm, width), lambda i: (i, 0))
    full = lambda shape: pl.BlockSpec(shape, lambda i: (0,) * len(shape))
    in_specs = [row(D_MODEL), row(PLE_DIM)]
    if n_parts > 1:
        in_specs += [pl.BlockSpec((tm * TOKEN_TILE, LANE), lambda i, k=k: (k * steps + i, 0)) for k in range(n_parts)]
    else:
        in_specs.append(row(D_MODEL))
    args = [x1, p2] + [y] * n_parts
    if n_parts > 1:
        in_specs.append(row(LANE))
        args.append(route)
    in_specs += [full((D_MODEL, D_MODEL)), full((1, D_MODEL)), full((PLE_DIM, D_MODEL)),
                 full((1, D_MODEL)), full((1, D_MODEL))]
    args += [wg.astype(BF16), bg[None], wp.astype(BF16), lg[None], lb[None]]
    return pl.pallas_call(
        functools.partial(_ple_ln2_kernel, n_parts=n_parts, alpha=alpha),
        grid=(steps,), in_specs=in_specs, out_specs=row(D_MODEL),
        out_shape=jax.ShapeDtypeStruct((t, D_MODEL), F32),
        compiler_params=_params(("parallel",)),
        name="ple_ln2",
    )(*args)


def _route_tables(route, t):
    flat_e = route[:, :TOP_K].astype(jnp.int32).reshape(-1)
    tk = flat_e.shape[0]
    onehot = (flat_e[:, None] == jnp.arange(N_EXPERTS, dtype=jnp.int32)[None, :]).astype(jnp.int32)
    before = jnp.cumsum(onehot, axis=0) - onehot
    rank = jnp.sum(before * onehot, axis=1)
    counts = jnp.sum(onehot, axis=0)
    padded = (counts + EXPERT_BLOCK - 1) // EXPERT_BLOCK * EXPERT_BLOCK
    pend = jnp.cumsum(padded)
    dest = (pend - padded)[flat_e] + rank
    n_blocks = -(-tk // EXPERT_BLOCK) + N_EXPERTS
    n_rows = n_blocks * EXPERT_BLOCK
    row_pair = jnp.full((n_rows,), -1, jnp.int32).at[dest].set(jnp.arange(tk, dtype=jnp.int32))
    is_pad = (row_pair < 0).astype(jnp.int32)
    row_tok = jnp.where(row_pair < 0, 0, row_pair // TOP_K)
    pad_slot = tk + jnp.cumsum(is_pad) - is_pad
    row_slot = jnp.where(row_pair < 0, pad_slot, (row_pair % TOP_K) * t + row_pair // TOP_K).astype(jnp.int32)
    block_start = jnp.arange(n_blocks, dtype=jnp.int32) * EXPERT_BLOCK
    block_e = jnp.sum((pend[None, :] <= block_start[:, None]).astype(jnp.int32), axis=1)
    block_e = jnp.minimum(block_e, N_EXPERTS - 1).astype(jnp.int32)
    return row_tok, row_slot, n_rows, block_e


def kernel(x, p, positions, w_in, pool_w, pool_scale, sg_ln_g, sg_ln_b, sg_w, sg_b, cmp_k_pos, cmp_k_w1, cmp_k_w2, cmp_v_pos, cmp_v_w1, cmp_v_w2, w_pool_out, w_sg_out, w_nsa_out, w_out, ln1_g, ln1_b, ffn_w_gate, ffn_w_up, ffn_w_down, moe_router, moe_router_b, moe_w_gate, moe_w_up, moe_w_down, ple_gate_w, ple_gate_b, ple_proj, ln2_g, ln2_b):
    b, seq, d = x.shape
    depth = w_in.shape[0]
    t = b * seq
    alpha = (2 * depth) ** 0.25
    tm = 512
    tabs = _rope_tables(positions)
    x2 = x.reshape(t, d)
    for i in range(depth):
        mix, br, qT, keys, cmp_src, valT, gates = _proj(x2, _prep_w_in(w_in[i]), tabs, b, seq, tm)
        pm, sgm = _mixers(mix, pool_w[i], pool_scale[i], sg_ln_g[i], sg_ln_b[i], sg_w[i], sg_b[i], seq, tm)
        cmp, cmpT = _compress(cmp_src, *_prep_compress(cmp_k_pos[i], cmp_k_w1[i], cmp_k_w2[i],
                                                       cmp_v_pos[i], cmp_v_w1[i], cmp_v_w2[i]))
        on = _nsa(qT, cmp, cmpT, keys, valT, gates, seq)
        j = i // 2
        moe = i % 2 == 1
        merged = _merge(pm, sgm, on, br, x2, w_pool_out[i], w_sg_out[i], w_nsa_out[i], w_out[i],
                        ln1_g[i], ln1_b[i], alpha, tm, router=(moe_router[j], moe_router_b[j]) if moe else None)
        if moe:
            x1, route, x1_tiles = merged
            row_tok, row_slot, n_slots, block_e = _route_tables(route, t)
            y = _moe(x1_tiles, row_tok, row_slot, n_slots, block_e, moe_w_gate[j].astype(BF16),
                     moe_w_up[j].astype(BF16), moe_w_down[j].astype(BF16))
        else:
            x1, route = merged, None
            y = _swiglu(x1, ffn_w_gate[j].astype(BF16), ffn_w_up[j].astype(BF16), ffn_w_down[j].astype(BF16), tm)
        x2 = _ple_ln2(x1, p[i].reshape(t, PLE_DIM), y, route, ple_gate_w[i], ple_gate_b[i], ple_proj[i],
                      ln2_g[i], ln2_b[i], alpha, tm)
    return x2.reshape(b, seq, d)
```

```python
import functools

import jax
import jax.numpy as jnp
import numpy as np
from jax import lax
from jax.experimental import pallas as pl
from jax.experimental.pallas import tpu as pltpu

F32 = jnp.float32
BF16 = jnp.bfloat16

D_MODEL = 1024
POOL_WIDTH = 512
POOL_WINDOWS = (2, 4, 8, 16)
POOL_HALO = 16
SG_WIDTH = 512
SG_HEADS = 4
SG_CHUNK = 128
NSA_Q_HEADS = 8
NSA_KV_HEADS = 2
GROUP = NSA_Q_HEADS // NSA_KV_HEADS
HEAD_DIM = 64
Q_WIDTH = NSA_Q_HEADS * HEAD_DIM
KV_WIDTH = NSA_KV_HEADS * HEAD_DIM
ROT_DIM = HEAD_DIM // 4
ROPE_THETA = 500000.0
CMP_LEN = 32
CMP_STRIDE = 16
CMP_HIDDEN = 256
SLC_LEN = 64
SLC_TOP = 16
WINDOW = 512
Q_BLOCK = 128
NEG_INF = -1e30
D_FF = 2816
N_EXPERTS = 8
TOP_K = 2
EXPERT_BLOCK = 512
PLE_DIM = 256
LN_EPS = 1e-5

LANE = 128
TOKEN_TILE = D_MODEL // LANE
PROJ_BLOCK = 512
NSA_STRIP = 64
SLC_BLOCK = 256
SLC_UNROLL = 8
SLC_AHEAD = 2
CMP_TILE = 128
SEL_GROUP = 8
ONES_PAD = 16
PSUM_PAD = 8
LOG2_E = 1.4426950408889634
FF_TILE = 1408
VMEM_LIMIT = 56 * 1024 * 1024


def _params(semantics, vmem=VMEM_LIMIT):
    return pltpu.CompilerParams(dimension_semantics=semantics, vmem_limit_bytes=vmem)


def _dot(a, b):
    return jnp.dot(a, b, preferred_element_type=F32)


def _dot_nt(a, b):
    return lax.dot_general(a, b, (((1,), (1,)), ((), ())), preferred_element_type=F32)


def _store_token_tiles(ref, v):
    n = v.shape[0]
    for s in range(TOKEN_TILE):
        ref[pl.ds(s, n, stride=TOKEN_TILE), :] = v[:, s * LANE:(s + 1) * LANE]


def _load_token_tiles(ref):
    n = ref.shape[0] // TOKEN_TILE
    return jnp.concatenate([ref[pl.ds(s, n, stride=TOKEN_TILE), :] for s in range(TOKEN_TILE)], axis=1)


def _layer_norm_rows(v, g, b):
    mu = jnp.mean(v, axis=-1, keepdims=True)
    c = v - mu
    var = jnp.mean(c * c, axis=-1, keepdims=True)
    return c * lax.rsqrt(var + LN_EPS) * g + b


def _rope_tables(positions):
    half = ROT_DIM // 2
    inv = ROPE_THETA ** (-jnp.arange(half, dtype=F32) / half)
    ang = positions.astype(F32)[:, :, None] * inv
    cos, sin = jnp.cos(ang), jnp.sin(ang)
    rest = HEAD_DIM - ROT_DIM
    one = jnp.ones(ang.shape[:2] + (rest,), F32)
    zero_r = jnp.zeros(ang.shape[:2] + (rest,), F32)
    zero_h = jnp.zeros_like(sin)
    c = jnp.concatenate([cos, cos, one], -1)
    s1 = jnp.concatenate([zero_h, sin, zero_r], -1)
    s2 = jnp.concatenate([-sin, zero_h, zero_r], -1)
    reps = LANE // HEAD_DIM
    lane_tabs = tuple(jnp.tile(t, (1, 1, reps)).reshape(-1, LANE) for t in (c, s1, s2))
    return lane_tabs + (cos.reshape(-1, half).T, sin.reshape(-1, half).T)


def _proj_kernel(x_ref, wrow_ref, wkey_ref, wq_ref, wval_ref, wgate_ref, c_ref, s1_ref, s2_ref, cos_ref, sin_ref,
                 mix_ref, br_ref, qT_ref, keys_ref, cmp_ref, valT_ref, gateT_ref, *, tiles_per_seq):
    tm = x_ref.shape[0]
    half = ROT_DIM // 2
    xb = x_ref[...].astype(BF16)

    for ref, col0 in ((mix_ref, 0), (br_ref, mix_ref.shape[1])):
        for col in range(0, ref.shape[1], PROJ_BLOCK):
            ref[:, col:col + PROJ_BLOCK] = _dot(xb, wrow_ref[:, col0 + col:col0 + col + PROJ_BLOCK]).astype(ref.dtype)

    def rope(v):
        return (v * c_ref[...] + pltpu.roll(v, half, 1) * s1_ref[...]
                + pltpu.roll(v, LANE - half, 1) * s2_ref[...])

    tpos = (pl.program_id(0) % tiles_per_seq) * tm + lax.broadcasted_iota(jnp.int32, (tm, LANE), 0)
    lane = lax.broadcasted_iota(jnp.int32, (tm, LANE), 1)
    onehot = jnp.where(lane == HEAD_DIM + (tpos // SLC_LEN) % SEL_GROUP, 1.0, 0.0)
    for k in range(4):
        acc = _dot(xb, wkey_ref[:, k * 2 * LANE:(k + 1) * 2 * LANE])
        for h in range(NSA_KV_HEADS):
            v = acc[:, h * LANE:(h + 1) * LANE]
            if k < 3:
                v = rope(v)
            if k == 0:
                cmp_ref[0, h] = v
            elif k == 1:
                keys_ref[0, h] = (v + onehot).astype(keys_ref.dtype)
            elif k == 2:
                keys_ref[1, h] = v.astype(keys_ref.dtype)
            else:
                cmp_ref[1, h] = v

    cos, sin = cos_ref[...], sin_ref[...]
    q_scale = HEAD_DIM ** -0.5 * LOG2_E
    for h in range(NSA_KV_HEADS):
        qt = _dot_nt(wq_ref[h], xb)
        for g in range(GROUP):
            blk = qt[g * HEAD_DIM:(g + 1) * HEAD_DIM]
            x1, x2 = blk[:half], blk[half:ROT_DIM]
            y = jnp.concatenate([x1 * cos - x2 * sin, x2 * cos + x1 * sin, blk[ROT_DIM:]], axis=0) * q_scale
            y = y.astype(qT_ref.dtype)
            for nn in range(tm // Q_BLOCK):
                dst = nn * GROUP * Q_BLOCK + g * Q_BLOCK
                qT_ref[h, :, dst:dst + Q_BLOCK] = y[:, nn * Q_BLOCK:(nn + 1) * Q_BLOCK]

    vt = _dot_nt(wval_ref[...], xb)
    ones_pad = jnp.where(lax.broadcasted_iota(jnp.int32, (ONES_PAD, tm), 0) == 0, 1.0, 0.0).astype(valT_ref.dtype)
    for k in range(2):
        for h in range(NSA_KV_HEADS):
            row = (k * NSA_KV_HEADS + h) * HEAD_DIM
            valT_ref[k, h, :HEAD_DIM, :] = vt[row:row + HEAD_DIM].astype(valT_ref.dtype)
            valT_ref[k, h, HEAD_DIM:, :] = ones_pad

    gt = _dot_nt(wgate_ref[...], xb)
    for h in range(NSA_KV_HEADS):
        gateT_ref[h] = gt[h * 16:(h + 1) * 16].astype(gateT_ref.dtype)


def _proj(x2, weights, tabs, b, seq, tm):
    t = x2.shape[0]
    tps = seq // tm
    wrow, wkey, wq, wval, wgate = weights
    h = NSA_KV_HEADS
    n_mix = POOL_WIDTH + 2 * SG_WIDTH
    row = lambda i: (i, 0)
    const = lambda shape: pl.BlockSpec(shape, lambda i: (0,) * len(shape), pipeline_mode=pl.Buffered(1))
    return pl.pallas_call(
        functools.partial(_proj_kernel, tiles_per_seq=tps),
        grid=(t // tm,),
        in_specs=[pl.BlockSpec((tm, D_MODEL), row), const(wrow.shape), const(wkey.shape), const(wq.shape),
                  const(wval.shape), const(wgate.shape),
                  pl.BlockSpec((tm, LANE), row), pl.BlockSpec((tm, LANE), row), pl.BlockSpec((tm, LANE), row),
                  pl.BlockSpec((ROT_DIM // 2, tm), lambda i: (0, i)), pl.BlockSpec((ROT_DIM // 2, tm), lambda i: (0, i))],
        out_specs=[pl.BlockSpec((tm, n_mix), row), pl.BlockSpec((tm, 3 * D_MODEL), row),
                   pl.BlockSpec((None, h, HEAD_DIM, tm * GROUP), lambda i: (i // tps, 0, 0, i % tps)),
                   pl.BlockSpec((2, None, h, tm, LANE), lambda i: (0, i // tps, 0, i % tps, 0)),
                   pl.BlockSpec((2, None, h, tm, LANE), lambda i: (0, i // tps, 0, i % tps, 0)),
                   pl.BlockSpec((2, None, h, HEAD_DIM + ONES_PAD, tm), lambda i: (0, i // tps, 0, 0, i % tps)),
                   pl.BlockSpec((None, h, 16, tm), lambda i: (i // tps, 0, 0, i % tps))],
        out_shape=[jax.ShapeDtypeStruct((t, n_mix), BF16), jax.ShapeDtypeStruct((t, 3 * D_MODEL), BF16),
                   jax.ShapeDtypeStruct((b, h, HEAD_DIM, seq * GROUP), BF16),
                   jax.ShapeDtypeStruct((2, b, h, seq, LANE), BF16),
                   jax.ShapeDtypeStruct((2, b, h, seq, LANE), F32),
                   jax.ShapeDtypeStruct((2, b, h, HEAD_DIM + ONES_PAD, seq), BF16),
                   jax.ShapeDtypeStruct((b, h, 16, seq), BF16)],
        compiler_params=_params(("parallel",)),
        name="proj",
    )(x2, wrow, wkey, wq, wval, wgate, *tabs)


def _prep_w_in(w):
    d = w.shape[0]
    h = NSA_KV_HEADS
    n_mix = POOL_WIDTH + 2 * SG_WIDTH
    kv0 = n_mix + Q_WIDTH
    n_gate = NSA_Q_HEADS * 3
    gate0 = kv0 + 6 * KV_WIDTH
    kv = lambda k: w[:, kv0 + k * KV_WIDTH:kv0 + (k + 1) * KV_WIDTH]
    wrow = jnp.concatenate([w[:, :n_mix], w[:, gate0 + n_gate:]], axis=1)
    pad_heads = lambda m: jnp.pad(m.reshape(d, h, HEAD_DIM), ((0, 0), (0, 0), (0, LANE - HEAD_DIM))).reshape(d, h * LANE)
    wkey = jnp.concatenate([pad_heads(kv(0)), pad_heads(kv(2)), pad_heads(kv(4)), pad_heads(kv(1))], axis=1)
    wq = w[:, n_mix:kv0].T.reshape(h, GROUP * HEAD_DIM, d)
    wval = jnp.concatenate([kv(3), kv(5)], axis=1).T
    wgate = jnp.pad(w[:, gate0:gate0 + n_gate].T.reshape(h, GROUP * 3, d), ((0, 0), (0, 16 - GROUP * 3), (0, 0)))
    return tuple(m.astype(BF16) for m in (wrow, wkey, wq, wval, wgate.reshape(h * 16, d)))


def _mixers_kernel(a_ref, ap_ref, u_ref, v_ref, pw_ref, ps_ref, lg_ref, lb_ref, sw_ref, sb_ref,
                   pm_ref, sg_ref, *, tiles_per_seq):
    i = pl.program_id(0)
    tm = a_ref.shape[0]
    it = i % tiles_per_seq
    gw = POOL_WIDTH // len(POOL_WINDOWS)

    a = a_ref[...].astype(F32)
    prev = jnp.where(it == 0, 0.0, ap_ref[...].astype(F32))
    ext = jnp.concatenate([prev, a], axis=0)
    tpos = (it * tm + lax.broadcasted_iota(jnp.int32, (tm, 1), 0) + 1).astype(F32)
    for g, w in enumerate(POOL_WINDOWS):
        s = ext[:, g * gw:(g + 1) * gw]
        k = 1
        while k < w:
            s = s + pltpu.roll(s, k, 0)
            k *= 2
        d = s[POOL_HALO:] / jnp.minimum(tpos, float(w)) - a[:, g * gw:(g + 1) * gw]
        y = _dot(d.astype(BF16), pw_ref[g]) * ps_ref[:, g * gw:(g + 1) * gw]
        pm_ref[:, g * gw:(g + 1) * gw] = y.astype(pm_ref.dtype)

    u = jax.nn.gelu(u_ref[...].astype(F32))
    v = _layer_norm_rows(jax.nn.gelu(v_ref[...].astype(F32)), lg_ref[...], lb_ref[...]).astype(BF16)
    hc = SG_WIDTH // SG_HEADS
    tri = (lax.broadcasted_iota(jnp.int32, (SG_CHUNK, SG_CHUNK), 0)
           >= lax.broadcasted_iota(jnp.int32, (SG_CHUNK, SG_CHUNK), 1))
    for g in range(SG_HEADS):
        ws = jnp.where(tri, sw_ref[g], 0.0).astype(BF16)
        bias = sb_ref[:, g:g + 1]
        for c in range(tm // SG_CHUNK):
            rows = slice(c * SG_CHUNK, (c + 1) * SG_CHUNK)
            cols = slice(g * hc, (g + 1) * hc)
            mixed = _dot(ws, v[rows, cols]) + bias
            sg_ref[rows, cols] = (u[rows, cols] * mixed).astype(sg_ref.dtype)


def _mixers(mix, pool_w, pool_scale, ln_g, ln_b, sg_w, sg_b, seq, tm):
    t = mix.shape[0]
    halo_blocks = tm // POOL_HALO
    full = lambda shape: pl.BlockSpec(shape, lambda i: (0,) * len(shape))
    return pl.pallas_call(
        functools.partial(_mixers_kernel, tiles_per_seq=seq // tm),
        grid=(t // tm,),
        in_specs=[pl.BlockSpec((tm, POOL_WIDTH), lambda i: (i, 0)),
                  pl.BlockSpec((POOL_HALO, POOL_WIDTH), lambda i: (jnp.maximum(i * halo_blocks - 1, 0), 0)),
                  pl.BlockSpec((tm, SG_WIDTH), lambda i: (i, 1)),
                  pl.BlockSpec((tm, SG_WIDTH), lambda i: (i, 2)),
                  full(pool_w.shape), full((1, POOL_WIDTH)), full((1, SG_WIDTH)), full((1, SG_WIDTH)),
                  full(sg_w.shape), full((SG_CHUNK, SG_HEADS))],
        out_specs=[pl.BlockSpec((tm, POOL_WIDTH), lambda i: (i, 0)),
                   pl.BlockSpec((tm, SG_WIDTH), lambda i: (i, 0))],
        out_shape=[jax.ShapeDtypeStruct((t, POOL_WIDTH), BF16), jax.ShapeDtypeStruct((t, SG_WIDTH), BF16)],
        compiler_params=_params(("parallel",)),
        name="mixers",
    )(mix, mix, mix, mix, pool_w.astype(BF16), pool_scale[None], ln_g[None], ln_b[None], sg_w, sg_b.T)


def _compress_kernel(src_ref, pos_ref, w1_ref, w2_ref, w2T_ref, o_ref, oT_ref):
    nc = o_ref.shape[0]
    first = jnp.zeros((nc, CMP_HIDDEN), F32)
    second = jnp.zeros((nc, CMP_HIDDEN), F32)
    for p in range(CMP_STRIDE):
        rows = src_ref[pl.ds(p, nc, stride=CMP_STRIDE), :]
        first = first + _dot((rows + pos_ref[p:p + 1, :]).astype(BF16), w1_ref[p])
        q = CMP_STRIDE + p
        second = second + _dot((rows + pos_ref[q:q + 1, :]).astype(BF16), w1_ref[q])
    pre = first + pltpu.roll(second, nc - 1, 0)
    hidden = jax.nn.gelu(pre).astype(BF16)
    o_ref[...] = _dot(hidden, w2_ref[...]).astype(o_ref.dtype)
    oT_ref[...] = _dot_nt(w2T_ref[...], hidden).astype(oT_ref.dtype)


def _compress(src, pos, w1, w2, w2T):
    _, b, h, seq, _ = src.shape
    nc = seq // CMP_STRIDE
    sel = lambda n: pl.BlockSpec((None,) + n, lambda s, bi, hi: (s,) + (0,) * len(n))
    return pl.pallas_call(
        _compress_kernel,
        grid=(2, b, h),
        in_specs=[pl.BlockSpec((None, None, None, seq, LANE), lambda s, bi, hi: (s, bi, hi, 0, 0)),
                  sel(pos.shape[1:]), sel(w1.shape[1:]), sel(w2.shape[1:]), sel(w2T.shape[1:])],
        out_specs=[pl.BlockSpec((None, None, None, nc, HEAD_DIM), lambda s, bi, hi: (s, bi, hi, 0, 0)),
                   pl.BlockSpec((None, None, None, HEAD_DIM, nc), lambda s, bi, hi: (s, bi, hi, 0, 0))],
        out_shape=[jax.ShapeDtypeStruct((2, b, h, nc, HEAD_DIM), BF16),
                   jax.ShapeDtypeStruct((2, b, h, HEAD_DIM, nc), BF16)],
        compiler_params=_params(("parallel", "parallel", "parallel")),
        name="compress",
    )(src, pos, w1, w2, w2T)


def _prep_compress(k_pos, k_w1, k_w2, v_pos, v_w1, v_w2):
    pad = LANE - HEAD_DIM
    pos = jnp.pad(jnp.stack([k_pos, v_pos]), ((0, 0), (0, 0), (0, pad)))
    w1 = jnp.stack([k_w1, v_w1]).reshape(2, CMP_LEN, HEAD_DIM, CMP_HIDDEN)
    w1 = jnp.pad(w1, ((0, 0), (0, 0), (0, pad), (0, 0))).astype(BF16)
    w2 = jnp.stack([k_w2, v_w2]).astype(BF16)
    return pos, w1, w2, w2.transpose(0, 2, 1)


def _score_strips(k_ref, k0, n_strips, q_aug, s_ref, mask):
    mx = None
    for r in range(n_strips):
        start = pl.multiple_of(k0 + r * NSA_STRIP, NSA_STRIP)
        s = _dot(k_ref[pl.ds(start, NSA_STRIP), :], q_aug).astype(s_ref.dtype)
        if mask is not None:
            s = mask(r, start, s)
        s_ref[r * NSA_STRIP:(r + 1) * NSA_STRIP, :] = s
        mx = s if mx is None else jnp.maximum(mx, s)
    return jnp.max(mx.astype(F32), axis=0, keepdims=True)


def _prob_strips(n_strips, m, s_ref, p_ref):
    m = m.astype(s_ref.dtype)
    for r in range(n_strips):
        rows = slice(r * NSA_STRIP, (r + 1) * NSA_STRIP)
        p_ref[rows, :] = jnp.exp2(s_ref[rows, :] - m).astype(BF16)


def _nsa_kernel(qT_ref, kc_ref, vcT_ref, ks_ref, vsT_ref, kw_ref, vwT_ref, g_ref, bound_ref, o_ref,
                s_ref, p_ref, pc_ref, s2_ref, p2_ref, psum_ref, selneg_ref, *, n_top):
    n = pl.program_id(2)
    qs = n * Q_BLOCK
    width = GROUP * Q_BLOCK
    qT = qT_ref[...]
    t_row = qs + lax.broadcasted_iota(jnp.int32, (1, width), 1) % Q_BLOCK
    nc = kc_ref.shape[0]
    ns = selneg_ref.shape[0]
    strip_iota = lax.broadcasted_iota(jnp.int32, (NSA_STRIP, 1), 0)

    per = SLC_LEN // CMP_STRIDE
    psum_ref[:PSUM_PAD, :] = jnp.zeros((PSUM_PAD, Q_BLOCK), F32)

    def compressed_and_selection(n_tiles):
        keys = n_tiles * CMP_TILE
        rows = keys // per
        strip_rows = [slice(r * NSA_STRIP, (r + 1) * NSA_STRIP) for r in range(keys // NSA_STRIP)]

        mx = None
        for sl in strip_rows:
            s = _dot(kc_ref[sl, :], qT)
            c_end = (sl.start + strip_iota) * CMP_STRIDE + (CMP_LEN - 1)
            s = jnp.where(c_end <= t_row, s, NEG_INF)
            s_ref[sl, :] = s
            mx = s if mx is None else jnp.maximum(mx, s)
        m_cmp = jnp.maximum(jnp.max(mx, axis=0, keepdims=True), 0.1 * NEG_INF)
        lsum = jnp.zeros((NSA_STRIP, width), F32)
        for sl in strip_rows:
            e = jnp.exp2(s_ref[sl, :] - m_cmp)
            s_ref[sl, :] = e
            pc_ref[sl, :] = e.astype(BF16)
            lsum = lsum + e
        l = jnp.sum(lsum, axis=0, keepdims=True)
        inv = jnp.where(l > 0.0, 1.0 / l, 0.0)
        out = _dot(vcT_ref[:, :keys], pc_ref[:keys, :]) * inv
        for sl in strip_rows:
            p = s_ref[sl, :] * inv
            psum = p[:, :Q_BLOCK]
            for g in range(1, GROUP):
                psum = psum + p[:, g * Q_BLOCK:(g + 1) * Q_BLOCK]
            psum_ref[PSUM_PAD + sl.start:PSUM_PAD + sl.stop, :] = psum

        part = [psum_ref[pl.ds(PSUM_PAD + k, rows, stride=per), :] for k in range(per)]
        before = psum_ref[pl.ds(PSUM_PAD - 1, rows, stride=per), :]
        imp = 2.0 * (part[0] + part[1] + part[2]) + part[3] + before

        blk = lax.broadcasted_iota(jnp.int32, (rows, Q_BLOCK), 0)
        cur = (qs + lax.broadcasted_iota(jnp.int32, (rows, Q_BLOCK), 1)) // SLC_LEN
        taken = -2.0
        score = jnp.where(blk <= cur, imp, -1.0)
        score = jnp.where(blk == 0, taken, jnp.where(blk == cur, taken, jnp.where(blk == cur - 1, taken, score)))
        blk_f = blk.astype(F32)
        for _ in range(n_top - 3):
            top = jnp.max(score, axis=0, keepdims=True)
            cand = jnp.where(score == top, blk_f, float(rows))
            first = jnp.min(cand, axis=0, keepdims=True)
            score = jnp.where(cand == first, taken, score)
        selneg_ref[:rows, :] = jnp.where(score == taken, 0.0, NEG_INF)
        if rows < ns:
            selneg_ref[rows:, :] = jnp.full((ns - rows, Q_BLOCK), NEG_INF, F32)
        return out

    reach = CMP_TILE * CMP_STRIDE // Q_BLOCK
    o_cmp = lax.switch(n // reach, [functools.partial(compressed_and_selection, t + 1) for t in range(nc // CMP_TILE)])

    strips = SLC_BLOCK // NSA_STRIP
    sel_rows = ks_ref.shape[1] - HEAD_DIM - SEL_GROUP
    last_block = ks_ref.shape[0] // SLC_BLOCK - 1
    n_diag = n // (SLC_BLOCK // Q_BLOCK)

    def past_only(r, start, s):
        return jnp.where(start + strip_iota <= t_row, s, NEG_INF)

    def slc_scores(kb, slot):
        per_block = SLC_BLOCK // Q_BLOCK
        variant = jnp.where(kb < n_diag, 0, jnp.where(kb > n_diag, 1 + per_block, 1 + n % per_block))

        def limit(r, start, s):
            return jnp.minimum(s, bound_ref[variant, r * NSA_STRIP:(r + 1) * NSA_STRIP, :])

        kb = jnp.minimum(kb, last_block)
        k0 = pl.multiple_of(kb * SLC_BLOCK, SLC_BLOCK)
        g0 = pl.multiple_of((kb * strips // SEL_GROUP) * SEL_GROUP, SEL_GROUP)
        bias = jnp.concatenate([selneg_ref[pl.ds(g0, SEL_GROUP), :]] * GROUP, axis=1)
        bias = jnp.concatenate([bias, jnp.zeros((sel_rows, width), F32)], axis=0).astype(BF16)
        q_aug = jnp.concatenate([qT, bias], axis=0)
        return _score_strips(ks_ref, k0, strips, q_aug, s2_ref.at[slot], limit)

    def slc_consume(kb, slot, m, acc, mx):
        k0 = pl.multiple_of(kb * SLC_BLOCK, SLC_BLOCK)
        m_new = jnp.maximum(m, mx)
        _prob_strips(strips, m_new, s2_ref.at[slot], p2_ref.at[slot])
        pv = _dot(vsT_ref[:, pl.ds(k0, SLC_BLOCK)], p2_ref[slot])
        return m_new, jnp.exp2(m - m_new) * acc + pv

    def slc_group(j, carry):
        m, acc, pending = carry[0], carry[1], list(carry[2:])
        for u in range(SLC_UNROLL):
            kb = SLC_UNROLL * j + u
            pending.append(slc_scores(kb + SLC_AHEAD, (u + SLC_AHEAD) % SLC_UNROLL))
            m, acc = slc_consume(kb, u, m, acc, pending.pop(0))
        return (m, acc, *pending)

    init = (jnp.full((1, width), NEG_INF, F32), jnp.zeros((vsT_ref.shape[0], width), F32),
            *[slc_scores(a, a) for a in range(SLC_AHEAD)])

    span = WINDOW + Q_BLOCK
    w_strips = span // NSA_STRIP
    edge = Q_BLOCK // NSA_STRIP
    w0 = pl.multiple_of(jnp.maximum(qs - WINDOW, 0), Q_BLOCK)

    def band(r, start, s):
        if r < edge:
            diff = t_row - (start + strip_iota)
            return jnp.where(diff >= 0, jnp.where(diff < WINDOW, s, NEG_INF), NEG_INF)
        return past_only(r, start, s)

    q_win = jnp.concatenate([qT, jnp.zeros((kw_ref.shape[1] - HEAD_DIM, width), BF16)], axis=0)
    m_win = _score_strips(kw_ref, w0, w_strips, q_win, s_ref, band)
    _prob_strips(w_strips, m_win, s_ref, p_ref)
    pv = _dot(vwT_ref[:, pl.ds(w0, span)], p_ref[:span, :])
    o_win = pv[:HEAD_DIM] / pv[HEAD_DIM:HEAD_DIM + 1]

    gates = jax.nn.sigmoid(g_ref[...].astype(F32))

    def gate(c):
        return jnp.concatenate([gates[g * 3 + c:g * 3 + c + 1, :] for g in range(GROUP)], axis=1)

    o_rest = gate(0) * o_cmp + gate(2) * o_win
    gate_slc = gate(1)

    acc = lax.fori_loop(0, (n_diag + SLC_UNROLL) // SLC_UNROLL, slc_group, init)[1]
    o = o_rest + gate_slc * (acc[:HEAD_DIM] / acc[HEAD_DIM:HEAD_DIM + 1])
    for g in range(GROUP):
        o_ref[:, g * HEAD_DIM:(g + 1) * HEAD_DIM] = o[:, g * Q_BLOCK:(g + 1) * Q_BLOCK].T.astype(o_ref.dtype)


def _causal_bounds():
    big = np.float32(3.0e38)
    key = np.arange(SLC_BLOCK)[:, None]
    query = np.tile(np.arange(Q_BLOCK), GROUP)[None, :]
    tables = [np.full((SLC_BLOCK, GROUP * Q_BLOCK), big, np.float32)]
    for v in range(SLC_BLOCK // Q_BLOCK):
        tables.append(np.where(key <= query + v * Q_BLOCK, big, np.float32(NEG_INF)).astype(np.float32))
    tables.append(np.full((SLC_BLOCK, GROUP * Q_BLOCK), NEG_INF, np.float32))
    return jnp.asarray(np.stack(tables))


def _nsa(qT, cmp, cmpT, keys, valT, gates, seq):
    b, h = qT.shape[:2]
    nq = seq // Q_BLOCK
    nc = cmp.shape[3]
    ns = seq // SLC_LEN
    width = GROUP * Q_BLOCK
    assert nc == ns * (SLC_LEN // CMP_STRIDE) and nc % CMP_TILE == 0 and seq % (SLC_BLOCK * SLC_UNROLL) == 0
    assert seq >= WINDOW + Q_BLOCK and ns >= SLC_TOP
    bounds = _causal_bounds()
    per_head = lambda arr, k: pl.BlockSpec((None, None, None) + arr.shape[3:], lambda bi, hi, n, k=k: (k, bi, hi, 0, 0))
    return pl.pallas_call(
        functools.partial(_nsa_kernel, n_top=SLC_TOP),
        grid=(b, h, nq),
        in_specs=[pl.BlockSpec((None, None, HEAD_DIM, width), lambda bi, hi, n: (bi, hi, 0, n)),
                  per_head(cmp, 0), per_head(cmpT, 1),
                  per_head(keys, 0), per_head(valT, 0), per_head(keys, 1), per_head(valT, 1),
                  pl.BlockSpec((None, None, 16, Q_BLOCK), lambda bi, hi, n: (bi, hi, 0, n)),
                  pl.BlockSpec(bounds.shape, lambda bi, hi, n: (0, 0, 0), pipeline_mode=pl.Buffered(1))],
        out_specs=pl.BlockSpec((Q_BLOCK, GROUP * HEAD_DIM), lambda bi, hi, n: (bi * nq + n, hi)),
        out_shape=jax.ShapeDtypeStruct((b * seq, Q_WIDTH), BF16),
        scratch_shapes=[pltpu.VMEM((max(nc, WINDOW + Q_BLOCK), width), F32),
                        pltpu.VMEM((WINDOW + Q_BLOCK, width), BF16),
                        pltpu.VMEM((nc, width), BF16),
                        pltpu.VMEM((SLC_UNROLL, SLC_BLOCK, width), F32),
                        pltpu.VMEM((SLC_UNROLL, SLC_BLOCK, width), BF16),
                        pltpu.VMEM((PSUM_PAD + nc, Q_BLOCK), F32),
                        pltpu.VMEM((ns, Q_BLOCK), F32)],
        compiler_params=_params(("parallel", "parallel", "arbitrary")),
        name="nsa",
    )(qT, cmp, cmpT, keys, valT, keys, valT, gates, bounds)


def _merge_kernel(*refs, with_router, alpha):
    (pm_ref, sg_ref, on_ref, gp_ref, gs_ref, gn_ref, x_ref, wp_ref, ws_ref, wn_ref, wo_ref, lg_ref, lb_ref) = refs[:13]
    if with_router:
        wr_ref, br_ref, x1_ref, route_ref, x1t_ref = refs[13:]
    else:
        (x1_ref,) = refs[13:]
    sig = lambda r: jax.nn.sigmoid(r[...].astype(F32))
    y = (sig(gp_ref) * _dot(pm_ref[...], wp_ref[...]) + sig(gs_ref) * _dot(sg_ref[...], ws_ref[...])
         + sig(gn_ref) * _dot(on_ref[...], wn_ref[...]))
    hmix = _dot(y.astype(BF16), wo_ref[...])
    x1 = _layer_norm_rows(alpha * x_ref[...] + hmix, lg_ref[...], lb_ref[...])
    x1_ref[...] = x1
    if with_router:
        _store_token_tiles(x1t_ref, x1)
        wr = wr_ref[...]
        w_hi = wr.astype(BF16)
        w_lo = (wr - w_hi.astype(F32)).astype(BF16)
        x_hi = x1.astype(BF16)
        x_lo = (x1 - x_hi.astype(F32)).astype(BF16)
        logits = _dot(x_hi, w_hi) + (_dot(x_lo, w_hi) + _dot(x_hi, w_lo)) + br_ref[...]
        lane = lax.broadcasted_iota(jnp.int32, logits.shape, 1)
        lane_f = lane.astype(F32)
        logits = jnp.where(lane < N_EXPERTS, logits, -jnp.inf)
        m1 = jnp.max(logits, axis=-1, keepdims=True)
        i1 = jnp.min(jnp.where(logits == m1, lane_f, float(LANE)), axis=-1, keepdims=True)
        rest = jnp.where(lane_f == i1, -jnp.inf, logits)
        m2 = jnp.max(rest, axis=-1, keepdims=True)
        i2 = jnp.min(jnp.where(rest == m2, lane_f, float(LANE)), axis=-1, keepdims=True)
        e2 = jnp.exp(m2 - m1)
        w1 = 1.0 / (1.0 + e2)
        route_ref[...] = jnp.where(lane == 0, i1, jnp.where(lane == 1, i2, jnp.where(lane == 2, w1,
                                   jnp.where(lane == 3, e2 * w1, 0.0))))


def _merge(pm, sgm, on, br, x2, wp, ws, wn, wo, lg, lb, alpha, tm, router=None):
    t = x2.shape[0]
    row = lambda width, cb=0: pl.BlockSpec((tm, width), lambda i, cb=cb: (i, cb))
    full = lambda shape: pl.BlockSpec(shape, lambda i: (0,) * len(shape))
    in_specs = [row(POOL_WIDTH), row(SG_WIDTH), row(Q_WIDTH),
                row(D_MODEL, 0), row(D_MODEL, 1), row(D_MODEL, 2),
                row(D_MODEL), full(wp.shape), full(ws.shape), full(wn.shape), full(wo.shape),
                full((1, D_MODEL)), full((1, D_MODEL))]
    args = [pm, sgm, on, br, br, br, x2, wp.astype(BF16), ws.astype(BF16), wn.astype(BF16), wo.astype(BF16),
            lg[None], lb[None]]
    out_specs = [row(D_MODEL)]
    out_shape = [jax.ShapeDtypeStruct((t, D_MODEL), F32)]
    if router is not None:
        w_router, b_router = router
        pad = LANE - N_EXPERTS
        in_specs += [full((D_MODEL, LANE)), full((1, LANE))]
        args += [jnp.pad(w_router, ((0, 0), (0, pad))), jnp.pad(b_router, (0, pad))[None]]
        out_specs += [row(LANE), pl.BlockSpec((tm * TOKEN_TILE, LANE), lambda i: (i, 0))]
        out_shape += [jax.ShapeDtypeStruct((t, LANE), F32), jax.ShapeDtypeStruct((t * TOKEN_TILE, LANE), F32)]
    out = pl.pallas_call(
        functools.partial(_merge_kernel, with_router=router is not None, alpha=alpha),
        grid=(t // tm,), in_specs=in_specs, out_specs=out_specs, out_shape=out_shape,
        compiler_params=_params(("parallel",)),
        name="merge",
    )(*args)
    return out if router is not None else out[0]


def _tile_rows(i):
    start = i * TOKEN_TILE if isinstance(i, int) else pl.multiple_of(i * TOKEN_TILE, TOKEN_TILE)
    return pl.ds(start, TOKEN_TILE)


def _for_rows(n, fn):
    def body(r, c):
        fn(r)
        return c
    lax.fori_loop(0, n, body, 0, unroll=8)


def _moe_kernel(be_ref, tok_ref, tok_next_ref, slot_ref, x_hbm, wg_ref, wu_ref, wd_ref, y_hbm,
                xg_ref, xb_ref, acc_ref, yt_ref, gather_sem, scatter_sem):
    i = pl.program_id(0)
    k = pl.program_id(1)
    n_blocks = pl.num_programs(0)
    rows = xb_ref.shape[0]

    def gather(idx_ref, slot):
        return lambda r: pltpu.make_async_copy(x_hbm.at[_tile_rows(idx_ref[0, r])], xg_ref.at[slot, _tile_rows(r)],
                                               gather_sem.at[slot])

    def scatter(idx_ref):
        return lambda r: pltpu.make_async_copy(yt_ref.at[_tile_rows(r)], y_hbm.at[_tile_rows(idx_ref[0, r])],
                                               scatter_sem)

    def wait_gather(slot):
        pltpu.make_async_copy(x_hbm.at[pl.ds(0, rows * TOKEN_TILE)], xg_ref.at[slot], gather_sem.at[slot]).wait()

    @pl.when(k == 0)
    def _():
        slot = i % 2

        @pl.when(i == 0)
        def _():
            _for_rows(rows, lambda r: gather(tok_ref, 0)(r).start())

        wait_gather(slot)
        for s in range(TOKEN_TILE):
            xb_ref[:, s * LANE:(s + 1) * LANE] = xg_ref[slot, pl.ds(s, rows, stride=TOKEN_TILE), :].astype(BF16)

    share = rows // (D_FF // FF_TILE)
    for r in range(share):
        gather(tok_next_ref, 1 - i % 2)(k * share + r).start()

    xb = xb_ref[...]
    hidden = (jax.nn.silu(_dot(xb, wg_ref[...])) * _dot(xb, wu_ref[...])).astype(BF16)
    part = _dot(hidden, wd_ref[...])

    @pl.when(k == 0)
    def _():
        acc_ref[...] = part

    @pl.when(k == pl.num_programs(1) - 1)
    def _():
        def wait_scatter():
            pltpu.make_async_copy(yt_ref, y_hbm.at[pl.ds(0, rows * TOKEN_TILE)], scatter_sem).wait()

        pl.when(i > 0)(wait_scatter)
        _store_token_tiles(yt_ref, acc_ref[...] + part)
        _for_rows(rows, lambda r: scatter(slot_ref)(r).start())

        @pl.when(i == n_blocks - 1)
        def _():
            wait_scatter()
            wait_gather(1 - i % 2)


def _moe(x_tiles, row_tok, row_slot, n_slots, block_e, wg, wu, wd):
    rows = EXPERT_BLOCK
    n_blocks = row_tok.shape[0] // rows
    kf = D_FF // FF_TILE
    assert kf == 2
    idx_spec = lambda fn: pl.BlockSpec((None, 1, rows), lambda i, k, be: (fn(i), 0, 0), memory_space=pltpu.SMEM)
    tok3 = row_tok.reshape(n_blocks, 1, rows)
    slot3 = row_slot.reshape(n_blocks, 1, rows)
    return pl.pallas_call(
        _moe_kernel,
        grid_spec=pltpu.PrefetchScalarGridSpec(
            num_scalar_prefetch=1,
            grid=(n_blocks, kf),
            in_specs=[idx_spec(lambda i: i), idx_spec(lambda i: jnp.minimum(i + 1, n_blocks - 1)),
                      idx_spec(lambda i: i),
                      pl.BlockSpec(memory_space=pl.ANY),
                      pl.BlockSpec((None, D_MODEL, FF_TILE), lambda i, k, be: (be[i], 0, k)),
                      pl.BlockSpec((None, D_MODEL, FF_TILE), lambda i, k, be: (be[i], 0, k)),
                      pl.BlockSpec((None, FF_TILE, D_MODEL), lambda i, k, be: (be[i], k, 0))],
            out_specs=pl.BlockSpec(memory_space=pl.ANY),
            scratch_shapes=[pltpu.VMEM((2, rows * TOKEN_TILE, LANE), F32), pltpu.VMEM((rows, D_MODEL), BF16),
                            pltpu.VMEM((rows, D_MODEL), F32), pltpu.VMEM((rows * TOKEN_TILE, LANE), F32),
                            pltpu.SemaphoreType.DMA((2,)), pltpu.SemaphoreType.DMA(())]),
        out_shape=jax.ShapeDtypeStruct((n_slots * TOKEN_TILE, LANE), F32),
        compiler_params=_params(("arbitrary", "arbitrary")),
        name="moe",
    )(block_e, tok3, tok3, slot3, x_tiles, wg, wu, wd)


def _swiglu_kernel(x_ref, wg_ref, wu_ref, wd_ref, o_ref, xb_ref):
    k = pl.program_id(1)

    @pl.when(k == 0)
    def _():
        xb_ref[...] = x_ref[...].astype(BF16)

    xb = xb_ref[...]
    hidden = (jax.nn.silu(_dot(xb, wg_ref[...])) * _dot(xb, wu_ref[...])).astype(BF16)
    part = _dot(hidden, wd_ref[...])

    @pl.when(k == 0)
    def _():
        o_ref[...] = part

    @pl.when(k != 0)
    def _():
        o_ref[...] += part


def _swiglu(rows, wg, wu, wd, tm):
    r = rows.shape[0]
    row = pl.BlockSpec((tm, D_MODEL), lambda i, k: (i, 0))
    return pl.pallas_call(
        _swiglu_kernel,
        grid=(r // tm, D_FF // FF_TILE),
        in_specs=[row,
                  pl.BlockSpec((D_MODEL, FF_TILE), lambda i, k: (0, k)),
                  pl.BlockSpec((D_MODEL, FF_TILE), lambda i, k: (0, k)),
                  pl.BlockSpec((FF_TILE, D_MODEL), lambda i, k: (k, 0))],
        out_specs=row,
        out_shape=jax.ShapeDtypeStruct(rows.shape, F32),
        scratch_shapes=[pltpu.VMEM((tm, D_MODEL), BF16)],
        compiler_params=_params(("parallel", "arbitrary")),
        name="swiglu",
    )(rows, wg, wu, wd)


def _ple_ln2_kernel(*refs, n_parts, alpha):
    x_ref, p_ref = refs[0], refs[1]
    parts = refs[2:2 + n_parts]
    rest = refs[2 + n_parts:]
    if n_parts > 1:
        route_ref, rest = rest[0], rest[1:]
    wg_ref, bg_ref, wp_ref, lg_ref, lb_ref, o_ref = rest
    x1 = x_ref[...]
    if n_parts > 1:
        route = route_ref[...]
        f = _load_token_tiles(parts[0]) * route[:, TOP_K:TOP_K + 1]
        for k in range(1, n_parts):
            f = f + _load_token_tiles(parts[k]) * route[:, TOP_K + k:TOP_K + k + 1]
    else:
        f = parts[0][...]
    gate = jax.nn.sigmoid(_dot(x1.astype(BF16), wg_ref[...]) + bg_ref[...])
    ple = gate * _dot(p_ref[...].astype(BF16), wp_ref[...])
    o_ref[...] = _layer_norm_rows(alpha * x1 + f + ple, lg_ref[...], lb_ref[...])


def _ple_ln2(x1, p2, y, route, wg, bg, wp, lg, lb, alpha, tm):
    t = x1.shape[0]
    n_parts = 1 if route is None else TOP_K
    steps = t // tm
    row = lambda width: pl.BlockSpec((tm, width), lambda i: (i, 0))
    full = lambda shape: pl.BlockSpec(shape, lambda i: (0,) * len(shape))
    in_specs = [row(D_MODEL), row(PLE_DIM)]
    if n_parts > 1:
        in_specs += [pl.BlockSpec((tm * TOKEN_TILE, LANE), lambda i, k=k: (k * steps + i, 0)) for k in range(n_parts)]
    else:
        in_specs.append(row(D_MODEL))
    args = [x1, p2] + [y] * n_parts
    if n_parts > 1:
        in_specs.append(row(LANE))
        args.append(route)
    in_specs += [full((D_MODEL, D_MODEL)), full((1, D_MODEL)), full((PLE_DIM, D_MODEL)),
                 full((1, D_MODEL)), full((1, D_MODEL))]
    args += [wg.astype(BF16), bg[None], wp.astype(BF16), lg[None], lb[None]]
    return pl.pallas_call(
        functools.partial(_ple_ln2_kernel, n_parts=n_parts, alpha=alpha),
        grid=(steps,), in_specs=in_specs, out_specs=row(D_MODEL),
        out_shape=jax.ShapeDtypeStruct((t, D_MODEL), F32),
        compiler_params=_params(("parallel",)),
        name="ple_ln2",
    )(*args)


def _route_tables(route, t):
    flat_e = route[:, :TOP_K].astype(jnp.int32).reshape(-1)
    tk = flat_e.shape[0]
    onehot = (flat_e[:, None] == jnp.arange(N_EXPERTS, dtype=jnp.int32)[None, :]).astype(jnp.int32)
    before = jnp.cumsum(onehot, axis=0) - onehot
    rank = jnp.sum(before * onehot, axis=1)
    counts = jnp.sum(onehot, axis=0)
    padded = (counts + EXPERT_BLOCK - 1) // EXPERT_BLOCK * EXPERT_BLOCK
    pend = jnp.cumsum(padded)
    dest = (pend - padded)[flat_e] + rank
    n_blocks = -(-tk // EXPERT_BLOCK) + N_EXPERTS
    n_rows = n_blocks * EXPERT_BLOCK
    row_pair = jnp.full((n_rows,), -1, jnp.int32).at[dest].set(jnp.arange(tk, dtype=jnp.int32))
    is_pad = (row_pair < 0).astype(jnp.int32)
    row_tok = jnp.where(row_pair < 0, 0, row_pair // TOP_K)
    pad_slot = tk + jnp.cumsum(is_pad) - is_pad
    row_slot = jnp.where(row_pair < 0, pad_slot, (row_pair % TOP_K) * t + row_pair // TOP_K).astype(jnp.int32)
    block_start = jnp.arange(n_blocks, dtype=jnp.int32) * EXPERT_BLOCK
    block_e = jnp.sum((pend[None, :] <= block_start[:, None]).astype(jnp.int32), axis=1)
    block_e = jnp.minimum(block_e, N_EXPERTS - 1).astype(jnp.int32)
    return row_tok, row_slot, n_rows, block_e


def kernel(x, p, positions, w_in, pool_w, pool_scale, sg_ln_g, sg_ln_b, sg_w, sg_b, cmp_k_pos, cmp_k_w1, cmp_k_w2, cmp_v_pos, cmp_v_w1, cmp_v_w2, w_pool_out, w_sg_out, w_nsa_out, w_out, ln1_g, ln1_b, ffn_w_gate, ffn_w_up, ffn_w_down, moe_router, moe_router_b, moe_w_gate, moe_w_up, moe_w_down, ple_gate_w, ple_gate_b, ple_proj, ln2_g, ln2_b):
    b, seq, d = x.shape
    depth = w_in.shape[0]
    t = b * seq
    alpha = (2 * depth) ** 0.25
    tm = 512
    tabs = _rope_tables(positions)
    x2 = x.reshape(t, d)
    for i in range(depth):
        mix, br, qT, keys, cmp_src, valT, gates = _proj(x2, _prep_w_in(w_in[i]), tabs, b, seq, tm)
        pm, sgm = _mixers(mix, pool_w[i], pool_scale[i], sg_ln_g[i], sg_ln_b[i], sg_w[i], sg_b[i], seq, tm)
        cmp, cmpT = _compress(cmp_src, *_prep_compress(cmp_k_pos[i], cmp_k_w1[i], cmp_k_w2[i],
                                                       cmp_v_pos[i], cmp_v_w1[i], cmp_v_w2[i]))
        on = _nsa(qT, cmp, cmpT, keys, valT, gates, seq)
        j = i // 2
        moe = i % 2 == 1
        merged = _merge(pm, sgm, on, br, x2, w_pool_out[i], w_sg_out[i], w_nsa_out[i], w_out[i],
                        ln1_g[i], ln1_b[i], alpha, tm, router=(moe_router[j], moe_router_b[j]) if moe else None)
        if moe:
            x1, route, x1_tiles = merged
            row_tok, row_slot, n_slots, block_e = _route_tables(route, t)
            y = _moe(x1_tiles, row_tok, row_slot, n_slots, block_e, moe_w_gate[j].astype(BF16),
                     moe_w_up[j].astype(BF16), moe_w_down[j].astype(BF16))
        else:
            x1, route = merged, None
            y = _swiglu(x1, ffn_w_gate[j].astype(BF16), ffn_w_up[j].astype(BF16), ffn_w_down[j].astype(BF16), tm)
        x2 = _ple_ln2(x1, p[i].reshape(t, PLE_DIM), y, route, ple_gate_w[i], ple_gate_b[i], ple_proj[i],
                      ln2_g[i], ln2_b[i], alpha, tm)
    return x2.reshape(b, seq, d)
```

```python
import functools

import jax
import jax.numpy as jnp
import numpy as np
from jax import lax
from jax.experimental import pallas as pl
from jax.experimental.pallas import tpu as pltpu

F32 = jnp.float32
BF16 = jnp.bfloat16

D_MODEL = 1024
POOL_WIDTH = 512
POOL_WINDOWS = (2, 4, 8, 16)
POOL_HALO = 16
SG_WIDTH = 512
SG_HEADS = 4
SG_CHUNK = 128
NSA_Q_HEADS = 8
NSA_KV_HEADS = 2
GROUP = NSA_Q_HEADS // NSA_KV_HEADS
HEAD_DIM = 64
Q_WIDTH = NSA_Q_HEADS * HEAD_DIM
KV_WIDTH = NSA_KV_HEADS * HEAD_DIM
ROT_DIM = HEAD_DIM // 4
ROPE_THETA = 500000.0
CMP_LEN = 32
CMP_STRIDE = 16
CMP_HIDDEN = 256
SLC_LEN = 64
SLC_TOP = 16
WINDOW = 512
Q_BLOCK = 128
NEG_INF = -1e30
D_FF = 2816
N_EXPERTS = 8
TOP_K = 2
EXPERT_BLOCK = 512
PLE_DIM = 256
LN_EPS = 1e-5

LANE = 128
TOKEN_TILE = D_MODEL // LANE
PROJ_BLOCK = 512
NSA_STRIP = 64
SLC_BLOCK = 256
SLC_UNROLL = 8
SLC_AHEAD = 3
CMP_TILE = 128
SEL_GROUP = 8
ONES_PAD = 16
PSUM_PAD = 8
LOG2_E = 1.4426950408889634
FF_TILE = 1408
VMEM_LIMIT = 56 * 1024 * 1024


def _params(semantics, vmem=VMEM_LIMIT):
    return pltpu.CompilerParams(dimension_semantics=semantics, vmem_limit_bytes=vmem)


def _dot(a, b):
    return jnp.dot(a, b, preferred_element_type=F32)


def _dot_nt(a, b):
    return lax.dot_general(a, b, (((1,), (1,)), ((), ())), preferred_element_type=F32)


def _store_token_tiles(ref, v):
    n = v.shape[0]
    for s in range(TOKEN_TILE):
        ref[pl.ds(s, n, stride=TOKEN_TILE), :] = v[:, s * LANE:(s + 1) * LANE]


def _load_token_tiles(ref):
    n = ref.shape[0] // TOKEN_TILE
    return jnp.concatenate([ref[pl.ds(s, n, stride=TOKEN_TILE), :] for s in range(TOKEN_TILE)], axis=1)


def _layer_norm_rows(v, g, b):
    mu = jnp.mean(v, axis=-1, keepdims=True)
    c = v - mu
    var = jnp.mean(c * c, axis=-1, keepdims=True)
    return c * lax.rsqrt(var + LN_EPS) * g + b


def _rope_tables(positions):
    half = ROT_DIM // 2
    inv = ROPE_THETA ** (-jnp.arange(half, dtype=F32) / half)
    ang = positions.astype(F32)[:, :, None] * inv
    cos, sin = jnp.cos(ang), jnp.sin(ang)
    rest = HEAD_DIM - ROT_DIM
    one = jnp.ones(ang.shape[:2] + (rest,), F32)
    zero_r = jnp.zeros(ang.shape[:2] + (rest,), F32)
    zero_h = jnp.zeros_like(sin)
    c = jnp.concatenate([cos, cos, one], -1)
    s1 = jnp.concatenate([zero_h, sin, zero_r], -1)
    s2 = jnp.concatenate([-sin, zero_h, zero_r], -1)
    reps = LANE // HEAD_DIM
    lane_tabs = tuple(jnp.tile(t, (1, 1, reps)).reshape(-1, LANE) for t in (c, s1, s2))
    return lane_tabs + (cos.reshape(-1, half).T, sin.reshape(-1, half).T)


def _proj_kernel(x_ref, wrow_ref, wkey_ref, wq_ref, wval_ref, wgate_ref, c_ref, s1_ref, s2_ref, cos_ref, sin_ref,
                 mix_ref, br_ref, qT_ref, keys_ref, cmp_ref, valT_ref, gateT_ref, *, tiles_per_seq):
    tm = x_ref.shape[0]
    half = ROT_DIM // 2
    xb = x_ref[...].astype(BF16)

    for ref, col0 in ((mix_ref, 0), (br_ref, mix_ref.shape[1])):
        for col in range(0, ref.shape[1], PROJ_BLOCK):
            ref[:, col:col + PROJ_BLOCK] = _dot(xb, wrow_ref[:, col0 + col:col0 + col + PROJ_BLOCK]).astype(ref.dtype)

    def rope(v):
        return (v * c_ref[...] + pltpu.roll(v, half, 1) * s1_ref[...]
                + pltpu.roll(v, LANE - half, 1) * s2_ref[...])

    tpos = (pl.program_id(0) % tiles_per_seq) * tm + lax.broadcasted_iota(jnp.int32, (tm, LANE), 0)
    lane = lax.broadcasted_iota(jnp.int32, (tm, LANE), 1)
    onehot = jnp.where(lane == HEAD_DIM + (tpos // SLC_LEN) % SEL_GROUP, 1.0, 0.0)
    for k in range(4):
        acc = _dot(xb, wkey_ref[:, k * 2 * LANE:(k + 1) * 2 * LANE])
        for h in range(NSA_KV_HEADS):
            v = acc[:, h * LANE:(h + 1) * LANE]
            if k < 3:
                v = rope(v)
            if k == 0:
                cmp_ref[0, h] = v
            elif k == 1:
                keys_ref[0, h] = (v + onehot).astype(keys_ref.dtype)
            elif k == 2:
                keys_ref[1, h] = v.astype(keys_ref.dtype)
            else:
                cmp_ref[1, h] = v

    cos, sin = cos_ref[...], sin_ref[...]
    q_scale = HEAD_DIM ** -0.5 * LOG2_E
    for h in range(NSA_KV_HEADS):
        qt = _dot_nt(wq_ref[h], xb)
        for g in range(GROUP):
            blk = qt[g * HEAD_DIM:(g + 1) * HEAD_DIM]
            x1, x2 = blk[:half], blk[half:ROT_DIM]
            y = jnp.concatenate([x1 * cos - x2 * sin, x2 * cos + x1 * sin, blk[ROT_DIM:]], axis=0) * q_scale
            y = y.astype(qT_ref.dtype)
            for nn in range(tm // Q_BLOCK):
                dst = nn * GROUP * Q_BLOCK + g * Q_BLOCK
                qT_ref[h, :, dst:dst + Q_BLOCK] = y[:, nn * Q_BLOCK:(nn + 1) * Q_BLOCK]

    vt = _dot_nt(wval_ref[...], xb)
    ones_pad = jnp.where(lax.broadcasted_iota(jnp.int32, (ONES_PAD, tm), 0) == 0, 1.0, 0.0).astype(valT_ref.dtype)
    for k in range(2):
        for h in range(NSA_KV_HEADS):
            row = (k * NSA_KV_HEADS + h) * HEAD_DIM
            valT_ref[k, h, :HEAD_DIM, :] = vt[row:row + HEAD_DIM].astype(valT_ref.dtype)
            valT_ref[k, h, HEAD_DIM:, :] = ones_pad

    gt = _dot_nt(wgate_ref[...], xb)
    for h in range(NSA_KV_HEADS):
        gateT_ref[h] = gt[h * 16:(h + 1) * 16].astype(gateT_ref.dtype)


def _proj(x2, weights, tabs, b, seq, tm):
    t = x2.shape[0]
    tps = seq // tm
    wrow, wkey, wq, wval, wgate = weights
    h = NSA_KV_HEADS
    n_mix = POOL_WIDTH + 2 * SG_WIDTH
    row = lambda i: (i, 0)
    const = lambda shape: pl.BlockSpec(shape, lambda i: (0,) * len(shape), pipeline_mode=pl.Buffered(1))
    return pl.pallas_call(
        functools.partial(_proj_kernel, tiles_per_seq=tps),
        grid=(t // tm,),
        in_specs=[pl.BlockSpec((tm, D_MODEL), row), const(wrow.shape), const(wkey.shape), const(wq.shape),
                  const(wval.shape), const(wgate.shape),
                  pl.BlockSpec((tm, LANE), row), pl.BlockSpec((tm, LANE), row), pl.BlockSpec((tm, LANE), row),
                  pl.BlockSpec((ROT_DIM // 2, tm), lambda i: (0, i)), pl.BlockSpec((ROT_DIM // 2, tm), lambda i: (0, i))],
        out_specs=[pl.BlockSpec((tm, n_mix), row), pl.BlockSpec((tm, 3 * D_MODEL), row),
                   pl.BlockSpec((None, h, HEAD_DIM, tm * GROUP), lambda i: (i // tps, 0, 0, i % tps)),
                   pl.BlockSpec((2, None, h, tm, LANE), lambda i: (0, i // tps, 0, i % tps, 0)),
                   pl.BlockSpec((2, None, h, tm, LANE), lambda i: (0, i // tps, 0, i % tps, 0)),
                   pl.BlockSpec((2, None, h, HEAD_DIM + ONES_PAD, tm), lambda i: (0, i // tps, 0, 0, i % tps)),
                   pl.BlockSpec((None, h, 16, tm), lambda i: (i // tps, 0, 0, i % tps))],
        out_shape=[jax.ShapeDtypeStruct((t, n_mix), BF16), jax.ShapeDtypeStruct((t, 3 * D_MODEL), BF16),
                   jax.ShapeDtypeStruct((b, h, HEAD_DIM, seq * GROUP), BF16),
                   jax.ShapeDtypeStruct((2, b, h, seq, LANE), BF16),
                   jax.ShapeDtypeStruct((2, b, h, seq, LANE), F32),
                   jax.ShapeDtypeStruct((2, b, h, HEAD_DIM + ONES_PAD, seq), BF16),
                   jax.ShapeDtypeStruct((b, h, 16, seq), BF16)],
        compiler_params=_params(("parallel",)),
        name="proj",
    )(x2, wrow, wkey, wq, wval, wgate, *tabs)


def _prep_w_in(w):
    d = w.shape[0]
    h = NSA_KV_HEADS
    n_mix = POOL_WIDTH + 2 * SG_WIDTH
    kv0 = n_mix + Q_WIDTH
    n_gate = NSA_Q_HEADS * 3
    gate0 = kv0 + 6 * KV_WIDTH
    kv = lambda k: w[:, kv0 + k * KV_WIDTH:kv0 + (k + 1) * KV_WIDTH]
    wrow = jnp.concatenate([w[:, :n_mix], w[:, gate0 + n_gate:]], axis=1)
    pad_heads = lambda m: jnp.pad(m.reshape(d, h, HEAD_DIM), ((0, 0), (0, 0), (0, LANE - HEAD_DIM))).reshape(d, h * LANE)
    wkey = jnp.concatenate([pad_heads(kv(0)), pad_heads(kv(2)), pad_heads(kv(4)), pad_heads(kv(1))], axis=1)
    wq = w[:, n_mix:kv0].T.reshape(h, GROUP * HEAD_DIM, d)
    wval = jnp.concatenate([kv(3), kv(5)], axis=1).T
    wgate = jnp.pad(w[:, gate0:gate0 + n_gate].T.reshape(h, GROUP * 3, d), ((0, 0), (0, 16 - GROUP * 3), (0, 0)))
    return tuple(m.astype(BF16) for m in (wrow, wkey, wq, wval, wgate.reshape(h * 16, d)))


def _mixers_kernel(a_ref, ap_ref, u_ref, v_ref, pw_ref, ps_ref, lg_ref, lb_ref, sw_ref, sb_ref,
                   pm_ref, sg_ref, *, tiles_per_seq):
    i = pl.program_id(0)
    tm = a_ref.shape[0]
    it = i % tiles_per_seq
    gw = POOL_WIDTH // len(POOL_WINDOWS)

    a = a_ref[...].astype(F32)
    prev = jnp.where(it == 0, 0.0, ap_ref[...].astype(F32))
    ext = jnp.concatenate([prev, a], axis=0)
    tpos = (it * tm + lax.broadcasted_iota(jnp.int32, (tm, 1), 0) + 1).astype(F32)
    for g, w in enumerate(POOL_WINDOWS):
        s = ext[:, g * gw:(g + 1) * gw]
        k = 1
        while k < w:
            s = s + pltpu.roll(s, k, 0)
            k *= 2
        d = s[POOL_HALO:] / jnp.minimum(tpos, float(w)) - a[:, g * gw:(g + 1) * gw]
        y = _dot(d.astype(BF16), pw_ref[g]) * ps_ref[:, g * gw:(g + 1) * gw]
        pm_ref[:, g * gw:(g + 1) * gw] = y.astype(pm_ref.dtype)

    u = jax.nn.gelu(u_ref[...].astype(F32))
    v = _layer_norm_rows(jax.nn.gelu(v_ref[...].astype(F32)), lg_ref[...], lb_ref[...]).astype(BF16)
    hc = SG_WIDTH // SG_HEADS
    tri = (lax.broadcasted_iota(jnp.int32, (SG_CHUNK, SG_CHUNK), 0)
           >= lax.broadcasted_iota(jnp.int32, (SG_CHUNK, SG_CHUNK), 1))
    for g in range(SG_HEADS):
        ws = jnp.where(tri, sw_ref[g], 0.0).astype(BF16)
        bias = sb_ref[:, g:g + 1]
        for c in range(tm // SG_CHUNK):
            rows = slice(c * SG_CHUNK, (c + 1) * SG_CHUNK)
            cols = slice(g * hc, (g + 1) * hc)
            mixed = _dot(ws, v[rows, cols]) + bias
            sg_ref[rows, cols] = (u[rows, cols] * mixed).astype(sg_ref.dtype)


def _mixers(mix, pool_w, pool_scale, ln_g, ln_b, sg_w, sg_b, seq, tm):
    t = mix.shape[0]
    halo_blocks = tm // POOL_HALO
    full = lambda shape: pl.BlockSpec(shape, lambda i: (0,) * len(shape))
    return pl.pallas_call(
        functools.partial(_mixers_kernel, tiles_per_seq=seq // tm),
        grid=(t // tm,),
        in_specs=[pl.BlockSpec((tm, POOL_WIDTH), lambda i: (i, 0)),
                  pl.BlockSpec((POOL_HALO, POOL_WIDTH), lambda i: (jnp.maximum(i * halo_blocks - 1, 0), 0)),
                  pl.BlockSpec((tm, SG_WIDTH), lambda i: (i, 1)),
                  pl.BlockSpec((tm, SG_WIDTH), lambda i: (i, 2)),
                  full(pool_w.shape), full((1, POOL_WIDTH)), full((1, SG_WIDTH)), full((1, SG_WIDTH)),
                  full(sg_w.shape), full((SG_CHUNK, SG_HEADS))],
        out_specs=[pl.BlockSpec((tm, POOL_WIDTH), lambda i: (i, 0)),
                   pl.BlockSpec((tm, SG_WIDTH), lambda i: (i, 0))],
        out_shape=[jax.ShapeDtypeStruct((t, POOL_WIDTH), BF16), jax.ShapeDtypeStruct((t, SG_WIDTH), BF16)],
        compiler_params=_params(("parallel",)),
        name="mixers",
    )(mix, mix, mix, mix, pool_w.astype(BF16), pool_scale[None], ln_g[None], ln_b[None], sg_w, sg_b.T)


def _compress_kernel(src_ref, pos_ref, w1_ref, w2_ref, w2T_ref, o_ref, oT_ref):
    nc = o_ref.shape[0]
    first = jnp.zeros((nc, CMP_HIDDEN), F32)
    second = jnp.zeros((nc, CMP_HIDDEN), F32)
    for p in range(CMP_STRIDE):
        rows = src_ref[pl.ds(p, nc, stride=CMP_STRIDE), :]
        first = first + _dot((rows + pos_ref[p:p + 1, :]).astype(BF16), w1_ref[p])
        q = CMP_STRIDE + p
        second = second + _dot((rows + pos_ref[q:q + 1, :]).astype(BF16), w1_ref[q])
    pre = first + pltpu.roll(second, nc - 1, 0)
    hidden = jax.nn.gelu(pre).astype(BF16)
    o_ref[...] = _dot(hidden, w2_ref[...]).astype(o_ref.dtype)
    oT_ref[...] = _dot_nt(w2T_ref[...], hidden).astype(oT_ref.dtype)


def _compress(src, pos, w1, w2, w2T):
    _, b, h, seq, _ = src.shape
    nc = seq // CMP_STRIDE
    sel = lambda n: pl.BlockSpec((None,) + n, lambda s, bi, hi: (s,) + (0,) * len(n))
    return pl.pallas_call(
        _compress_kernel,
        grid=(2, b, h),
        in_specs=[pl.BlockSpec((None, None, None, seq, LANE), lambda s, bi, hi: (s, bi, hi, 0, 0)),
                  sel(pos.shape[1:]), sel(w1.shape[1:]), sel(w2.shape[1:]), sel(w2T.shape[1:])],
        out_specs=[pl.BlockSpec((None, None, None, nc, HEAD_DIM), lambda s, bi, hi: (s, bi, hi, 0, 0)),
                   pl.BlockSpec((None, None, None, HEAD_DIM, nc), lambda s, bi, hi: (s, bi, hi, 0, 0))],
        out_shape=[jax.ShapeDtypeStruct((2, b, h, nc, HEAD_DIM), BF16),
                   jax.ShapeDtypeStruct((2, b, h, HEAD_DIM, nc), BF16)],
        compiler_params=_params(("parallel", "parallel", "parallel")),
        name="compress",
    )(src, pos, w1, w2, w2T)


def _prep_compress(k_pos, k_w1, k_w2, v_pos, v_w1, v_w2):
    pad = LANE - HEAD_DIM
    pos = jnp.pad(jnp.stack([k_pos, v_pos]), ((0, 0), (0, 0), (0, pad)))
    w1 = jnp.stack([k_w1, v_w1]).reshape(2, CMP_LEN, HEAD_DIM, CMP_HIDDEN)
    w1 = jnp.pad(w1, ((0, 0), (0, 0), (0, pad), (0, 0))).astype(BF16)
    w2 = jnp.stack([k_w2, v_w2]).astype(BF16)
    return pos, w1, w2, w2.transpose(0, 2, 1)


def _score_strips(k_ref, k0, n_strips, q_aug, s_ref, mask):
    mx = None
    for r in range(n_strips):
        start = pl.multiple_of(k0 + r * NSA_STRIP, NSA_STRIP)
        s = _dot(k_ref[pl.ds(start, NSA_STRIP), :], q_aug).astype(s_ref.dtype)
        if mask is not None:
            s = mask(r, start, s)
        s_ref[r * NSA_STRIP:(r + 1) * NSA_STRIP, :] = s
        mx = s if mx is None else jnp.maximum(mx, s)
    return jnp.max(mx.astype(F32), axis=0, keepdims=True)


def _prob_strips(n_strips, m, s_ref, p_ref):
    m = m.astype(s_ref.dtype)
    for r in range(n_strips):
        rows = slice(r * NSA_STRIP, (r + 1) * NSA_STRIP)
        p_ref[rows, :] = jnp.exp2(s_ref[rows, :] - m).astype(BF16)


def _nsa_kernel(qT_ref, kc_ref, vcT_ref, ks_ref, vsT_ref, kw_ref, vwT_ref, g_ref, bound_ref, o_ref,
                s_ref, p_ref, pc_ref, s2_ref, p2_ref, psum_ref, selneg_ref, *, n_top):
    n = pl.program_id(2)
    qs = n * Q_BLOCK
    width = GROUP * Q_BLOCK
    qT = qT_ref[...]
    t_row = qs + lax.broadcasted_iota(jnp.int32, (1, width), 1) % Q_BLOCK
    nc = kc_ref.shape[0]
    ns = selneg_ref.shape[0]
    strip_iota = lax.broadcasted_iota(jnp.int32, (NSA_STRIP, 1), 0)

    per = SLC_LEN // CMP_STRIDE
    psum_ref[:PSUM_PAD, :] = jnp.zeros((PSUM_PAD, Q_BLOCK), F32)

    def compressed_and_selection(n_tiles):
        keys = n_tiles * CMP_TILE
        rows = keys // per
        strip_rows = [slice(r * NSA_STRIP, (r + 1) * NSA_STRIP) for r in range(keys // NSA_STRIP)]

        mx = None
        for sl in strip_rows:
            s = _dot(kc_ref[sl, :], qT)
            c_end = (sl.start + strip_iota) * CMP_STRIDE + (CMP_LEN - 1)
            s = jnp.where(c_end <= t_row, s, NEG_INF)
            s_ref[sl, :] = s
            mx = s if mx is None else jnp.maximum(mx, s)
        m_cmp = jnp.maximum(jnp.max(mx, axis=0, keepdims=True), 0.1 * NEG_INF)
        lsum = jnp.zeros((NSA_STRIP, width), F32)
        for sl in strip_rows:
            e = jnp.exp2(s_ref[sl, :] - m_cmp)
            s_ref[sl, :] = e
            pc_ref[sl, :] = e.astype(BF16)
            lsum = lsum + e
        l = jnp.sum(lsum, axis=0, keepdims=True)
        inv = jnp.where(l > 0.0, 1.0 / l, 0.0)
        out = _dot(vcT_ref[:, :keys], pc_ref[:keys, :]) * inv
        for sl in strip_rows:
            p = s_ref[sl, :] * inv
            psum = p[:, :Q_BLOCK]
            for g in range(1, GROUP):
                psum = psum + p[:, g * Q_BLOCK:(g + 1) * Q_BLOCK]
            psum_ref[PSUM_PAD + sl.start:PSUM_PAD + sl.stop, :] = psum

        part = [psum_ref[pl.ds(PSUM_PAD + k, rows, stride=per), :] for k in range(per)]
        before = psum_ref[pl.ds(PSUM_PAD - 1, rows, stride=per), :]
        imp = 2.0 * (part[0] + part[1] + part[2]) + part[3] + before

        blk = lax.broadcasted_iota(jnp.int32, (rows, Q_BLOCK), 0)
        cur = (qs + lax.broadcasted_iota(jnp.int32, (rows, Q_BLOCK), 1)) // SLC_LEN
        taken = -2.0
        score = jnp.where(blk <= cur, imp, -1.0)
        score = jnp.where(blk == 0, taken, jnp.where(blk == cur, taken, jnp.where(blk == cur - 1, taken, score)))
        blk_f = blk.astype(F32)
        for _ in range(n_top - 3):
            top = jnp.max(score, axis=0, keepdims=True)
            cand = jnp.where(score == top, blk_f, float(rows))
            first = jnp.min(cand, axis=0, keepdims=True)
            score = jnp.where(cand == first, taken, score)
        selneg_ref[:rows, :] = jnp.where(score == taken, 0.0, NEG_INF)
        if rows < ns:
            selneg_ref[rows:, :] = jnp.full((ns - rows, Q_BLOCK), NEG_INF, F32)
        return out

    reach = CMP_TILE * CMP_STRIDE // Q_BLOCK
    o_cmp = lax.switch(n // reach, [functools.partial(compressed_and_selection, t + 1) for t in range(nc // CMP_TILE)])

    strips = SLC_BLOCK // NSA_STRIP
    sel_rows = ks_ref.shape[1] - HEAD_DIM - SEL_GROUP
    last_block = ks_ref.shape[0] // SLC_BLOCK - 1
    n_diag = n // (SLC_BLOCK // Q_BLOCK)

    def past_only(r, start, s):
        return jnp.where(start + strip_iota <= t_row, s, NEG_INF)

    def slc_scores(kb, slot):
        per_block = SLC_BLOCK // Q_BLOCK
        variant = jnp.where(kb < n_diag, 0, jnp.where(kb > n_diag, 1 + per_block, 1 + n % per_block))

        def limit(r, start, s):
            return jnp.minimum(s, bound_ref[variant, r * NSA_STRIP:(r + 1) * NSA_STRIP, :])

        kb = jnp.minimum(kb, last_block)
        k0 = pl.multiple_of(kb * SLC_BLOCK, SLC_BLOCK)
        g0 = pl.multiple_of((kb * strips // SEL_GROUP) * SEL_GROUP, SEL_GROUP)
        bias = jnp.concatenate([selneg_ref[pl.ds(g0, SEL_GROUP), :]] * GROUP, axis=1)
        bias = jnp.concatenate([bias, jnp.zeros((sel_rows, width), F32)], axis=0).astype(BF16)
        q_aug = jnp.concatenate([qT, bias], axis=0)
        return _score_strips(ks_ref, k0, strips, q_aug, s2_ref.at[slot], limit)

    def slc_consume(kb, slot, m, acc, mx):
        k0 = pl.multiple_of(kb * SLC_BLOCK, SLC_BLOCK)
        m_new = jnp.maximum(m, mx)
        _prob_strips(strips, m_new, s2_ref.at[slot], p2_ref.at[slot])
        pv = _dot(vsT_ref[:, pl.ds(k0, SLC_BLOCK)], p2_ref[slot])
        return m_new, jnp.exp2(m - m_new) * acc + pv

    def slc_group(j, carry):
        m, acc, pending = carry[0], carry[1], list(carry[2:])
        for u in range(SLC_UNROLL):
            kb = SLC_UNROLL * j + u
            pending.append(slc_scores(kb + SLC_AHEAD, (u + SLC_AHEAD) % SLC_UNROLL))
            m, acc = slc_consume(kb, u, m, acc, pending.pop(0))
        return (m, acc, *pending)

    init = (jnp.full((1, width), NEG_INF, F32), jnp.zeros((vsT_ref.shape[0], width), F32),
            *[slc_scores(a, a) for a in range(SLC_AHEAD)])

    span = WINDOW + Q_BLOCK
    w_strips = span // NSA_STRIP
    edge = Q_BLOCK // NSA_STRIP
    w0 = pl.multiple_of(jnp.maximum(qs - WINDOW, 0), Q_BLOCK)

    def band(r, start, s):
        if r < edge:
            diff = t_row - (start + strip_iota)
            return jnp.where(diff >= 0, jnp.where(diff < WINDOW, s, NEG_INF), NEG_INF)
        return past_only(r, start, s)

    q_win = jnp.concatenate([qT, jnp.zeros((kw_ref.shape[1] - HEAD_DIM, width), BF16)], axis=0)
    m_win = _score_strips(kw_ref, w0, w_strips, q_win, s_ref, band)
    _prob_strips(w_strips, m_win, s_ref, p_ref)
    pv = _dot(vwT_ref[:, pl.ds(w0, span)], p_ref[:span, :])
    o_win = pv[:HEAD_DIM] / pv[HEAD_DIM:HEAD_DIM + 1]

    gates = jax.nn.sigmoid(g_ref[...].astype(F32))

    def gate(c):
        return jnp.concatenate([gates[g * 3 + c:g * 3 + c + 1, :] for g in range(GROUP)], axis=1)

    o_rest = gate(0) * o_cmp + gate(2) * o_win
    gate_slc = gate(1)

    acc = lax.fori_loop(0, (n_diag + SLC_UNROLL) // SLC_UNROLL, slc_group, init)[1]
    o = o_rest + gate_slc * (acc[:HEAD_DIM] / acc[HEAD_DIM:HEAD_DIM + 1])
    for g in range(GROUP):
        o_ref[:, g * HEAD_DIM:(g + 1) * HEAD_DIM] = o[:, g * Q_BLOCK:(g + 1) * Q_BLOCK].T.astype(o_ref.dtype)


def _causal_bounds():
    big = np.float32(3.0e38)
    key = np.arange(SLC_BLOCK)[:, None]
    query = np.tile(np.arange(Q_BLOCK), GROUP)[None, :]
    tables = [np.full((SLC_BLOCK, GROUP * Q_BLOCK), big, np.float32)]
    for v in range(SLC_BLOCK // Q_BLOCK):
        tables.append(np.where(key <= query + v * Q_BLOCK, big, np.float32(NEG_INF)).astype(np.float32))
    tables.append(np.full((SLC_BLOCK, GROUP * Q_BLOCK), NEG_INF, np.float32))
    return jnp.asarray(np.stack(tables))


def _nsa(qT, cmp, cmpT, keys, valT, gates, seq):
    b, h = qT.shape[:2]
    nq = seq // Q_BLOCK
    nc = cmp.shape[3]
    ns = seq // SLC_LEN
    width = GROUP * Q_BLOCK
    assert nc == ns * (SLC_LEN // CMP_STRIDE) and nc % CMP_TILE == 0 and seq % (SLC_BLOCK * SLC_UNROLL) == 0
    assert seq >= WINDOW + Q_BLOCK and ns >= SLC_TOP
    bounds = _causal_bounds()
    per_head = lambda arr, k: pl.BlockSpec((None, None, None) + arr.shape[3:], lambda bi, hi, n, k=k: (k, bi, hi, 0, 0))
    return pl.pallas_call(
        functools.partial(_nsa_kernel, n_top=SLC_TOP),
        grid=(b, h, nq),
        in_specs=[pl.BlockSpec((None, None, HEAD_DIM, width), lambda bi, hi, n: (bi, hi, 0, n)),
                  per_head(cmp, 0), per_head(cmpT, 1),
                  per_head(keys, 0), per_head(valT, 0), per_head(keys, 1), per_head(valT, 1),
                  pl.BlockSpec((None, None, 16, Q_BLOCK), lambda bi, hi, n: (bi, hi, 0, n)),
                  pl.BlockSpec(bounds.shape, lambda bi, hi, n: (0, 0, 0), pipeline_mode=pl.Buffered(1))],
        out_specs=pl.BlockSpec((Q_BLOCK, GROUP * HEAD_DIM), lambda bi, hi, n: (bi * nq + n, hi)),
        out_shape=jax.ShapeDtypeStruct((b * seq, Q_WIDTH), BF16),
        scratch_shapes=[pltpu.VMEM((max(nc, WINDOW + Q_BLOCK), width), F32),
                        pltpu.VMEM((WINDOW + Q_BLOCK, width), BF16),
                        pltpu.VMEM((nc, width), BF16),
                        pltpu.VMEM((SLC_UNROLL, SLC_BLOCK, width), F32),
                        pltpu.VMEM((SLC_UNROLL, SLC_BLOCK, width), BF16),
                        pltpu.VMEM((PSUM_PAD + nc, Q_BLOCK), F32),
                        pltpu.VMEM((ns, Q_BLOCK), F32)],
        compiler_params=_params(("parallel", "parallel", "arbitrary")),
        name="nsa",
    )(qT, cmp, cmpT, keys, valT, keys, valT, gates, bounds)


def _merge_kernel(*refs, with_router, alpha):
    (pm_ref, sg_ref, on_ref, gp_ref, gs_ref, gn_ref, x_ref, wp_ref, ws_ref, wn_ref, wo_ref, lg_ref, lb_ref) = refs[:13]
    if with_router:
        wr_ref, br_ref, x1_ref, route_ref, x1t_ref = refs[13:]
    else:
        (x1_ref,) = refs[13:]
    sig = lambda r: jax.nn.sigmoid(r[...].astype(F32))
    y = (sig(gp_ref) * _dot(pm_ref[...], wp_ref[...]) + sig(gs_ref) * _dot(sg_ref[...], ws_ref[...])
         + sig(gn_ref) * _dot(on_ref[...], wn_ref[...]))
    hmix = _dot(y.astype(BF16), wo_ref[...])
    x1 = _layer_norm_rows(alpha * x_ref[...] + hmix, lg_ref[...], lb_ref[...])
    x1_ref[...] = x1
    if with_router:
        _store_token_tiles(x1t_ref, x1)
        wr = wr_ref[...]
        w_hi = wr.astype(BF16)
        w_lo = (wr - w_hi.astype(F32)).astype(BF16)
        x_hi = x1.astype(BF16)
        x_lo = (x1 - x_hi.astype(F32)).astype(BF16)
        logits = _dot(x_hi, w_hi) + (_dot(x_lo, w_hi) + _dot(x_hi, w_lo)) + br_ref[...]
        lane = lax.broadcasted_iota(jnp.int32, logits.shape, 1)
        lane_f = lane.astype(F32)
        logits = jnp.where(lane < N_EXPERTS, logits, -jnp.inf)
        m1 = jnp.max(logits, axis=-1, keepdims=True)
        i1 = jnp.min(jnp.where(logits == m1, lane_f, float(LANE)), axis=-1, keepdims=True)
        rest = jnp.where(lane_f == i1, -jnp.inf, logits)
        m2 = jnp.max(rest, axis=-1, keepdims=True)
        i2 = jnp.min(jnp.where(rest == m2, lane_f, float(LANE)), axis=-1, keepdims=True)
        e2 = jnp.exp(m2 - m1)
        w1 = 1.0 / (1.0 + e2)
        route_ref[...] = jnp.where(lane == 0, i1, jnp.where(lane == 1, i2, jnp.where(lane == 2, w1,
                                   jnp.where(lane == 3, e2 * w1, 0.0))))


def _merge(pm, sgm, on, br, x2, wp, ws, wn, wo, lg, lb, alpha, tm, router=None):
    t = x2.shape[0]
    row = lambda width, cb=0: pl.BlockSpec((tm, width), lambda i, cb=cb: (i, cb))
    full = lambda shape: pl.BlockSpec(shape, lambda i: (0,) * len(shape))
    in_specs = [row(POOL_WIDTH), row(SG_WIDTH), row(Q_WIDTH),
                row(D_MODEL, 0), row(D_MODEL, 1), row(D_MODEL, 2),
                row(D_MODEL), full(wp.shape), full(ws.shape), full(wn.shape), full(wo.shape),
                full((1, D_MODEL)), full((1, D_MODEL))]
    args = [pm, sgm, on, br, br, br, x2, wp.astype(BF16), ws.astype(BF16), wn.astype(BF16), wo.astype(BF16),
            lg[None], lb[None]]
    out_specs = [row(D_MODEL)]
    out_shape = [jax.ShapeDtypeStruct((t, D_MODEL), F32)]
    if router is not None:
        w_router, b_router = router
        pad = LANE - N_EXPERTS
        in_specs += [full((D_MODEL, LANE)), full((1, LANE))]
        args += [jnp.pad(w_router, ((0, 0), (0, pad))), jnp.pad(b_router, (0, pad))[None]]
        out_specs += [row(LANE), pl.BlockSpec((tm * TOKEN_TILE, LANE), lambda i: (i, 0))]
        out_shape += [jax.ShapeDtypeStruct((t, LANE), F32), jax.ShapeDtypeStruct((t * TOKEN_TILE, LANE), F32)]
    out = pl.pallas_call(
        functools.partial(_merge_kernel, with_router=router is not None, alpha=alpha),
        grid=(t // tm,), in_specs=in_specs, out_specs=out_specs, out_shape=out_shape,
        compiler_params=_params(("parallel",)),
        name="merge",
    )(*args)
    return out if router is not None else out[0]


def _tile_rows(i):
    start = i * TOKEN_TILE if isinstance(i, int) else pl.multiple_of(i * TOKEN_TILE, TOKEN_TILE)
    return pl.ds(start, TOKEN_TILE)


def _for_rows(n, fn):
    def body(r, c):
        fn(r)
        return c
    lax.fori_loop(0, n, body, 0, unroll=8)


def _moe_kernel(be_ref, tok_ref, tok_next_ref, slot_ref, x_hbm, wg_ref, wu_ref, wd_ref, y_hbm,
                xg_ref, xb_ref, acc_ref, yt_ref, gather_sem, scatter_sem):
    i = pl.program_id(0)
    k = pl.program_id(1)
    n_blocks = pl.num_programs(0)
    rows = xb_ref.shape[0]

    def gather(idx_ref, slot):
        return lambda r: pltpu.make_async_copy(x_hbm.at[_tile_rows(idx_ref[0, r])], xg_ref.at[slot, _tile_rows(r)],
                                               gather_sem.at[slot])

    def scatter(idx_ref):
        return lambda r: pltpu.make_async_copy(yt_ref.at[_tile_rows(r)], y_hbm.at[_tile_rows(idx_ref[0, r])],
                                               scatter_sem)

    def wait_gather(slot):
        pltpu.make_async_copy(x_hbm.at[pl.ds(0, rows * TOKEN_TILE)], xg_ref.at[slot], gather_sem.at[slot]).wait()

    @pl.when(k == 0)
    def _():
        slot = i % 2

        @pl.when(i == 0)
        def _():
            _for_rows(rows, lambda r: gather(tok_ref, 0)(r).start())

        wait_gather(slot)
        for s in range(TOKEN_TILE):
            xb_ref[:, s * LANE:(s + 1) * LANE] = xg_ref[slot, pl.ds(s, rows, stride=TOKEN_TILE), :].astype(BF16)

    share = rows // (D_FF // FF_TILE)
    for r in range(share):
        gather(tok_next_ref, 1 - i % 2)(k * share + r).start()

    xb = xb_ref[...]
    hidden = (jax.nn.silu(_dot(xb, wg_ref[...])) * _dot(xb, wu_ref[...])).astype(BF16)
    part = _dot(hidden, wd_ref[...])

    @pl.when(k == 0)
    def _():
        acc_ref[...] = part

    @pl.when(k == pl.num_programs(1) - 1)
    def _():
        def wait_scatter():
            pltpu.make_async_copy(yt_ref, y_hbm.at[pl.ds(0, rows * TOKEN_TILE)], scatter_sem).wait()

        pl.when(i > 0)(wait_scatter)
        _store_token_tiles(yt_ref, acc_ref[...] + part)
        _for_rows(rows, lambda r: scatter(slot_ref)(r).start())

        @pl.when(i == n_blocks - 1)
        def _():
            wait_scatter()
            wait_gather(1 - i % 2)


def _moe(x_tiles, row_tok, row_slot, n_slots, block_e, wg, wu, wd):
    rows = EXPERT_BLOCK
    n_blocks = row_tok.shape[0] // rows
    kf = D_FF // FF_TILE
    assert kf == 2
    idx_spec = lambda fn: pl.BlockSpec((None, 1, rows), lambda i, k, be: (fn(i), 0, 0), memory_space=pltpu.SMEM)
    tok3 = row_tok.reshape(n_blocks, 1, rows)
    slot3 = row_slot.reshape(n_blocks, 1, rows)
    return pl.pallas_call(
        _moe_kernel,
        grid_spec=pltpu.PrefetchScalarGridSpec(
            num_scalar_prefetch=1,
            grid=(n_blocks, kf),
            in_specs=[idx_spec(lambda i: i), idx_spec(lambda i: jnp.minimum(i + 1, n_blocks - 1)),
                      idx_spec(lambda i: i),
                      pl.BlockSpec(memory_space=pl.ANY),
                      pl.BlockSpec((None, D_MODEL, FF_TILE), lambda i, k, be: (be[i], 0, k)),
                      pl.BlockSpec((None, D_MODEL, FF_TILE), lambda i, k, be: (be[i], 0, k)),
                      pl.BlockSpec((None, FF_TILE, D_MODEL), lambda i, k, be: (be[i], k, 0))],
            out_specs=pl.BlockSpec(memory_space=pl.ANY),
            scratch_shapes=[pltpu.VMEM((2, rows * TOKEN_TILE, LANE), F32), pltpu.VMEM((rows, D_MODEL), BF16),
                            pltpu.VMEM((rows, D_MODEL), F32), pltpu.VMEM((rows * TOKEN_TILE, LANE), F32),
                            pltpu.SemaphoreType.DMA((2,)), pltpu.SemaphoreType.DMA(())]),
        out_shape=jax.ShapeDtypeStruct((n_slots * TOKEN_TILE, LANE), F32),
        compiler_params=_params(("arbitrary", "arbitrary")),
        name="moe",
    )(block_e, tok3, tok3, slot3, x_tiles, wg, wu, wd)


def _swiglu_kernel(x_ref, wg_ref, wu_ref, wd_ref, o_ref, xb_ref):
    k = pl.program_id(1)

    @pl.when(k == 0)
    def _():
        xb_ref[...] = x_ref[...].astype(BF16)

    xb = xb_ref[...]
    hidden = (jax.nn.silu(_dot(xb, wg_ref[...])) * _dot(xb, wu_ref[...])).astype(BF16)
    part = _dot(hidden, wd_ref[...])

    @pl.when(k == 0)
    def _():
        o_ref[...] = part

    @pl.when(k != 0)
    def _():
        o_ref[...] += part


def _swiglu(rows, wg, wu, wd, tm):
    r = rows.shape[0]
    row = pl.BlockSpec((tm, D_MODEL), lambda i, k: (i, 0))
    return pl.pallas_call(
        _swiglu_kernel,
        grid=(r // tm, D_FF // FF_TILE),
        in_specs=[row,
                  pl.BlockSpec((D_MODEL, FF_TILE), lambda i, k: (0, k)),
                  pl.BlockSpec((D_MODEL, FF_TILE), lambda i, k: (0, k)),
                  pl.BlockSpec((FF_TILE, D_MODEL), lambda i, k: (k, 0))],
        out_specs=row,
        out_shape=jax.ShapeDtypeStruct(rows.shape, F32),
        scratch_shapes=[pltpu.VMEM((tm, D_MODEL), BF16)],
        compiler_params=_params(("parallel", "arbitrary")),
        name="swiglu",
    )(rows, wg, wu, wd)


def _ple_ln2_kernel(*refs, n_parts, alpha):
    x_ref, p_ref = refs[0], refs[1]
    parts = refs[2:2 + n_parts]
    rest = refs[2 + n_parts:]
    if n_parts > 1:
        route_ref, rest = rest[0], rest[1:]
    wg_ref, bg_ref, wp_ref, lg_ref, lb_ref, o_ref = rest
    x1 = x_ref[...]
    if n_parts > 1:
        route = route_ref[...]
        f = _load_token_tiles(parts[0]) * route[:, TOP_K:TOP_K + 1]
        for k in range(1, n_parts):
            f = f + _load_token_tiles(parts[k]) * route[:, TOP_K + k:TOP_K + k + 1]
    else:
        f = parts[0][...]
    gate = jax.nn.sigmoid(_dot(x1.astype(BF16), wg_ref[...]) + bg_ref[...])
    ple = gate * _dot(p_ref[...].astype(BF16), wp_ref[...])
    o_ref[...] = _layer_norm_rows(alpha * x1 + f + ple, lg_ref[...], lb_ref[...])


def _ple_ln2(x1, p2, y, route, wg, bg, wp, lg, lb, alpha, tm):
    t = x1.shape[0]
    n_parts = 1 if route is None else TOP_K
    steps = t // tm
    row = lambda width: pl.BlockSpec((tm, width), lambda i: (i, 0))
    full = lambda shape: pl.BlockSpec(shape, lambda i: (0,) * len(shape))
    in_specs = [row(D_MODEL), row(PLE_DIM)]
    if n_parts > 1:
        in_specs += [pl.BlockSpec((tm * TOKEN_TILE, LANE), lambda i, k=k: (k * steps + i, 0)) for k in range(n_parts)]
    else:
        in_specs.append(row(D_MODEL))
    args = [x1, p2] + [y] * n_parts
    if n_parts > 1:
        in_specs.append(row(LANE))
        args.append(route)
    in_specs += [full((D_MODEL, D_MODEL)), full((1, D_MODEL)), full((PLE_DIM, D_MODEL)),
                 full((1, D_MODEL)), full((1, D_MODEL))]
    args += [wg.astype(BF16), bg[None], wp.astype(BF16), lg[None], lb[None]]
    return pl.pallas_call(
        functools.partial(_ple_ln2_kernel, n_parts=n_parts, alpha=alpha),
        grid=(steps,), in_specs=in_specs, out_specs=row(D_MODEL),
        out_shape=jax.ShapeDtypeStruct((t, D_MODEL), F32),
        compiler_params=_params(("parallel",)),
        name="ple_ln2",
    )(*args)


def _route_tables(route, t):
    flat_e = route[:, :TOP_K].astype(jnp.int32).reshape(-1)
    tk = flat_e.shape[0]
    onehot = (flat_e[:, None] == jnp.arange(N_EXPERTS, dtype=jnp.int32)[None, :]).astype(jnp.int32)
    before = jnp.cumsum(onehot, axis=0) - onehot
    rank = jnp.sum(before * onehot, axis=1)
    counts = jnp.sum(onehot, axis=0)
    padded = (counts + EXPERT_BLOCK - 1) // EXPERT_BLOCK * EXPERT_BLOCK
    pend = jnp.cumsum(padded)
    dest = (pend - padded)[flat_e] + rank
    n_blocks = -(-tk // EXPERT_BLOCK) + N_EXPERTS
    n_rows = n_blocks * EXPERT_BLOCK
    row_pair = jnp.full((n_rows,), -1, jnp.int32).at[dest].set(jnp.arange(tk, dtype=jnp.int32))
    is_pad = (row_pair < 0).astype(jnp.int32)
    row_tok = jnp.where(row_pair < 0, 0, row_pair // TOP_K)
    pad_slot = tk + jnp.cumsum(is_pad) - is_pad
    row_slot = jnp.where(row_pair < 0, pad_slot, (row_pair % TOP_K) * t + row_pair // TOP_K).astype(jnp.int32)
    block_start = jnp.arange(n_blocks, dtype=jnp.int32) * EXPERT_BLOCK
    block_e = jnp.sum((pend[None, :] <= block_start[:, None]).astype(jnp.int32), axis=1)
    block_e = jnp.minimum(block_e, N_EXPERTS - 1).astype(jnp.int32)
    return row_tok, row_slot, n_rows, block_e


def kernel(x, p, positions, w_in, pool_w, pool_scale, sg_ln_g, sg_ln_b, sg_w, sg_b, cmp_k_pos, cmp_k_w1, cmp_k_w2, cmp_v_pos, cmp_v_w1, cmp_v_w2, w_pool_out, w_sg_out, w_nsa_out, w_out, ln1_g, ln1_b, ffn_w_gate, ffn_w_up, ffn_w_down, moe_router, moe_router_b, moe_w_gate, moe_w_up, moe_w_down, ple_gate_w, ple_gate_b, ple_proj, ln2_g, ln2_b):
    b, seq, d = x.shape
    depth = w_in.shape[0]
    t = b * seq
    alpha = (2 * depth) ** 0.25
    tm = 512
    tabs = _rope_tables(positions)
    x2 = x.reshape(t, d)
    for i in range(depth):
        mix, br, qT, keys, cmp_src, valT, gates = _proj(x2, _prep_w_in(w_in[i]), tabs, b, seq, tm)
        pm, sgm = _mixers(mix, pool_w[i], pool_scale[i], sg_ln_g[i], sg_ln_b[i], sg_w[i], sg_b[i], seq, tm)
        cmp, cmpT = _compress(cmp_src, *_prep_compress(cmp_k_pos[i], cmp_k_w1[i], cmp_k_w2[i],
                                                       cmp_v_pos[i], cmp_v_w1[i], cmp_v_w2[i]))
        on = _nsa(qT, cmp, cmpT, keys, valT, gates, seq)
        j = i // 2
        moe = i % 2 == 1
        merged = _merge(pm, sgm, on, br, x2, w_pool_out[i], w_sg_out[i], w_nsa_out[i], w_out[i],
                        ln1_g[i], ln1_b[i], alpha, tm, router=(moe_router[j], moe_router_b[j]) if moe else None)
        if moe:
            x1, route, x1_tiles = merged
            row_tok, row_slot, n_slots, block_e = _route_tables(route, t)
            y = _moe(x1_tiles, row_tok, row_slot, n_slots, block_e, moe_w_gate[j].astype(BF16),
                     moe_w_up[j].astype(BF16), moe_w_down[j].astype(BF16))
        else:
            x1, route = merged, None
            y = _swiglu(x1, ffn_w_gate[j].astype(BF16), ffn_w_up[j].astype(BF16), ffn_w_down[j].astype(BF16), tm)
        x2 = _ple_ln2(x1, p[i].reshape(t, PLE_DIM), y, route, ple_gate_w[i], ple_gate_b[i], ple_proj[i],
                      ln2_g[i], ln2_b[i], alpha, tm)
    return x2.reshape(b, seq, d)
```
